```python
import math
import jax, jax.numpy as jnp
from jax import lax
import numpy as np

D_MODEL = 2048
BATCH = 4
SEQ = 4096
DEPTH = 2

RMS_EPS = 1e-6
ROPE_THETA = 500000.0
HEAD_DIM = 128
ROT_DIMS = HEAD_DIM // 4
Q_BLOCK = 128

NSA_HEADS = 8
NSA_GROUPS = 2
NSA_HPG = NSA_HEADS // NSA_GROUPS
NSA_CMP_LEN = 32
NSA_CMP_STRIDE = 16
NSA_CMP_HIDDEN = 256
NSA_SEL_BLOCK = 64
NSA_SEL_TOPK = 16
NSA_WINDOW = 512
NSA_SEL_QCHUNK = 32
NSA_WIDTH = NSA_HEADS * HEAD_DIM
NSA_KV = NSA_GROUPS * HEAD_DIM
NSA_COLS = (NSA_WIDTH, NSA_KV, NSA_KV, NSA_KV, NSA_KV, NSA_KV, NSA_KV, 3 * NSA_HEADS)
NSA_IN = NSA_WIDTH + 6 * NSA_KV + 3 * NSA_HEADS

DIFF_HEADS = 4
DIFF_VDIM = 2 * HEAD_DIM
DIFF_WIDTH = DIFF_HEADS * DIFF_VDIM
DIFF_QK = 2 * DIFF_HEADS * HEAD_DIM
DIFF_COLS = (DIFF_QK, DIFF_QK, DIFF_WIDTH)
DIFF_IN = 2 * DIFF_QK + DIFF_WIDTH

RWKV_HEAD = 64
RWKV_WIDTH = 1024
RWKV_HEADS = RWKV_WIDTH // RWKV_HEAD
RWKV_W_LORA = 64
RWKV_A_LORA = 64
RWKV_G_LORA = 160
RWKV_LNX_EPS = 64e-5
RWKV_COLS = (RWKV_WIDTH, RWKV_WIDTH, RWKV_WIDTH, RWKV_W_LORA, RWKV_A_LORA, RWKV_G_LORA)
RWKV_IN = 3 * RWKV_WIDTH + RWKV_W_LORA + RWKV_A_LORA + RWKV_G_LORA

IN_COLS = NSA_IN + DIFF_IN + RWKV_IN
N_BRANCH = 3
BRANCH_WIDTH = 1024
D_FF = 4 * D_MODEL

kernel_name = "hybrid_nsa_diff_rwkv7_gated_block"


def _split(z, sizes):
    offs = np.cumsum(sizes)[:-1].tolist()
    return jnp.split(z, offs, axis=-1)


def rmsnorm(z, g, eps=RMS_EPS):
    z32 = z.astype(jnp.float32)
    y = z32 * lax.rsqrt(jnp.mean(z32 * z32, axis=-1, keepdims=True) + eps)
    return (y * g.astype(jnp.float32)).astype(z.dtype)


def partial_rope(z, pos):
    half = ROT_DIMS // 2
    inv_freq = ROPE_THETA ** (-jnp.arange(half, dtype=jnp.float32) / half)
    ang = pos.astype(jnp.float32)[..., None] * inv_freq
    ang = ang.reshape(pos.shape + (1,) * (z.ndim - pos.ndim - 1) + (half,))
    cos, sin = jnp.cos(ang), jnp.sin(ang)
    zr = z[..., :ROT_DIMS].astype(jnp.float32)
    z1, z2 = zr[..., :half], zr[..., half:]
    rot = jnp.concatenate([z1 * cos - z2 * sin, z2 * cos + z1 * sin], axis=-1)
    return jnp.concatenate([rot.astype(z.dtype), z[..., ROT_DIMS:]], axis=-1)


def masked_softmax(s, mask):
    s = jnp.where(mask, s.astype(jnp.float32), -1e30)
    return jax.nn.softmax(s, axis=-1) * mask


def nsa_mixer(q, kc, vc, ks, vs, kw, vw, gates, pos, cmp_pos, cmp_w1, cmp_w2):
    B, S = q.shape[:2]
    dh = HEAD_DIM
    scale = dh ** -0.5
    t_idx = jnp.arange(S)
    qg = partial_rope(q, pos).reshape(B, S, NSA_GROUPS, NSA_HPG, dh).transpose(0, 2, 3, 1, 4)

    ratio = NSA_CMP_LEN // NSA_CMP_STRIDE
    n_chunk = S // NSA_CMP_STRIDE
    n_c = n_chunk - ratio + 1

    def compress(z, pe, w1, w2):
        ch = z.reshape(B, n_chunk, NSA_CMP_STRIDE, NSA_GROUPS, dh)
        blk = jnp.concatenate([ch[:, i:i + n_c] for i in range(ratio)], axis=2) + pe[None, None, :, None, :]
        flat = blk.transpose(0, 1, 3, 2, 4).reshape(B, n_c, NSA_GROUPS, NSA_CMP_LEN * dh)
        return jax.nn.gelu(flat @ w1) @ w2

    c_start = jnp.arange(n_c) * NSA_CMP_STRIDE
    cmp_end = c_start + NSA_CMP_LEN - 1
    kcmp = partial_rope(compress(kc, cmp_pos[0], cmp_w1[0], cmp_w2[0]), jnp.take(pos, cmp_end, axis=1))
    vcmp = compress(vc, cmp_pos[1], cmp_w1[1], cmp_w2[1])
    s_c = jnp.einsum('bghqd,bcgd->bghqc', qg, kcmp) * scale
    p_c = masked_softmax(s_c, cmp_end[None, :] <= t_idx[:, None])
    o_cmp = jnp.einsum('bghqc,bcgd->bghqd', p_c.astype(vcmp.dtype), vcmp)

    n_sel = S // NSA_SEL_BLOCK
    top = min(NSA_SEL_TOPK, n_sel)
    j_start = jnp.arange(n_sel) * NSA_SEL_BLOCK
    overlap = ((c_start[:, None] <= j_start[None, :] + NSA_SEL_BLOCK - 1)
               & (cmp_end[:, None] >= j_start[None, :])).astype(jnp.float32)
    imp = jnp.einsum('bghqc,cj->bgqj', p_c, overlap)
    jj = jnp.arange(n_sel)[None, :]
    blk_t = (t_idx // NSA_SEL_BLOCK)[:, None]
    forced = (jj == 0) | (jj == blk_t) | (jj == blk_t - 1)
    imp = jnp.where(forced, 1e9, jnp.where(jj > blk_t, -jnp.inf, imp))
    _, sel_idx = lax.top_k(imp, top)

    k_blk = partial_rope(ks, pos).reshape(B, n_sel, NSA_SEL_BLOCK, NSA_GROUPS, dh).transpose(0, 3, 1, 2, 4)
    v_blk = vs.reshape(B, n_sel, NSA_SEL_BLOCK, NSA_GROUPS, dh).transpose(0, 3, 1, 2, 4)
    n_qc = S // NSA_SEL_QCHUNK
    q_ch = jnp.moveaxis(qg.reshape(B, NSA_GROUPS, NSA_HPG, n_qc, NSA_SEL_QCHUNK, dh), 3, 0)
    i_ch = jnp.moveaxis(sel_idx.reshape(B, NSA_GROUPS, n_qc, NSA_SEL_QCHUNK, top), 2, 0)
    t_ch = t_idx.reshape(n_qc, NSA_SEL_QCHUNK)
    gather = jax.vmap(jax.vmap(lambda blk, ix: blk[ix]))

    def sel_chunk(args):
        qc, ic, tc = args
        kg = gather(k_blk, ic)
        vg = gather(v_blk, ic)
        s = jnp.einsum('bghqd,bgqnld->bghqnl', qc, kg) * scale
        kpos = ic[..., None] * NSA_SEL_BLOCK + jnp.arange(NSA_SEL_BLOCK)
        mask = (kpos <= tc[None, None, :, None, None]).reshape(B, NSA_GROUPS, 1, NSA_SEL_QCHUNK, top * NSA_SEL_BLOCK)
        p = masked_softmax(s.reshape(B, NSA_GROUPS, NSA_HPG, NSA_SEL_QCHUNK, top * NSA_SEL_BLOCK), mask)
        p = p.reshape(B, NSA_GROUPS, NSA_HPG, NSA_SEL_QCHUNK, top, NSA_SEL_BLOCK)
        return jnp.einsum('bghqnl,bgqnld->bghqd', p.astype(vg.dtype), vg)

    o_slc = jnp.moveaxis(lax.map(sel_chunk, (q_ch, i_ch, t_ch)), 0, 3).reshape(B, NSA_GROUPS, NSA_HPG, S, dh)

    pad = ((0, 0), (0, 0), (NSA_WINDOW, 0), (0, 0))
    kw_p = jnp.pad(partial_rope(kw, pos).transpose(0, 2, 1, 3), pad)
    vw_p = jnp.pad(vw.transpose(0, 2, 1, 3), pad)
    n_qb = S // Q_BLOCK
    span = Q_BLOCK + NSA_WINDOW
    q_bl = jnp.moveaxis(qg.reshape(B, NSA_GROUPS, NSA_HPG, n_qb, Q_BLOCK, dh), 3, 0)

    def win_block(args):
        qb, i = args
        start = i * Q_BLOCK
        kb = lax.dynamic_slice_in_dim(kw_p, start, span, axis=2)
        vb = lax.dynamic_slice_in_dim(vw_p, start, span, axis=2)
        tq = start + jnp.arange(Q_BLOCK)
        tk = start - NSA_WINDOW + jnp.arange(span)
        dlt = tq[:, None] - tk[None, :]
        mask = (dlt >= 0) & (dlt < NSA_WINDOW) & (tk[None, :] >= 0)
        p = masked_softmax(jnp.einsum('bghqd,bgkd->bghqk', qb, kb) * scale, mask)
        return jnp.einsum('bghqk,bgkd->bghqd', p.astype(vb.dtype), vb)

    o_win = jnp.moveaxis(lax.map(win_block, (q_bl, jnp.arange(n_qb))), 0, 3).reshape(B, NSA_GROUPS, NSA_HPG, S, dh)

    g = jax.nn.sigmoid(gates.astype(jnp.float32)).reshape(B, S, NSA_GROUPS, NSA_HPG, 3).transpose(0, 2, 3, 1, 4)
    o = g[..., 0:1] * o_cmp + g[..., 1:2] * o_slc + g[..., 2:3] * o_win
    return o.transpose(0, 3, 1, 2, 4).reshape(B, S, NSA_WIDTH).astype(q.dtype)


def diff_mixer(q, k, v, pos, lam, subln, layer):
    B, S = q.shape[:2]
    dh = HEAD_DIM
    q = partial_rope(q.reshape(B, S, DIFF_HEADS, 2, dh), pos).transpose(0, 2, 3, 1, 4)
    k = partial_rope(k.reshape(B, S, DIFF_HEADS, 2, dh), pos).transpose(0, 2, 3, 1, 4)
    v = v.reshape(B, S, DIFF_HEADS, DIFF_VDIM).transpose(0, 2, 1, 3)
    lam_init = 0.8 - 0.6 * math.exp(-0.3 * layer)
    l32 = lam.astype(jnp.float32)
    lam_full = jnp.exp(jnp.sum(l32[0] * l32[1])) - jnp.exp(jnp.sum(l32[2] * l32[3])) + lam_init
    scale = dh ** -0.5
    n_qb = S // Q_BLOCK
    q_bl = jnp.moveaxis(q.reshape(B, DIFF_HEADS, 2, n_qb, Q_BLOCK, dh), 3, 0)
    tk = jnp.arange(S)

    def blk(args):
        qb, i = args
        tq = i * Q_BLOCK + jnp.arange(Q_BLOCK)
        p = masked_softmax(jnp.einsum('bhmqd,bhmkd->bhmqk', qb, k) * scale, tk[None, :] <= tq[:, None])
        a = p[:, :, 0] - lam_full * p[:, :, 1]
        return jnp.einsum('bhqk,bhkd->bhqd', a.astype(v.dtype), v)

    o = jnp.moveaxis(lax.map(blk, (q_bl, jnp.arange(n_qb))), 0, 2).reshape(B, DIFF_HEADS, S, DIFF_VDIM)
    o = rmsnorm(o, subln, 1e-5) * (1.0 - lam_init)
    return o.transpose(0, 2, 1, 3).reshape(B, S, DIFF_WIDTH)


def rwkv_mixer(p, mu, w0, w_up, a0, a_up, g_up, k_k, k_a, r_k, lnx_w, lnx_b):
    B, S = p.shape[:2]
    f32 = jnp.float32
    H, N = RWKV_HEADS, RWKV_HEAD
    prev = jnp.pad(p, ((0, 0), (1, 0), (0, 0)))[:, :-1]
    p = p + (prev - p) * mu
    r, k, v, wd, ad, gd = _split(p, RWKV_COLS)
    w = jnp.exp(-math.exp(-0.5) * jax.nn.sigmoid((w0 + jnp.tanh(wd) @ w_up).astype(f32)))
    a = jax.nn.sigmoid((a0 + ad @ a_up).astype(f32))
    g = jax.nn.sigmoid(gd) @ g_up
    r, k, v = r.astype(f32), k.astype(f32), v.astype(f32)
    heads = lambda z: z.reshape(B, S, H, N)
    kk = heads(k * k_k)
    kk = kk / jnp.maximum(jnp.sqrt(jnp.sum(kk * kk, axis=-1, keepdims=True)), 1e-12)
    k = k * (1.0 + (a - 1.0) * k_a)
    xs = tuple(jnp.moveaxis(z, 1, 0) for z in (heads(r), heads(w), heads(k), heads(v), kk, heads(a)))

    def step(state, inp):
        rt, wt, kt, vt, kkt, at = inp
        sa = jnp.einsum('bhvk,bhk->bhv', state, -kkt)
        state = state * wt[:, :, None, :] + sa[..., None] * (kkt * at)[:, :, None, :] + vt[..., None] * kt[:, :, None, :]
        return state, jnp.einsum('bhvk,bhk->bhv', state, rt)

    _, y = lax.scan(step, jnp.zeros((B, H, N, N), f32), xs)
    y = jnp.moveaxis(y, 0, 1)
    mean = jnp.mean(y, axis=-1, keepdims=True)
    var = jnp.mean(jnp.square(y - mean), axis=-1, keepdims=True)
    y = ((y - mean) * lax.rsqrt(var + RWKV_LNX_EPS)).reshape(B, S, RWKV_WIDTH) * lnx_w + lnx_b
    bonus = jnp.sum(heads(r) * heads(k) * r_k, axis=-1, keepdims=True) * heads(v)
    return ((y + bonus.reshape(B, S, RWKV_WIDTH)) * g).astype(p.dtype)


def setup_inputs(seed: int = 0) -> dict:
    key = jax.random.key(seed)
    ks = iter(jax.random.split(key, 32))
    L, D = DEPTH, D_MODEL
    nrm = lambda shape, scale: scale * jax.random.normal(next(ks), shape, jnp.float32)
    gain = lambda shape: 1.0 + 0.05 * jax.random.normal(next(ks), shape, jnp.float32)
    return {
        "x": nrm((BATCH, SEQ, D), 1.0),
        "positions": jnp.broadcast_to(jnp.arange(SEQ, dtype=jnp.int32), (BATCH, SEQ)),
        "norm_pre_mix": gain((L, D)),
        "norm_post_mix": gain((L, D)),
        "norm_pre_mlp": gain((L, D)),
        "norm_post_mlp": gain((L, D)),
        "w_in": nrm((L, D, IN_COLS), D ** -0.5),
        "nsa_cmp_pos": nrm((L, 2, NSA_CMP_LEN, HEAD_DIM), 0.1),
        "nsa_cmp_w1": nrm((L, 2, NSA_CMP_LEN * HEAD_DIM, NSA_CMP_HIDDEN), (NSA_CMP_LEN * HEAD_DIM) ** -0.5),
        "nsa_cmp_w2": nrm((L, 2, NSA_CMP_HIDDEN, HEAD_DIM), NSA_CMP_HIDDEN ** -0.5),
        "diff_lambda": nrm((L, 4, HEAD_DIM), 0.1),
        "diff_subln": gain((L, DIFF_VDIM)),
        "rwkv_mu": jax.random.uniform(next(ks), (L, RWKV_IN), jnp.float32),
        "rwkv_w0": jax.random.uniform(next(ks), (L, RWKV_WIDTH), jnp.float32, -4.0, 2.0),
        "rwkv_w_up": nrm((L, RWKV_W_LORA, RWKV_WIDTH), 0.1),
        "rwkv_a0": nrm((L, RWKV_WIDTH), 0.5),
        "rwkv_a_up": nrm((L, RWKV_A_LORA, RWKV_WIDTH), 0.1),
        "rwkv_g_up": nrm((L, RWKV_G_LORA, RWKV_WIDTH), RWKV_G_LORA ** -0.5),
        "rwkv_k_k": 0.85 + nrm((L, RWKV_WIDTH), 0.05),
        "rwkv_k_a": gain((L, RWKV_WIDTH)),
        "rwkv_r_k": nrm((L, RWKV_HEADS, RWKV_HEAD), 0.1),
        "rwkv_lnx_w": gain((L, RWKV_WIDTH)),
        "rwkv_lnx_b": nrm((L, RWKV_WIDTH), 0.01),
        "w_gate": nrm((L, N_BRANCH, D, D), D ** -0.5),
        "b_gate": nrm((L, N_BRANCH, D), 0.02),
        "w_branch": nrm((L, N_BRANCH, BRANCH_WIDTH, D), BRANCH_WIDTH ** -0.5),
        "w_out": nrm((L, D, D), D ** -0.5),
        "w_up": nrm((L, D, D_FF), D ** -0.5),
        "w_down": nrm((L, D_FF, D), D_FF ** -0.5),
    }


def reference(x, positions, norm_pre_mix, norm_post_mix, norm_pre_mlp, norm_post_mlp, w_in,
              nsa_cmp_pos, nsa_cmp_w1, nsa_cmp_w2, diff_lambda, diff_subln,
              rwkv_mu, rwkv_w0, rwkv_w_up, rwkv_a0, rwkv_a_up, rwkv_g_up, rwkv_k_k, rwkv_k_a, rwkv_r_k,
              rwkv_lnx_w, rwkv_lnx_b, w_gate, b_gate, w_branch, w_out, w_up, w_down):
    B, S, _ = x.shape
    for l in range(DEPTH):
        u = rmsnorm(x, norm_pre_mix[l])
        p_nsa, p_diff, p_rwkv = _split(u @ w_in[l], (NSA_IN, DIFF_IN, RWKV_IN))
        q, kc, vc, ks, vs, kw, vw, ng = _split(p_nsa, NSA_COLS)
        grp = lambda z: z.reshape(B, S, NSA_GROUPS, HEAD_DIM)
        y_nsa = nsa_mixer(q.reshape(B, S, NSA_HEADS, HEAD_DIM), grp(kc), grp(vc), grp(ks), grp(vs), grp(kw), grp(vw),
                          ng, positions, nsa_cmp_pos[l], nsa_cmp_w1[l], nsa_cmp_w2[l])
        dq, dk, dv = _split(p_diff, DIFF_COLS)
        y_diff = diff_mixer(dq, dk, dv, positions, diff_lambda[l], diff_subln[l], l)
        y_rwkv = rwkv_mixer(p_rwkv, rwkv_mu[l], rwkv_w0[l], rwkv_w_up[l], rwkv_a0[l], rwkv_a_up[l], rwkv_g_up[l],
                            rwkv_k_k[l], rwkv_k_a[l], rwkv_r_k[l], rwkv_lnx_w[l], rwkv_lnx_b[l])
        gate = lambda b: jax.nn.sigmoid((u @ w_gate[l, b] + b_gate[l, b]).astype(jnp.float32)).astype(x.dtype)
        merged = (gate(0) * (y_nsa @ w_branch[l, 0])
                  + gate(1) * (y_diff @ w_branch[l, 1])
                  + gate(2) * (y_rwkv @ w_branch[l, 2]))
        x = x + rmsnorm(merged @ w_out[l], norm_post_mix[l])
        m = rmsnorm(x, norm_pre_mlp[l])
        x = x + rmsnorm(jnp.square(jax.nn.relu(m @ w_up[l])) @ w_down[l], norm_post_mlp[l])
    return x
```

```python
import functools
import math

import jax
import jax.numpy as jnp
import numpy as np
from jax import lax
from jax.experimental import pallas as pl
from jax.experimental.pallas import tpu as pltpu

F32 = jnp.float32
BF16 = jnp.bfloat16

D_MODEL = 2048
RMS_EPS = 1e-6
ROPE_THETA = 500000.0
HEAD_DIM = 128
ROT_HALF = HEAD_DIM // 8
NSA_HEADS = 8
NSA_GROUPS = 2
NSA_HPG = NSA_HEADS // NSA_GROUPS
NSA_CMP_LEN = 32
NSA_CMP_STRIDE = 16
NSA_CMP_HIDDEN = 256
NSA_SEL_BLOCK = 64
SEL_SHIFT = NSA_SEL_BLOCK.bit_length() - 1
NSA_SEL_TOPK = 16
NSA_WINDOW = 512
NSA_WIDTH = NSA_HEADS * HEAD_DIM
NSA_KV = NSA_GROUPS * HEAD_DIM
DIFF_HEADS = 4
DIFF_VDIM = 2 * HEAD_DIM
DIFF_WIDTH = DIFF_HEADS * DIFF_VDIM
DIFF_QK = 2 * DIFF_HEADS * HEAD_DIM
RWKV_HEAD = 64
RWKV_WIDTH = 1024
RWKV_W_LORA = 64
RWKV_A_LORA = 64
RWKV_G_LORA = 160
RWKV_LNX_EPS = 64e-5
N_BRANCH = 3
BRANCH_WIDTH = 1024

LANES = 128
SUBLANES = 8
VMEM_LIMIT_BYTES = 56 * 1024 * 1024

NEG = -1e30

ROPE_Q, ROPE_DQ, ROPE_KS, ROPE_KW, ROPE_DK = 0, 1024, 2048, 2304, 2560
ROPE_COLS = 3584
PL_KC, PL_VC, PL_VS, PL_VW, PL_DV = 0, 256, 512, 768, 1024
PLAIN_COLS = 2048
RW_LORA_PAD = 128
RW_G_PAD = 256
RW_COLS = 3 * RWKV_WIDTH + RW_LORA_PAD + RW_G_PAD
F_RW, F_NG, F_BG = 0, RW_COLS, RW_COLS + 128
F32_COLS = RW_COLS + 128 + N_BRANCH * D_MODEL


def _cparams(sem):
    return pltpu.CompilerParams(dimension_semantics=sem, vmem_limit_bytes=VMEM_LIMIT_BYTES)


def _bdot(a, b):
    return jnp.dot(a.astype(BF16), b.astype(BF16), preferred_element_type=F32)


def _dot_nt(a, b):
    return lax.dot_general(a.astype(BF16), b.astype(BF16), (((1,), (1,)), ((), ())),
                           preferred_element_type=F32)


def _dot_tn(a, b):
    return lax.dot_general(a.astype(BF16), b.astype(BF16), (((0,), (0,)), ((), ())),
                           preferred_element_type=F32)


def _split2(x):
    hi = x.astype(BF16)
    lo = (x - hi.astype(F32)).astype(BF16)
    return hi, lo


def _split3(x):
    hi = x.astype(BF16)
    r1 = x - hi.astype(F32)
    mid = r1.astype(BF16)
    lo = (r1 - mid.astype(F32)).astype(BF16)
    return hi, mid, lo


def _dot_exact_lhs(a_bf, x):
    hi, mid, lo = _split3(x)
    d = lambda p: jnp.dot(a_bf, p, preferred_element_type=F32)
    return d(hi) + (d(mid) + d(lo))


def _dot_exact_rhs(x, b_bf):
    hi, mid, lo = _split3(x)
    d = lambda p: jnp.dot(p, b_bf, preferred_element_type=F32)
    return d(hi) + (d(mid) + d(lo))


def _dot3(a, b):
    ah, al = _split2(a)
    bh, bl = _split2(b)
    d = lambda p, q: jnp.dot(p, q, preferred_element_type=F32)
    return d(ah, bh) + (d(ah, bl) + d(al, bh))


def _rope_partner(z, lane):
    return jnp.where(lane < ROT_HALF, pltpu.roll(z, LANES - ROT_HALF, 1), pltpu.roll(z, ROT_HALF, 1))


def _rope_table_kernel(pos_ref, invf_ref, sign_ref, cos_ref, sin_ref):
    ang = pos_ref[...].astype(F32) * invf_ref[...]
    cos_ref[...] = jnp.cos(ang)
    sin_ref[...] = jnp.sin(ang) * sign_ref[...]


def _rope_tables(pos_flat):
    n = pos_flat.shape[0]
    half = ROT_HALF
    inv_freq = ROPE_THETA ** (-jnp.arange(half, dtype=F32) / half)
    zeros = jnp.zeros((LANES - 2 * half,), F32)
    invf = jnp.concatenate([inv_freq, inv_freq, zeros])[None, :]
    sign = jnp.concatenate([-jnp.ones((half,), F32), jnp.ones((half,), F32), zeros])[None, :]
    tm = min(n, 2048)
    assert n % tm == 0
    vec = pl.BlockSpec((1, LANES), lambda i: (0, 0))
    out = pl.BlockSpec((tm, LANES), lambda i: (i, 0))
    return pl.pallas_call(
        _rope_table_kernel,
        out_shape=(jax.ShapeDtypeStruct((n, LANES), F32),) * 2,
        grid=(n // tm,),
        in_specs=[pl.BlockSpec((tm, 1), lambda i: (i, 0)), vec, vec],
        out_specs=(out, out),
        compiler_params=_cparams(("parallel",)),
        name="rope_tables",
    )(pos_flat[:, None], invf, sign)


def _norm_matmul_kernel(x_ref, g_ref, w_ref, *rest, mode, tn):
    if mode == "rope":
        cos_ref, sin_ref, cs_ref, o_ref, u_ref = rest
    else:
        o_ref, u_ref = rest

    @pl.when(pl.program_id(1) == 0)
    def _():
        x = x_ref[...]
        ms = jnp.mean(x * x, axis=-1, keepdims=True)
        u_ref[...] = (x * lax.rsqrt(ms + RMS_EPS) * g_ref[...]).astype(BF16)

    acc = jnp.dot(u_ref[...], w_ref[...], preferred_element_type=F32)
    if mode == "rope":
        acc = acc * cs_ref[...]
        cosv = cos_ref[...]
        sinv = sin_ref[...]
        lane = lax.broadcasted_iota(jnp.int32, cosv.shape, 1)
        for h in range(tn // LANES):
            z = acc[:, h * LANES:(h + 1) * LANES]
            o_ref[:, h * LANES:(h + 1) * LANES] = (z * cosv + _rope_partner(z, lane) * sinv).astype(o_ref.dtype)
    elif mode == "relu2":
        o_ref[...] = jnp.square(jnp.maximum(acc, 0.0)).astype(o_ref.dtype)
    else:
        o_ref[...] = acc.astype(o_ref.dtype)


def _norm_matmul(x, g, w, mode, out_dtype, rope=None, tm=1024, tn=512):
    m, d = x.shape
    n = w.shape[1]
    tm = min(tm, m)
    assert m % tm == 0 and n % tn == 0
    in_specs = [
        pl.BlockSpec((tm, d), lambda i, j: (i, 0)),
        pl.BlockSpec((1, d), lambda i, j: (0, 0)),
        pl.BlockSpec((d, tn), lambda i, j: (0, j)),
    ]
    args = [x, g[None, :], w]
    if mode == "rope":
        cosf, sinf, colscale = rope
        in_specs += [
            pl.BlockSpec((tm, LANES), lambda i, j: (i, 0)),
            pl.BlockSpec((tm, LANES), lambda i, j: (i, 0)),
            pl.BlockSpec((1, tn), lambda i, j: (0, j)),
        ]
        args += [cosf, sinf, colscale]
    return pl.pallas_call(
        functools.partial(_norm_matmul_kernel, mode=mode, tn=tn),
        out_shape=jax.ShapeDtypeStruct((m, n), out_dtype),
        grid=(m // tm, n // tn),
        in_specs=in_specs,
        out_specs=pl.BlockSpec((tm, tn), lambda i, j: (i, j)),
        scratch_shapes=[pltpu.VMEM((tm, d), BF16)],
        compiler_params=_cparams(("parallel", "arbitrary")),
        name="norm_matmul_" + mode,
    )(*args)


def _matmul_norm_res_kernel(a_ref, w_ref, g_ref, res_ref, o_ref, acc_ref):
    k = pl.program_id(1)

    @pl.when(k == 0)
    def _():
        acc_ref[...] = jnp.zeros_like(acc_ref)

    acc_ref[...] += jnp.dot(a_ref[...], w_ref[...], preferred_element_type=F32)

    @pl.when(k == pl.num_programs(1) - 1)
    def _():
        y = acc_ref[...]
        ms = jnp.mean(y * y, axis=-1, keepdims=True)
        o_ref[...] = res_ref[...] + y * lax.rsqrt(ms + RMS_EPS) * g_ref[...]


def _matmul_norm_res(a, w, g, res, tm=512, tk=1024):
    m, kdim = a.shape
    n = w.shape[1]
    tm = min(tm, m)
    assert m % tm == 0 and kdim % tk == 0
    return pl.pallas_call(
        _matmul_norm_res_kernel,
        out_shape=jax.ShapeDtypeStruct((m, n), F32),
        grid=(m // tm, kdim // tk),
        in_specs=[
            pl.BlockSpec((tm, tk), lambda i, k: (i, k)),
            pl.BlockSpec((tk, n), lambda i, k: (k, 0)),
            pl.BlockSpec((1, n), lambda i, k: (0, 0)),
            pl.BlockSpec((tm, n), lambda i, k: (i, 0)),
        ],
        out_specs=pl.BlockSpec((tm, n), lambda i, k: (i, 0)),
        scratch_shapes=[pltpu.VMEM((tm, n), F32)],
        compiler_params=_cparams(("parallel", "arbitrary")),
        name="matmul_norm_res",
    )(a, w, g[None, :], res)


def _gelu_tanh(x):
    return 0.5 * x * (1.0 + jnp.tanh(math.sqrt(2.0 / math.pi) * (x + 0.044715 * (x * x * x))))


def _nsa_compress_kernel(x_ref, w1_ref, w2_ref, pe_ref, cos_ref, sin_ref, o_ref, *, use_rope):
    x = x_ref[0, 0]
    half = x.shape[1]
    n_chunk = x.shape[0]
    a = jnp.dot(x, w1_ref[0:half, :], preferred_element_type=F32)
    b = jnp.dot(x, w1_ref[half:2 * half, :], preferred_element_type=F32)
    pe = jnp.broadcast_to(pe_ref[...], (SUBLANES, pe_ref.shape[1]))
    peb = _dot_exact_rhs_general(pe, w1_ref[...])[0:1, :]
    h = a + pltpu.roll(b, n_chunk - 1, 0) + peb
    y = _bdot(_gelu_tanh(h), w2_ref[...])
    if use_rope:
        lane = lax.broadcasted_iota(jnp.int32, y.shape, 1)
        y = y * cos_ref[0] + _rope_partner(y, lane) * sin_ref[0]
    o_ref[0, 0] = y.astype(o_ref.dtype)


def _dot_exact_rhs_general(x, w_bf):
    hi, mid, lo = _split3(x)
    d = lambda p: jnp.dot(p, w_bf, preferred_element_type=F32)
    return d(hi) + (d(mid) + d(lo))


def _nsa_compress(x2, w1, w2, pe_flat, cosc, sinc, use_rope):
    b, g, n_chunk, width = x2.shape
    dh = w2.shape[1]
    return pl.pallas_call(
        functools.partial(_nsa_compress_kernel, use_rope=use_rope),
        out_shape=jax.ShapeDtypeStruct((b, g, n_chunk, dh), BF16),
        grid=(b, g),
        in_specs=[
            pl.BlockSpec((1, 1, n_chunk, width), lambda i, j: (i, j, 0, 0)),
            pl.BlockSpec(w1.shape, lambda i, j: (0, 0)),
            pl.BlockSpec(w2.shape, lambda i, j: (0, 0)),
            pl.BlockSpec(pe_flat.shape, lambda i, j: (0, 0)),
            pl.BlockSpec((1, n_chunk, dh), lambda i, j: (i, 0, 0)),
            pl.BlockSpec((1, n_chunk, dh), lambda i, j: (i, 0, 0)),
        ],
        out_specs=pl.BlockSpec((1, 1, n_chunk, dh), lambda i, j: (i, j, 0, 0)),
        compiler_params=_cparams(("parallel", "parallel")),
        name="nsa_compress",
    )(x2, w1, w2, pe_flat, cosc, sinc)


def _nsa_cmp_kernel(q_ref, kc_ref, vc_ref, ov_ref, o_ref, sel_ref, *, tq, top):
    i = pl.program_id(2)
    kc = kc_ref[0, 0]
    vc = vc_ref[0, 0]
    ncp = kc.shape[0]
    n_sel = sel_ref.shape[3]
    t = i * tq + lax.broadcasted_iota(jnp.int32, (tq, ncp), 0)
    c = lax.broadcasted_iota(jnp.int32, (tq, ncp), 1)
    valid = (c * NSA_CMP_STRIDE + (NSA_CMP_LEN - 1)) <= t
    psum = jnp.zeros((tq, ncp), F32)
    for h in range(NSA_HPG):
        q = q_ref[0, :, h * HEAD_DIM:(h + 1) * HEAD_DIM]
        s = jnp.where(valid, _dot_nt(q, kc), NEG)
        m = jnp.max(s, axis=1, keepdims=True)
        e = jnp.where(valid, jnp.exp(s - m), 0.0)
        l = jnp.sum(e, axis=1, keepdims=True)
        p = e / jnp.where(l > 0.0, l, 1.0)
        o_ref[0, :, h * HEAD_DIM:(h + 1) * HEAD_DIM] = _bdot(p, vc).astype(o_ref.dtype)
        psum = psum + p
    imp = _dot_exact_rhs(psum, ov_ref[...])
    jj = lax.broadcasted_iota(jnp.int32, (tq, n_sel), 1)
    blk_t = jnp.right_shift(i * tq + lax.broadcasted_iota(jnp.int32, (tq, n_sel), 0), SEL_SHIFT)
    forced = (jj == 0) | (jj == blk_t) | (jj == blk_t - 1)
    imp = jnp.where(forced, 1e9, jnp.where(jj > blk_t, -1.0, imp))
    rank = jnp.zeros((tq, n_sel), F32)
    for ii in range(n_sel):
        col = imp[:, ii:ii + 1]
        beats = (col > imp) | ((col == imp) & (jj > ii))
        rank = rank + jnp.where(beats, 1.0, 0.0)
    sel_ref[0, 0] = jnp.where(rank < float(top), 1.0, 0.0).astype(sel_ref.dtype)


def _nsa_cmp(proj_rope, kcmp, vcmp, overlap, tq):
    b, s, _ = proj_rope.shape
    g = kcmp.shape[1]
    ncp = kcmp.shape[2]
    n_sel = overlap.shape[1]
    top = min(NSA_SEL_TOPK, n_sel)
    qw = NSA_HPG * HEAD_DIM
    return pl.pallas_call(
        functools.partial(_nsa_cmp_kernel, tq=tq, top=top),
        out_shape=(jax.ShapeDtypeStruct((b, s, NSA_WIDTH), F32),
                   jax.ShapeDtypeStruct((b, g, s, n_sel), BF16)),
        grid=(b, g, s // tq),
        in_specs=[
            pl.BlockSpec((1, tq, qw), lambda bi, gi, i: (bi, i, ROPE_Q // qw + gi)),
            pl.BlockSpec((1, 1, ncp, HEAD_DIM), lambda bi, gi, i: (bi, gi, 0, 0)),
            pl.BlockSpec((1, 1, ncp, HEAD_DIM), lambda bi, gi, i: (bi, gi, 0, 0)),
            pl.BlockSpec(overlap.shape, lambda bi, gi, i: (0, 0)),
        ],
        out_specs=(pl.BlockSpec((1, tq, qw), lambda bi, gi, i: (bi, i, gi)),
                   pl.BlockSpec((1, 1, tq, n_sel), lambda bi, gi, i: (bi, gi, i, 0))),
        compiler_params=_cparams(("parallel", "parallel", "parallel")),
        name="nsa_cmp_select",
    )(proj_rope, kcmp, vcmp, overlap)


def _nsa_flash_kernel(*refs, mode, tq, tk, nkb):
    if mode == "sel":
        q_ref, k_ref, v_ref, sel_ref, o_ref, m_ref, l_ref, acc_ref = refs
    else:
        q_ref, k_ref, v_ref, o_ref, m_ref, l_ref, acc_ref = refs
    i = pl.program_id(2)
    kk = pl.program_id(3)
    if mode == "sel":
        kb = kk
        live = kb <= i
    else:
        kb = i - (nkb - 1) + kk
        live = kb >= 0

    @pl.when(kk == 0)
    def _():
        m_ref[...] = jnp.full_like(m_ref, NEG)
        l_ref[...] = jnp.zeros_like(l_ref)
        acc_ref[...] = jnp.zeros_like(acc_ref)

    @pl.when(live)
    def _():
        rows = i * tq + lax.broadcasted_iota(jnp.int32, (tq, tk), 0)
        cols = kb * tk + lax.broadcasted_iota(jnp.int32, (tq, tk), 1)
        d = rows - cols
        valid = d >= 0
        if mode == "win":
            valid = valid & (d < NSA_WINDOW)
        else:
            n_sel = sel_ref.shape[3]
            jrow = lax.broadcasted_iota(jnp.int32, (n_sel, tk), 0)
            jcol = kb * (tk // NSA_SEL_BLOCK) + jnp.right_shift(
                lax.broadcasted_iota(jnp.int32, (n_sel, tk), 1), SEL_SHIFT)
            expand = jnp.where(jrow == jcol, 1.0, 0.0).astype(BF16)
            chosen = jnp.dot(sel_ref[0, 0], expand, preferred_element_type=F32)
            valid = valid & (chosen > 0.5)
        k = k_ref[0]
        v = v_ref[0]
        for h in range(NSA_HPG):
            q = q_ref[0, :, h * HEAD_DIM:(h + 1) * HEAD_DIM]
            s = jnp.where(valid, _dot_nt(q, k), NEG)
            m_old = m_ref[h]
            m_new = jnp.maximum(m_old, jnp.max(s, axis=1, keepdims=True))
            alpha = jnp.exp(m_old - m_new)
            p = jnp.exp(s - m_new)
            l_ref[h] = alpha * l_ref[h] + jnp.sum(p, axis=1, keepdims=True)
            acc_ref[h] = alpha * acc_ref[h] + _bdot(p, v)
            m_ref[h] = m_new

    @pl.when(kk == pl.num_programs(3) - 1)
    def _():
        for h in range(NSA_HPG):
            o_ref[0, :, h * HEAD_DIM:(h + 1) * HEAD_DIM] = (acc_ref[h] / l_ref[h]).astype(o_ref.dtype)


def _nsa_flash(proj_rope, proj_plain, sel, mode, tq, tk):
    b, s, _ = proj_rope.shape
    g = NSA_GROUPS
    qw = NSA_HPG * HEAD_DIM
    nq = s // tq
    if mode == "sel":
        assert tq == tk
        nkb = nq
        kidx = lambda i, kk: jnp.minimum(kk, i)
        kcol, vcol = ROPE_KS // HEAD_DIM, PL_VS // HEAD_DIM
    else:
        assert NSA_WINDOW % tk == 0 and tq == tk
        nkb = min(NSA_WINDOW // tk + 1, nq)
        kidx = lambda i, kk: jnp.maximum(i - (nkb - 1) + kk, 0)
        kcol, vcol = ROPE_KW // HEAD_DIM, PL_VW // HEAD_DIM
    in_specs = [
        pl.BlockSpec((1, tq, qw), lambda bi, gi, i, kk: (bi, i, ROPE_Q // qw + gi)),
        pl.BlockSpec((1, tk, HEAD_DIM), lambda bi, gi, i, kk: (bi, kidx(i, kk), kcol + gi)),
        pl.BlockSpec((1, tk, HEAD_DIM), lambda bi, gi, i, kk: (bi, kidx(i, kk), vcol + gi)),
    ]
    args = [proj_rope, proj_rope, proj_plain]
    if mode == "sel":
        n_sel = sel.shape[3]
        in_specs.append(pl.BlockSpec((1, 1, tq, n_sel), lambda bi, gi, i, kk: (bi, gi, i, 0)))
        args.append(sel)
    return pl.pallas_call(
        functools.partial(_nsa_flash_kernel, mode=mode, tq=tq, tk=tk, nkb=nkb),
        out_shape=jax.ShapeDtypeStruct((b, s, NSA_WIDTH), F32),
        grid=(b, g, nq, nkb),
        in_specs=in_specs,
        out_specs=pl.BlockSpec((1, tq, qw), lambda bi, gi, i, kk: (bi, i, gi)),
        scratch_shapes=[
            pltpu.VMEM((NSA_HPG, tq, 1), F32),
            pltpu.VMEM((NSA_HPG, tq, 1), F32),
            pltpu.VMEM((NSA_HPG, tq, HEAD_DIM), F32),
        ],
        compiler_params=_cparams(("parallel", "parallel", "parallel", "arbitrary")),
        name="nsa_flash_" + mode,
    )(*args)


def _diff_flash_kernel(q_ref, k_ref, v_ref, lam_ref, sub_ref, o_ref, m_ref, l_ref, acc_ref, *, tq, tk, lam_init):
    i = pl.program_id(2)
    kb = pl.program_id(3)

    @pl.when(kb == 0)
    def _():
        m_ref[...] = jnp.full_like(m_ref, NEG)
        l_ref[...] = jnp.zeros_like(l_ref)
        acc_ref[...] = jnp.zeros_like(acc_ref)

    @pl.when(kb <= i)
    def _():
        rows = i * tq + lax.broadcasted_iota(jnp.int32, (tq, tk), 0)
        cols = kb * tk + lax.broadcasted_iota(jnp.int32, (tq, tk), 1)
        valid = rows >= cols
        v = v_ref[0]
        for mp in range(2):
            q = q_ref[0, :, mp * HEAD_DIM:(mp + 1) * HEAD_DIM]
            k = k_ref[0, :, mp * HEAD_DIM:(mp + 1) * HEAD_DIM]
            s = jnp.where(valid, _dot_nt(q, k), NEG)
            m_old = m_ref[mp]
            m_new = jnp.maximum(m_old, jnp.max(s, axis=1, keepdims=True))
            alpha = jnp.exp(m_old - m_new)
            p = jnp.exp(s - m_new)
            l_ref[mp] = alpha * l_ref[mp] + jnp.sum(p, axis=1, keepdims=True)
            acc_ref[mp] = alpha * acc_ref[mp] + _bdot(p, v)
            m_ref[mp] = m_new

    @pl.when(kb == pl.num_programs(3) - 1)
    def _():
        lam = lam_ref[...]
        lam_full = (jnp.exp(jnp.sum(lam[0:1, :] * lam[1:2, :], axis=1, keepdims=True))
                    - jnp.exp(jnp.sum(lam[2:3, :] * lam[3:4, :], axis=1, keepdims=True)) + lam_init)
        o = acc_ref[0] / l_ref[0] - lam_full * (acc_ref[1] / l_ref[1])
        ms = jnp.mean(o * o, axis=-1, keepdims=True)
        o = o * lax.rsqrt(ms + 1e-5) * sub_ref[...]
        o_ref[0] = (o * (1.0 - lam_init)).astype(o_ref.dtype)


def _diff_flash(proj_rope, proj_plain, lam, subln, layer, tq, tk):
    b, s, _ = proj_rope.shape
    assert tq == tk
    nq = s // tq
    lam_init = 0.8 - 0.6 * math.exp(-0.3 * layer)
    w = DIFF_VDIM
    return pl.pallas_call(
        functools.partial(_diff_flash_kernel, tq=tq, tk=tk, lam_init=lam_init),
        out_shape=jax.ShapeDtypeStruct((b, s, DIFF_WIDTH), BF16),
        grid=(b, DIFF_HEADS, nq, nq),
        in_specs=[
            pl.BlockSpec((1, tq, w), lambda bi, h, i, kb: (bi, i, ROPE_DQ // w + h)),
            pl.BlockSpec((1, tk, w), lambda bi, h, i, kb: (bi, jnp.minimum(kb, i), ROPE_DK // w + h)),
            pl.BlockSpec((1, tk, w), lambda bi, h, i, kb: (bi, jnp.minimum(kb, i), PL_DV // w + h)),
            pl.BlockSpec(lam.shape, lambda bi, h, i, kb: (0, 0)),
            pl.BlockSpec((1, w), lambda bi, h, i, kb: (0, 0)),
        ],
        out_specs=pl.BlockSpec((1, tq, w), lambda bi, h, i, kb: (bi, i, h)),
        scratch_shapes=[
            pltpu.VMEM((2, tq, 1), F32),
            pltpu.VMEM((2, tq, 1), F32),
            pltpu.VMEM((2, tq, w), F32),
        ],
        compiler_params=_cparams(("parallel", "parallel", "parallel", "arbitrary")),
        name="diff_flash",
    )(proj_rope, proj_rope, proj_plain, lam, subln[None, :])


def _rwkv_prep_kernel(p_ref, prev_ref, mu_ref, wwa_ref, gup_ref, w0_ref, a0_ref,
                      r_ref, k_ref, v_ref, lw_ref, a_ref, g_ref, *, tm, seq):
    i = pl.program_id(0)
    p = p_ref[...]
    row = lax.broadcasted_iota(jnp.int32, p.shape, 0)
    carry = jnp.where((i * tm) % seq == 0, 0.0, 1.0) * prev_ref[SUBLANES - 1:SUBLANES, :]
    prev = jnp.where(row == 0, carry, pltpu.roll(p, 1, 0))
    xs = p + (prev - p) * mu_ref[...]
    W = RWKV_WIDTH
    r_ref[...] = xs[:, 0:W]
    k_ref[...] = xs[:, W:2 * W]
    v_ref[...] = xs[:, 2 * W:3 * W]
    wa = xs[:, 3 * W:3 * W + RW_LORA_PAD]
    lane = lax.broadcasted_iota(jnp.int32, wa.shape, 1)
    wa = jnp.where(lane < RWKV_W_LORA, jnp.tanh(wa), wa)
    lora = _bdot(wa, wwa_ref[...])
    lw_ref[...] = -math.exp(-0.5) * jax.nn.sigmoid(w0_ref[...] + lora[:, 0:W])
    a_ref[...] = jax.nn.sigmoid(a0_ref[...] + lora[:, W:2 * W])
    gd = xs[:, 3 * W + RW_LORA_PAD:3 * W + RW_LORA_PAD + RW_G_PAD]
    g_ref[...] = _bdot(jax.nn.sigmoid(gd), gup_ref[...])


def _rwkv_prep(proj_f32, mu_p, wwa, gup, w0, a0, seq, tm=256):
    m = proj_f32.shape[0]
    tm = min(tm, seq)
    assert seq % tm == 0 and m % tm == 0
    W = RWKV_WIDTH
    row = pl.BlockSpec((tm, W), lambda i: (i, 0))
    full = lambda a: pl.BlockSpec(a.shape, lambda i: (0,) * a.ndim)
    out = jax.ShapeDtypeStruct((m, W), F32)
    return pl.pallas_call(
        functools.partial(_rwkv_prep_kernel, tm=tm, seq=seq),
        out_shape=(out,) * 6,
        grid=(m // tm,),
        in_specs=[
            pl.BlockSpec((tm, RW_COLS), lambda i: (i, 0)),
            pl.BlockSpec((SUBLANES, RW_COLS), lambda i: (jnp.maximum(i * (tm // SUBLANES) - 1, 0), 0)),
            full(mu_p), full(wwa), full(gup), full(w0), full(a0),
        ],
        out_specs=(row,) * 6,
        compiler_params=_cparams(("parallel",)),
        name="rwkv_prep",
    )(proj_f32, proj_f32, mu_p, wwa, gup, w0, a0)


def _rwkv_chunk(r, k, v, lw, a, kk_w, ka_w, state, tri_incl, strict, incl, head0, blockdiag):
    c = r.shape[0]
    kk = k * kk_w
    ssq = kk * kk
    s0 = jnp.sum(jnp.where(head0, ssq, 0.0), axis=1, keepdims=True)
    s1 = jnp.sum(jnp.where(head0, 0.0, ssq), axis=1, keepdims=True)
    kap = kk / jnp.maximum(jnp.sqrt(jnp.where(head0, s0, s1)), 1e-12)
    kmod = k * (1.0 + (a - 1.0) * ka_w)
    bvec = kap * a

    cum = _dot_exact_lhs(tri_incl, lw)
    g_incl = jnp.exp(cum)
    g_inv = jnp.exp(-cum)
    g_last = g_incl[c - 1:c, :]
    kh = kmod * g_inv
    bh = bvec * g_inv
    kaph = kap * jnp.exp(cum - lw)
    rh = r * g_incl

    kt_parts, rt_parts, z0_parts, y0_parts = [], [], [], []
    for hd in range(2):
        msk = head0 if hd == 0 else jnp.logical_not(head0)
        kap_m = jnp.where(msk, kaph, 0.0)
        r_m = jnp.where(msk, rh, 0.0)
        lhs = jnp.concatenate([kap_m, r_m], axis=0)
        gram_b = _dot_nt(lhs, bh)
        gram_k = _dot_nt(lhs, kh)
        lb = jnp.where(strict, gram_b[0:c], 0.0)
        lk = jnp.where(strict, gram_k[0:c], 0.0)
        ab = jnp.where(incl, gram_b[c:2 * c], 0.0)
        ak = jnp.where(incl, gram_k[c:2 * c], 0.0)
        tm = -lb
        q = _bdot(lb, lb)
        n = 2
        while True:
            tm = tm + q + _bdot(tm, q)
            n *= 2
            if n >= c:
                break
            q = _bdot(q, q)
        kap_t = kap_m + _bdot(tm, kap_m)
        lkv = _bdot(lk, v)
        z0 = -(lkv + _bdot(tm, lkv))
        r_t = r_m - _bdot(ab, kap_t)
        y0 = _bdot(ab, z0) + _bdot(ak, v)
        kt_parts.append(kap_t)
        rt_parts.append(r_t)
        z0_parts.append(jnp.where(msk, z0, 0.0))
        y0_parts.append(jnp.where(msk, y0, 0.0))
    kap_t = kt_parts[0] + kt_parts[1]
    r_t = rt_parts[0] + rt_parts[1]
    z0 = z0_parts[0] + z0_parts[1]
    y0 = y0_parts[0] + y0_parts[1]

    y = _dot3(r_t, state) + y0
    bg = bh * g_last
    kg = kh * g_last
    trans = jnp.where(blockdiag, -_dot_tn(bg, kap_t), 0.0)
    eye = lax.broadcasted_iota(jnp.int32, (LANES, LANES), 0) == lax.broadcasted_iota(jnp.int32, (LANES, LANES), 1)
    trans = trans + jnp.where(eye, jnp.broadcast_to(g_last, (LANES, LANES)), 0.0)
    inject = jnp.where(blockdiag, _dot_tn(jnp.concatenate([bg, kg], axis=0),
                                          jnp.concatenate([z0, v], axis=0)), 0.0)
    new_state = _dot3(trans, state) + inject
    return y, new_state, kmod


def _rwkv_scan_kernel(r_ref, k_ref, v_ref, lw_ref, a_ref, g_ref, kk_ref, ka_ref, rk_ref, lnw_ref, lnb_ref,
                      o_ref, st_ref, *, ts, chunk):
    @pl.when(pl.program_id(2) == 0)
    def _():
        st_ref[...] = jnp.zeros_like(st_ref)

    c = chunk
    ri = lax.broadcasted_iota(jnp.int32, (c, c), 0)
    ci = lax.broadcasted_iota(jnp.int32, (c, c), 1)
    strict = ri > ci
    incl = ri >= ci
    tri_incl = jnp.where(incl, 1.0, 0.0).astype(BF16)
    head0 = lax.broadcasted_iota(jnp.int32, (c, LANES), 1) < RWKV_HEAD
    br = lax.broadcasted_iota(jnp.int32, (LANES, LANES), 0) < RWKV_HEAD
    bc = lax.broadcasted_iota(jnp.int32, (LANES, LANES), 1) < RWKV_HEAD
    blockdiag = br == bc
    state = st_ref[...]
    for ch in range(ts // c):
        sl = slice(ch * c, (ch + 1) * c)
        r = r_ref[0, sl, :]
        v = v_ref[0, sl, :]
        y, state, kmod = _rwkv_chunk(r, k_ref[0, sl, :], v, lw_ref[0, sl, :], a_ref[0, sl, :],
                                     kk_ref[...], ka_ref[...], state, tri_incl, strict, incl, head0, blockdiag)

        def seg_mean(x):
            s0 = jnp.sum(jnp.where(head0, x, 0.0), axis=1, keepdims=True)
            s1 = jnp.sum(jnp.where(head0, 0.0, x), axis=1, keepdims=True)
            return jnp.where(head0, s0, s1) * (1.0 / RWKV_HEAD)

        mean = seg_mean(y)
        yc = y - mean
        var = seg_mean(yc * yc)
        yn = yc * lax.rsqrt(var + RWKV_LNX_EPS) * lnw_ref[...] + lnb_ref[...]
        bonus = seg_mean(r * kmod * rk_ref[...]) * float(RWKV_HEAD) * v
        o_ref[0, sl, :] = ((yn + bonus) * g_ref[0, sl, :]).astype(o_ref.dtype)
    st_ref[...] = state


def _rwkv_scan(r, k, v, lw, a, g, k_k, k_a, r_k, lnx_w, lnx_b, ts=256, chunk=64):
    b, s, w = r.shape
    ts = min(ts, s)
    assert s % ts == 0 and ts % chunk == 0
    seq = pl.BlockSpec((1, ts, LANES), lambda bi, j, t: (bi, t, j))
    vec = pl.BlockSpec((1, LANES), lambda bi, j, t: (0, j))
    return pl.pallas_call(
        functools.partial(_rwkv_scan_kernel, ts=ts, chunk=chunk),
        out_shape=jax.ShapeDtypeStruct((b, s, w), BF16),
        grid=(b, w // LANES, s // ts),
        in_specs=[seq] * 6 + [vec] * 5,
        out_specs=seq,
        scratch_shapes=[pltpu.VMEM((LANES, LANES), F32)],
        compiler_params=_cparams(("parallel", "parallel", "arbitrary")),
        name="rwkv_scan",
    )(r, k, v, lw, a, g, k_k[None, :], k_a[None, :], r_k[None, :], lnx_w[None, :], lnx_b[None, :])


def _merge_kernel(oc_ref, os_ref, ow_ref, ng_ref, yd_ref, yr_ref, wb_ref, g0_ref, g1_ref, g2_ref, bias_ref,
                  o_ref, yn_ref):
    @pl.when(pl.program_id(1) == 0)
    def _():
        gates = jax.nn.sigmoid(ng_ref[...])
        for hd in range(NSA_HEADS):
            sl = slice(hd * HEAD_DIM, (hd + 1) * HEAD_DIM)
            o = (gates[:, 3 * hd:3 * hd + 1] * oc_ref[:, sl]
                 + gates[:, 3 * hd + 1:3 * hd + 2] * os_ref[:, sl]
                 + gates[:, 3 * hd + 2:3 * hd + 3] * ow_ref[:, sl])
            yn_ref[:, sl] = o.astype(BF16)

    branches = (yn_ref, yd_ref, yr_ref)
    graw = (g0_ref, g1_ref, g2_ref)
    acc = None
    for bi in range(N_BRANCH):
        gate = jax.nn.sigmoid(graw[bi][...] + bias_ref[bi])
        term = gate * jnp.dot(branches[bi][...], wb_ref[bi], preferred_element_type=F32)
        acc = term if acc is None else acc + term
    o_ref[...] = acc.astype(o_ref.dtype)


def _merge(o_cmp, o_slc, o_win, proj_f32, y_diff, y_rwkv, w_branch, b_gate, tm=512, tn=512):
    m = o_cmp.shape[0]
    n = w_branch.shape[2]
    tm = min(tm, m)
    assert m % tm == 0 and n % tn == 0 and F_BG % tn == 0 and n % tn == 0
    bw = BRANCH_WIDTH
    rowblk = lambda: pl.BlockSpec((tm, bw), lambda i, j: (i, 0))
    gate_spec = lambda bi: pl.BlockSpec((tm, tn), lambda i, j: (i, F_BG // tn + bi * (n // tn) + j))
    return pl.pallas_call(
        _merge_kernel,
        out_shape=jax.ShapeDtypeStruct((m, n), BF16),
        grid=(m // tm, n // tn),
        in_specs=[
            rowblk(), rowblk(), rowblk(),
            pl.BlockSpec((tm, LANES), lambda i, j: (i, F_NG // LANES)),
            rowblk(), rowblk(),
            pl.BlockSpec((N_BRANCH, bw, tn), lambda i, j: (0, 0, j)),
            gate_spec(0), gate_spec(1), gate_spec(2),
            pl.BlockSpec((N_BRANCH, 1, tn), lambda i, j: (0, 0, j)),
        ],
        out_specs=pl.BlockSpec((tm, tn), lambda i, j: (i, j)),
        scratch_shapes=[pltpu.VMEM((tm, bw), BF16)],
        compiler_params=_cparams(("parallel", "arbitrary")),
        name="gated_merge",
    )(o_cmp, o_slc, o_win, proj_f32, y_diff, y_rwkv, w_branch, proj_f32, proj_f32, proj_f32, b_gate[:, None, :])


def _pack_layer_weights(w_in, w_gate, rwkv_mu, w_up_lora, a_up_lora, g_up_lora):
    d = w_in.shape[0]
    nsa_sizes = (NSA_WIDTH,) + (NSA_KV,) * 6 + (3 * NSA_HEADS,)
    diff_sizes = (DIFF_QK, DIFF_QK, DIFF_WIDTH)
    rw_sizes = (RWKV_WIDTH,) * 3 + (RWKV_W_LORA, RWKV_A_LORA, RWKV_G_LORA)
    offs = np.cumsum((0,) + nsa_sizes + diff_sizes + rw_sizes)
    seg = [w_in[:, offs[i]:offs[i + 1]] for i in range(len(offs) - 1)]
    q, kc, vc, ks, vs, kw, vw, ng, dq, dk, dv, rr, rk, rv, wd, ad, gd = seg
    w_rope = jnp.concatenate([q, dq, ks, kw, dk], axis=1).astype(BF16)
    w_plain = jnp.concatenate([kc, vc, vs, vw, dv], axis=1).astype(BF16)
    zpad = lambda n: jnp.zeros((d, n), w_in.dtype)
    w_f32 = jnp.concatenate(
        [rr, rk, rv, wd, ad, gd, zpad(RW_G_PAD - RWKV_G_LORA), ng, zpad(LANES - 3 * NSA_HEADS)]
        + [w_gate[bi] for bi in range(N_BRANCH)], axis=1).astype(BF16)
    mu_p = jnp.concatenate([rwkv_mu, jnp.zeros((RW_G_PAD - RWKV_G_LORA,), rwkv_mu.dtype)])[None, :]
    zz = jnp.zeros((RWKV_W_LORA, RWKV_WIDTH), w_up_lora.dtype)
    wwa = jnp.concatenate([jnp.concatenate([w_up_lora, zz], axis=1),
                           jnp.concatenate([zz, a_up_lora], axis=1)], axis=0).astype(BF16)
    gup = jnp.concatenate([g_up_lora, jnp.zeros((RW_G_PAD - RWKV_G_LORA, RWKV_WIDTH), g_up_lora.dtype)],
                          axis=0).astype(BF16)
    return w_rope, w_plain, w_f32, mu_p, wwa, gup


def _overlap_matrix(n_cp, n_sel):
    c_start = np.arange(n_cp) * NSA_CMP_STRIDE
    c_end = c_start + NSA_CMP_LEN - 1
    j_start = np.arange(n_sel) * NSA_SEL_BLOCK
    ov = (c_start[:, None] <= j_start[None, :] + NSA_SEL_BLOCK - 1) & (c_end[:, None] >= j_start[None, :])
    return jnp.asarray(ov.astype(np.float32)).astype(BF16)


def _chunk_rows(z, groups):
    b, s, _ = z.shape
    z = z.reshape(b, s // NSA_CMP_STRIDE, NSA_CMP_STRIDE, groups, HEAD_DIM)
    return z.transpose(0, 3, 1, 2, 4).reshape(b, groups, s // NSA_CMP_STRIDE, NSA_CMP_STRIDE * HEAD_DIM)


def _nsa_branches(proj_rope, proj_plain, positions, cmp_pos, cmp_w1, cmp_w2, tq):
    b, s, _ = proj_rope.shape
    n_chunk = s // NSA_CMP_STRIDE
    n_sel = s // NSA_SEL_BLOCK
    kc2 = _chunk_rows(proj_plain[:, :, PL_KC:PL_KC + NSA_KV], NSA_GROUPS)
    vc2 = _chunk_rows(proj_plain[:, :, PL_VC:PL_VC + NSA_KV], NSA_GROUPS)
    cmp_end = np.minimum(np.arange(n_chunk) * NSA_CMP_STRIDE + NSA_CMP_LEN - 1, s - 1)
    pos_c = jnp.take(positions, jnp.asarray(cmp_end), axis=1)
    cosc, sinc = _rope_tables(pos_c.reshape(-1))
    cosc = cosc.reshape(b, n_chunk, LANES)
    sinc = sinc.reshape(b, n_chunk, LANES)
    w1 = cmp_w1.astype(BF16)
    w2 = cmp_w2.astype(BF16)
    pe = cmp_pos.reshape(2, 1, NSA_CMP_LEN * HEAD_DIM)
    kcmp = _nsa_compress(kc2, w1[0], w2[0], pe[0], cosc, sinc, True)
    vcmp = _nsa_compress(vc2, w1[1], w2[1], pe[1], cosc, sinc, False)
    o_cmp, sel = _nsa_cmp(proj_rope, kcmp, vcmp, _overlap_matrix(n_chunk, n_sel), tq)
    o_slc = _nsa_flash(proj_rope, proj_plain, sel, "sel", tq, tq)
    o_win = _nsa_flash(proj_rope, proj_plain, None, "win", tq, tq)
    return o_cmp, o_slc, o_win


def _layer(x2, b, s, layer, cosf, sinf, positions, p):
    m = b * s
    w_rope, w_plain, w_f32, mu_p, wwa, gup = _pack_layer_weights(
        p["w_in"], p["w_gate"], p["rwkv_mu"], p["rwkv_w_up"], p["rwkv_a_up"], p["rwkv_g_up"])
    scale = HEAD_DIM ** -0.5
    colscale = jnp.concatenate([jnp.full((ROPE_KS,), scale, F32), jnp.ones((ROPE_COLS - ROPE_KS,), F32)])[None, :]
    g_pre = p["norm_pre_mix"]
    proj_rope = _norm_matmul(x2, g_pre, w_rope, "rope", BF16, rope=(cosf, sinf, colscale))
    proj_plain = _norm_matmul(x2, g_pre, w_plain, "plain", BF16)
    proj_f32 = _norm_matmul(x2, g_pre, w_f32, "plain", F32)
    pr3 = proj_rope.reshape(b, s, ROPE_COLS)
    pp3 = proj_plain.reshape(b, s, PLAIN_COLS)

    tq = min(512, s)
    o_cmp, o_slc, o_win = _nsa_branches(pr3, pp3, positions, p["nsa_cmp_pos"], p["nsa_cmp_w1"], p["nsa_cmp_w2"], tq)
    y_diff = _diff_flash(pr3, pp3, p["diff_lambda"], p["diff_subln"], layer, tq, tq)

    r, k, v, lw, a, g = _rwkv_prep(proj_f32, mu_p, wwa, gup, p["rwkv_w0"][None, :], p["rwkv_a0"][None, :], s)
    sh = lambda z: z.reshape(b, s, RWKV_WIDTH)
    y_rwkv = _rwkv_scan(sh(r), sh(k), sh(v), sh(lw), sh(a), sh(g), p["rwkv_k_k"], p["rwkv_k_a"],
                        p["rwkv_r_k"].reshape(-1), p["rwkv_lnx_w"], p["rwkv_lnx_b"])

    merged = _merge(o_cmp.reshape(m, NSA_WIDTH), o_slc.reshape(m, NSA_WIDTH), o_win.reshape(m, NSA_WIDTH),
                    proj_f32, y_diff.reshape(m, DIFF_WIDTH), y_rwkv.reshape(m, RWKV_WIDTH),
                    p["w_branch"].astype(BF16), p["b_gate"])
    x2 = _matmul_norm_res(merged, p["w_out"].astype(BF16), p["norm_post_mix"], x2)
    hidden = _norm_matmul(x2, p["norm_pre_mlp"], p["w_up"].astype(BF16), "relu2", BF16)
    x2 = _matmul_norm_res(hidden, p["w_down"].astype(BF16), p["norm_post_mlp"], x2)
    return x2


def kernel(x, positions, norm_pre_mix, norm_post_mix, norm_pre_mlp, norm_post_mlp, w_in, nsa_cmp_pos, nsa_cmp_w1, nsa_cmp_w2, diff_lambda, diff_subln, rwkv_mu, rwkv_w0, rwkv_w_up, rwkv_a0, rwkv_a_up, rwkv_g_up, rwkv_k_k, rwkv_k_a, rwkv_r_k, rwkv_lnx_w, rwkv_lnx_b, w_gate, b_gate, w_branch, w_out, w_up, w_down):
    b, s, d = x.shape
    depth = w_in.shape[0]
    stacked = dict(
        norm_pre_mix=norm_pre_mix, norm_post_mix=norm_post_mix, norm_pre_mlp=norm_pre_mlp,
        norm_post_mlp=norm_post_mlp, w_in=w_in, nsa_cmp_pos=nsa_cmp_pos, nsa_cmp_w1=nsa_cmp_w1,
        nsa_cmp_w2=nsa_cmp_w2, diff_lambda=diff_lambda, diff_subln=diff_subln, rwkv_mu=rwkv_mu,
        rwkv_w0=rwkv_w0, rwkv_w_up=rwkv_w_up, rwkv_a0=rwkv_a0, rwkv_a_up=rwkv_a_up, rwkv_g_up=rwkv_g_up,
        rwkv_k_k=rwkv_k_k, rwkv_k_a=rwkv_k_a, rwkv_r_k=rwkv_r_k, rwkv_lnx_w=rwkv_lnx_w,
        rwkv_lnx_b=rwkv_lnx_b, w_gate=w_gate, b_gate=b_gate, w_branch=w_branch, w_out=w_out,
        w_up=w_up, w_down=w_down)
    cosf, sinf = _rope_tables(positions.reshape(-1))
    x2 = x.reshape(b * s, d)
    for layer in range(depth):
        x2 = _layer(x2, b, s, layer, cosf, sinf, positions, {n: a[layer] for n, a in stacked.items()})
    return x2.reshape(b, s, d)
```

```python
import functools
import math

import jax
import jax.numpy as jnp
import numpy as np
from jax import lax
from jax.experimental import pallas as pl
from jax.experimental.pallas import tpu as pltpu

F32 = jnp.float32
BF16 = jnp.bfloat16

D_MODEL = 2048
RMS_EPS = 1e-6
ROPE_THETA = 500000.0
HEAD_DIM = 128
ROT_HALF = HEAD_DIM // 8
NSA_HEADS = 8
NSA_GROUPS = 2
NSA_HPG = NSA_HEADS // NSA_GROUPS
NSA_CMP_LEN = 32
NSA_CMP_STRIDE = 16
NSA_CMP_HIDDEN = 256
NSA_SEL_BLOCK = 64
SEL_SHIFT = NSA_SEL_BLOCK.bit_length() - 1
NSA_SEL_TOPK = 16
NSA_WINDOW = 512
NSA_WIDTH = NSA_HEADS * HEAD_DIM
NSA_KV = NSA_GROUPS * HEAD_DIM
DIFF_HEADS = 4
DIFF_VDIM = 2 * HEAD_DIM
DIFF_WIDTH = DIFF_HEADS * DIFF_VDIM
DIFF_QK = 2 * DIFF_HEADS * HEAD_DIM
RWKV_HEAD = 64
RWKV_WIDTH = 1024
RWKV_W_LORA = 64
RWKV_A_LORA = 64
RWKV_G_LORA = 160
RWKV_LNX_EPS = 64e-5
N_BRANCH = 3
BRANCH_WIDTH = 1024

LANES = 128
SUBLANES = 8
VMEM_LIMIT_BYTES = 56 * 1024 * 1024

NEG = -1e30

ROPE_Q, ROPE_DQ, ROPE_KS, ROPE_KW, ROPE_DK = 0, 1024, 2048, 2304, 2560
ROPE_COLS = 3584
PL_KC, PL_VC, PL_VS, PL_VW, PL_DV = 0, 256, 512, 768, 1024
PLAIN_COLS = 2048
RW_LORA_PAD = 128
RW_G_PAD = 256
RW_COLS = 3 * RWKV_WIDTH + RW_LORA_PAD + RW_G_PAD
F_RW, F_NG, F_BG = 0, RW_COLS, RW_COLS + 128
F32_COLS = RW_COLS + 128 + N_BRANCH * D_MODEL


def _cparams(sem):
    return pltpu.CompilerParams(dimension_semantics=sem, vmem_limit_bytes=VMEM_LIMIT_BYTES)


def _bdot(a, b):
    return jnp.dot(a.astype(BF16), b.astype(BF16), preferred_element_type=F32)


def _dot_nt(a, b):
    return lax.dot_general(a.astype(BF16), b.astype(BF16), (((1,), (1,)), ((), ())),
                           preferred_element_type=F32)


def _dot_tn(a, b):
    return lax.dot_general(a.astype(BF16), b.astype(BF16), (((0,), (0,)), ((), ())),
                           preferred_element_type=F32)


def _split2(x):
    hi = x.astype(BF16)
    lo = (x - hi.astype(F32)).astype(BF16)
    return hi, lo


def _split3(x):
    hi = x.astype(BF16)
    r1 = x - hi.astype(F32)
    mid = r1.astype(BF16)
    lo = (r1 - mid.astype(F32)).astype(BF16)
    return hi, mid, lo


def _dot_exact_lhs(a_bf, x):
    hi, mid, lo = _split3(x)
    d = lambda p: jnp.dot(a_bf, p, preferred_element_type=F32)
    return d(hi) + (d(mid) + d(lo))


def _dot_exact_rhs(x, b_bf):
    hi, mid, lo = _split3(x)
    d = lambda p: jnp.dot(p, b_bf, preferred_element_type=F32)
    return d(hi) + (d(mid) + d(lo))


def _dot3(a, b):
    ah, al = _split2(a)
    bh, bl = _split2(b)
    d = lambda p, q: jnp.dot(p, q, preferred_element_type=F32)
    return d(ah, bh) + (d(ah, bl) + d(al, bh))


def _rope_partner(z, lane):
    return jnp.where(lane < ROT_HALF, pltpu.roll(z, LANES - ROT_HALF, 1), pltpu.roll(z, ROT_HALF, 1))


def _rope_table_kernel(pos_ref, invf_ref, sign_ref, cos_ref, sin_ref):
    ang = pos_ref[...].astype(F32) * invf_ref[...]
    cos_ref[...] = jnp.cos(ang)
    sin_ref[...] = jnp.sin(ang) * sign_ref[...]


def _rope_tables(pos_flat):
    n = pos_flat.shape[0]
    half = ROT_HALF
    inv_freq = ROPE_THETA ** (-jnp.arange(half, dtype=F32) / half)
    zeros = jnp.zeros((LANES - 2 * half,), F32)
    invf = jnp.concatenate([inv_freq, inv_freq, zeros])[None, :]
    sign = jnp.concatenate([-jnp.ones((half,), F32), jnp.ones((half,), F32), zeros])[None, :]
    tm = min(n, 2048)
    assert n % tm == 0
    vec = pl.BlockSpec((1, LANES), lambda i: (0, 0))
    out = pl.BlockSpec((tm, LANES), lambda i: (i, 0))
    return pl.pallas_call(
        _rope_table_kernel,
        out_shape=(jax.ShapeDtypeStruct((n, LANES), F32),) * 2,
        grid=(n // tm,),
        in_specs=[pl.BlockSpec((tm, 1), lambda i: (i, 0)), vec, vec],
        out_specs=(out, out),
        compiler_params=_cparams(("parallel",)),
        name="rope_tables",
    )(pos_flat[:, None], invf, sign)


def _norm_matmul_kernel(x_ref, g_ref, w_ref, *rest, mode, tn):
    if mode == "rope":
        cos_ref, sin_ref, cs_ref, o_ref, u_ref = rest
    else:
        o_ref, u_ref = rest

    @pl.when(pl.program_id(1) == 0)
    def _():
        x = x_ref[...]
        ms = jnp.mean(x * x, axis=-1, keepdims=True)
        u_ref[...] = (x * lax.rsqrt(ms + RMS_EPS) * g_ref[...]).astype(BF16)

    acc = jnp.dot(u_ref[...], w_ref[...], preferred_element_type=F32)
    if mode == "rope":
        acc = acc * cs_ref[...]
        cosv = cos_ref[...]
        sinv = sin_ref[...]
        lane = lax.broadcasted_iota(jnp.int32, cosv.shape, 1)
        for h in range(tn // LANES):
            z = acc[:, h * LANES:(h + 1) * LANES]
            o_ref[:, h * LANES:(h + 1) * LANES] = (z * cosv + _rope_partner(z, lane) * sinv).astype(o_ref.dtype)
    elif mode == "relu2":
        o_ref[...] = jnp.square(jnp.maximum(acc, 0.0)).astype(o_ref.dtype)
    else:
        o_ref[...] = acc.astype(o_ref.dtype)


def _norm_matmul(x, g, w, mode, out_dtype, rope=None, tm=1024, tn=512):
    m, d = x.shape
    n = w.shape[1]
    tm = min(tm, m)
    assert m % tm == 0 and n % tn == 0
    in_specs = [
        pl.BlockSpec((tm, d), lambda i, j: (i, 0)),
        pl.BlockSpec((1, d), lambda i, j: (0, 0)),
        pl.BlockSpec((d, tn), lambda i, j: (0, j)),
    ]
    args = [x, g[None, :], w]
    if mode == "rope":
        cosf, sinf, colscale = rope
        in_specs += [
            pl.BlockSpec((tm, LANES), lambda i, j: (i, 0)),
            pl.BlockSpec((tm, LANES), lambda i, j: (i, 0)),
            pl.BlockSpec((1, tn), lambda i, j: (0, j)),
        ]
        args += [cosf, sinf, colscale]
    return pl.pallas_call(
        functools.partial(_norm_matmul_kernel, mode=mode, tn=tn),
        out_shape=jax.ShapeDtypeStruct((m, n), out_dtype),
        grid=(m // tm, n // tn),
        in_specs=in_specs,
        out_specs=pl.BlockSpec((tm, tn), lambda i, j: (i, j)),
        scratch_shapes=[pltpu.VMEM((tm, d), BF16)],
        compiler_params=_cparams(("parallel", "arbitrary")),
        name="norm_matmul_" + mode,
    )(*args)


def _matmul_norm_res_kernel(a_ref, w_ref, g_ref, res_ref, o_ref, acc_ref):
    k = pl.program_id(1)

    @pl.when(k == 0)
    def _():
        acc_ref[...] = jnp.zeros_like(acc_ref)

    acc_ref[...] += jnp.dot(a_ref[...], w_ref[...], preferred_element_type=F32)

    @pl.when(k == pl.num_programs(1) - 1)
    def _():
        y = acc_ref[...]
        ms = jnp.mean(y * y, axis=-1, keepdims=True)
        o_ref[...] = res_ref[...] + y * lax.rsqrt(ms + RMS_EPS) * g_ref[...]


def _matmul_norm_res(a, w, g, res, tm=512, tk=1024):
    m, kdim = a.shape
    n = w.shape[1]
    tm = min(tm, m)
    assert m % tm == 0 and kdim % tk == 0
    return pl.pallas_call(
        _matmul_norm_res_kernel,
        out_shape=jax.ShapeDtypeStruct((m, n), F32),
        grid=(m // tm, kdim // tk),
        in_specs=[
            pl.BlockSpec((tm, tk), lambda i, k: (i, k)),
            pl.BlockSpec((tk, n), lambda i, k: (k, 0)),
            pl.BlockSpec((1, n), lambda i, k: (0, 0)),
            pl.BlockSpec((tm, n), lambda i, k: (i, 0)),
        ],
        out_specs=pl.BlockSpec((tm, n), lambda i, k: (i, 0)),
        scratch_shapes=[pltpu.VMEM((tm, n), F32)],
        compiler_params=_cparams(("parallel", "arbitrary")),
        name="matmul_norm_res",
    )(a, w, g[None, :], res)


def _gelu_tanh(x):
    return 0.5 * x * (1.0 + jnp.tanh(math.sqrt(2.0 / math.pi) * (x + 0.044715 * (x * x * x))))


def _nsa_compress_kernel(x_ref, w1_ref, w2_ref, pe_ref, cos_ref, sin_ref, o_ref, *, use_rope):
    x = x_ref[0, 0]
    half = x.shape[1]
    n_chunk = x.shape[0]
    a = jnp.dot(x, w1_ref[0:half, :], preferred_element_type=F32)
    b = jnp.dot(x, w1_ref[half:2 * half, :], preferred_element_type=F32)
    pe = jnp.broadcast_to(pe_ref[...], (SUBLANES, pe_ref.shape[1]))
    peb = _dot_exact_rhs_general(pe, w1_ref[...])[0:1, :]
    h = a + pltpu.roll(b, n_chunk - 1, 0) + peb
    y = _bdot(_gelu_tanh(h), w2_ref[...])
    if use_rope:
        lane = lax.broadcasted_iota(jnp.int32, y.shape, 1)
        y = y * cos_ref[0] + _rope_partner(y, lane) * sin_ref[0]
    o_ref[0, 0] = y.astype(o_ref.dtype)


def _dot_exact_rhs_general(x, w_bf):
    hi, mid, lo = _split3(x)
    d = lambda p: jnp.dot(p, w_bf, preferred_element_type=F32)
    return d(hi) + (d(mid) + d(lo))


def _nsa_compress(x2, w1, w2, pe_flat, cosc, sinc, use_rope):
    b, g, n_chunk, width = x2.shape
    dh = w2.shape[1]
    return pl.pallas_call(
        functools.partial(_nsa_compress_kernel, use_rope=use_rope),
        out_shape=jax.ShapeDtypeStruct((b, g, n_chunk, dh), BF16),
        grid=(b, g),
        in_specs=[
            pl.BlockSpec((1, 1, n_chunk, width), lambda i, j: (i, j, 0, 0)),
            pl.BlockSpec(w1.shape, lambda i, j: (0, 0)),
            pl.BlockSpec(w2.shape, lambda i, j: (0, 0)),
            pl.BlockSpec(pe_flat.shape, lambda i, j: (0, 0)),
            pl.BlockSpec((1, n_chunk, dh), lambda i, j: (i, 0, 0)),
            pl.BlockSpec((1, n_chunk, dh), lambda i, j: (i, 0, 0)),
        ],
        out_specs=pl.BlockSpec((1, 1, n_chunk, dh), lambda i, j: (i, j, 0, 0)),
        compiler_params=_cparams(("parallel", "parallel")),
        name="nsa_compress",
    )(x2, w1, w2, pe_flat, cosc, sinc)


def _nsa_cmp_kernel(q_ref, kc_ref, vc_ref, ov_ref, o_ref, sel_ref, *, tq, top):
    i = pl.program_id(2)
    kc = kc_ref[0, 0]
    vc = vc_ref[0, 0]
    ncp = kc.shape[0]
    n_sel = sel_ref.shape[3]
    t = i * tq + lax.broadcasted_iota(jnp.int32, (tq, ncp), 0)
    c = lax.broadcasted_iota(jnp.int32, (tq, ncp), 1)
    valid = (c * NSA_CMP_STRIDE + (NSA_CMP_LEN - 1)) <= t
    psum = jnp.zeros((tq, ncp), F32)
    for h in range(NSA_HPG):
        q = q_ref[0, :, h * HEAD_DIM:(h + 1) * HEAD_DIM]
        s = jnp.where(valid, _dot_nt(q, kc), NEG)
        m = jnp.max(s, axis=1, keepdims=True)
        e = jnp.where(valid, jnp.exp(s - m), 0.0)
        l = jnp.sum(e, axis=1, keepdims=True)
        p = e / jnp.where(l > 0.0, l, 1.0)
        o_ref[0, :, h * HEAD_DIM:(h + 1) * HEAD_DIM] = _bdot(p, vc).astype(o_ref.dtype)
        psum = psum + p
    imp = _dot_exact_rhs(psum, ov_ref[...])
    jj = lax.broadcasted_iota(jnp.int32, (tq, n_sel), 1)
    blk_t = jnp.right_shift(i * tq + lax.broadcasted_iota(jnp.int32, (tq, n_sel), 0), SEL_SHIFT)
    forced = (jj == 0) | (jj == blk_t) | (jj == blk_t - 1)
    imp = jnp.where(forced, 1e9, jnp.where(jj > blk_t, -1.0, imp))
    rank = jnp.zeros((tq, n_sel), F32)
    for ii in range(n_sel):
        col = imp[:, ii:ii + 1]
        beats = (col > imp) | ((col == imp) & (jj > ii))
        rank = rank + jnp.where(beats, 1.0, 0.0)
    sel_ref[0, 0] = jnp.where(rank < float(top), 1.0, 0.0).astype(sel_ref.dtype)


def _nsa_cmp(proj_rope, kcmp, vcmp, overlap, tq):
    b, s, _ = proj_rope.shape
    g = kcmp.shape[1]
    ncp = kcmp.shape[2]
    n_sel = overlap.shape[1]
    top = min(NSA_SEL_TOPK, n_sel)
    qw = NSA_HPG * HEAD_DIM
    return pl.pallas_call(
        functools.partial(_nsa_cmp_kernel, tq=tq, top=top),
        out_shape=(jax.ShapeDtypeStruct((b, s, NSA_WIDTH), F32),
                   jax.ShapeDtypeStruct((b, g, s, n_sel), BF16)),
        grid=(b, g, s // tq),
        in_specs=[
            pl.BlockSpec((1, tq, qw), lambda bi, gi, i: (bi, i, ROPE_Q // qw + gi)),
            pl.BlockSpec((1, 1, ncp, HEAD_DIM), lambda bi, gi, i: (bi, gi, 0, 0)),
            pl.BlockSpec((1, 1, ncp, HEAD_DIM), lambda bi, gi, i: (bi, gi, 0, 0)),
            pl.BlockSpec(overlap.shape, lambda bi, gi, i: (0, 0)),
        ],
        out_specs=(pl.BlockSpec((1, tq, qw), lambda bi, gi, i: (bi, i, gi)),
                   pl.BlockSpec((1, 1, tq, n_sel), lambda bi, gi, i: (bi, gi, i, 0))),
        compiler_params=_cparams(("parallel", "parallel", "parallel")),
        name="nsa_cmp_select",
    )(proj_rope, kcmp, vcmp, overlap)


def _nsa_flash_kernel(*refs, mode, tq, tk, nkb):
    if mode == "sel":
        q_ref, k_ref, v_ref, sel_ref, o_ref, m_ref, l_ref, acc_ref = refs
    else:
        q_ref, k_ref, v_ref, o_ref, m_ref, l_ref, acc_ref = refs
    i = pl.program_id(2)
    kk = pl.program_id(3)
    if mode == "sel":
        kb = kk
        live = kb <= i
    else:
        kb = i - (nkb - 1) + kk
        live = kb >= 0

    @pl.when(kk == 0)
    def _():
        m_ref[...] = jnp.full_like(m_ref, NEG)
        l_ref[...] = jnp.zeros_like(l_ref)
        acc_ref[...] = jnp.zeros_like(acc_ref)

    @pl.when(live)
    def _():
        rows = i * tq + lax.broadcasted_iota(jnp.int32, (tq, tk), 0)
        cols = kb * tk + lax.broadcasted_iota(jnp.int32, (tq, tk), 1)
        d = rows - cols
        valid = d >= 0
        if mode == "win":
            valid = valid & (d < NSA_WINDOW)
        else:
            n_sel = sel_ref.shape[3]
            jrow = lax.broadcasted_iota(jnp.int32, (n_sel, tk), 0)
            jcol = kb * (tk // NSA_SEL_BLOCK) + jnp.right_shift(
                lax.broadcasted_iota(jnp.int32, (n_sel, tk), 1), SEL_SHIFT)
            expand = jnp.where(jrow == jcol, 1.0, 0.0).astype(BF16)
            chosen = jnp.dot(sel_ref[0, 0], expand, preferred_element_type=F32)
            valid = valid & (chosen > 0.5)
        k = k_ref[0]
        v = v_ref[0]
        for h in range(NSA_HPG):
            q = q_ref[0, :, h * HEAD_DIM:(h + 1) * HEAD_DIM]
            s = jnp.where(valid, _dot_nt(q, k), NEG)
            m_old = m_ref[h]
            m_new = jnp.maximum(m_old, jnp.max(s, axis=1, keepdims=True))
            alpha = jnp.exp(m_old - m_new)
            p = jnp.exp(s - m_new)
            l_ref[h] = alpha * l_ref[h] + jnp.sum(p, axis=1, keepdims=True)
            acc_ref[h] = alpha * acc_ref[h] + _bdot(p, v)
            m_ref[h] = m_new

    @pl.when(kk == pl.num_programs(3) - 1)
    def _():
        for h in range(NSA_HPG):
            o_ref[0, :, h * HEAD_DIM:(h + 1) * HEAD_DIM] = (acc_ref[h] / l_ref[h]).astype(o_ref.dtype)


def _nsa_flash(proj_rope, proj_plain, sel, mode, tq, tk):
    b, s, _ = proj_rope.shape
    g = NSA_GROUPS
    qw = NSA_HPG * HEAD_DIM
    nq = s // tq
    if mode == "sel":
        assert tq == tk
        nkb = nq
        kidx = lambda i, kk: jnp.minimum(kk, i)
        kcol, vcol = ROPE_KS // HEAD_DIM, PL_VS // HEAD_DIM
    else:
        assert NSA_WINDOW % tk == 0 and tq == tk
        nkb = min(NSA_WINDOW // tk + 1, nq)
        kidx = lambda i, kk: jnp.maximum(i - (nkb - 1) + kk, 0)
        kcol, vcol = ROPE_KW // HEAD_DIM, PL_VW // HEAD_DIM
    in_specs = [
        pl.BlockSpec((1, tq, qw), lambda bi, gi, i, kk: (bi, i, ROPE_Q // qw + gi)),
        pl.BlockSpec((1, tk, HEAD_DIM), lambda bi, gi, i, kk: (bi, kidx(i, kk), kcol + gi)),
        pl.BlockSpec((1, tk, HEAD_DIM), lambda bi, gi, i, kk: (bi, kidx(i, kk), vcol + gi)),
    ]
    args = [proj_rope, proj_rope, proj_plain]
    if mode == "sel":
        n_sel = sel.shape[3]
        in_specs.append(pl.BlockSpec((1, 1, tq, n_sel), lambda bi, gi, i, kk: (bi, gi, i, 0)))
        args.append(sel)
    return pl.pallas_call(
        functools.partial(_nsa_flash_kernel, mode=mode, tq=tq, tk=tk, nkb=nkb),
        out_shape=jax.ShapeDtypeStruct((b, s, NSA_WIDTH), F32),
        grid=(b, g, nq, nkb),
        in_specs=in_specs,
        out_specs=pl.BlockSpec((1, tq, qw), lambda bi, gi, i, kk: (bi, i, gi)),
        scratch_shapes=[
            pltpu.VMEM((NSA_HPG, tq, 1), F32),
            pltpu.VMEM((NSA_HPG, tq, 1), F32),
            pltpu.VMEM((NSA_HPG, tq, HEAD_DIM), F32),
        ],
        compiler_params=_cparams(("parallel", "parallel", "parallel", "arbitrary")),
        name="nsa_flash_" + mode,
    )(*args)


def _diff_flash_kernel(q_ref, k_ref, v_ref, lam_ref, sub_ref, o_ref, m_ref, l_ref, acc_ref, *, tq, tk, lam_init):
    i = pl.program_id(2)
    kb = pl.program_id(3)

    @pl.when(kb == 0)
    def _():
        m_ref[...] = jnp.full_like(m_ref, NEG)
        l_ref[...] = jnp.zeros_like(l_ref)
        acc_ref[...] = jnp.zeros_like(acc_ref)

    @pl.when(kb <= i)
    def _():
        rows = i * tq + lax.broadcasted_iota(jnp.int32, (tq, tk), 0)
        cols = kb * tk + lax.broadcasted_iota(jnp.int32, (tq, tk), 1)
        valid = rows >= cols
        v = v_ref[0]
        for mp in range(2):
            q = q_ref[0, :, mp * HEAD_DIM:(mp + 1) * HEAD_DIM]
            k = k_ref[0, :, mp * HEAD_DIM:(mp + 1) * HEAD_DIM]
            s = jnp.where(valid, _dot_nt(q, k), NEG)
            m_old = m_ref[mp]
            m_new = jnp.maximum(m_old, jnp.max(s, axis=1, keepdims=True))
            alpha = jnp.exp(m_old - m_new)
            p = jnp.exp(s - m_new)
            l_ref[mp] = alpha * l_ref[mp] + jnp.sum(p, axis=1, keepdims=True)
            acc_ref[mp] = alpha * acc_ref[mp] + _bdot(p, v)
            m_ref[mp] = m_new

    @pl.when(kb == pl.num_programs(3) - 1)
    def _():
        lam = lam_ref[...]
        lam_full = (jnp.exp(jnp.sum(lam[0:1, :] * lam[1:2, :], axis=1, keepdims=True))
                    - jnp.exp(jnp.sum(lam[2:3, :] * lam[3:4, :], axis=1, keepdims=True)) + lam_init)
        o = acc_ref[0] / l_ref[0] - lam_full * (acc_ref[1] / l_ref[1])
        ms = jnp.mean(o * o, axis=-1, keepdims=True)
        o = o * lax.rsqrt(ms + 1e-5) * sub_ref[...]
        o_ref[0] = (o * (1.0 - lam_init)).astype(o_ref.dtype)


def _diff_flash(proj_rope, proj_plain, lam, subln, layer, tq, tk):
    b, s, _ = proj_rope.shape
    assert tq == tk
    nq = s // tq
    lam_init = 0.8 - 0.6 * math.exp(-0.3 * layer)
    w = DIFF_VDIM
    return pl.pallas_call(
        functools.partial(_diff_flash_kernel, tq=tq, tk=tk, lam_init=lam_init),
        out_shape=jax.ShapeDtypeStruct((b, s, DIFF_WIDTH), BF16),
        grid=(b, DIFF_HEADS, nq, nq),
        in_specs=[
            pl.BlockSpec((1, tq, w), lambda bi, h, i, kb: (bi, i, ROPE_DQ // w + h)),
            pl.BlockSpec((1, tk, w), lambda bi, h, i, kb: (bi, jnp.minimum(kb, i), ROPE_DK // w + h)),
            pl.BlockSpec((1, tk, w), lambda bi, h, i, kb: (bi, jnp.minimum(kb, i), PL_DV // w + h)),
            pl.BlockSpec(lam.shape, lambda bi, h, i, kb: (0, 0)),
            pl.BlockSpec((1, w), lambda bi, h, i, kb: (0, 0)),
        ],
        out_specs=pl.BlockSpec((1, tq, w), lambda bi, h, i, kb: (bi, i, h)),
        scratch_shapes=[
            pltpu.VMEM((2, tq, 1), F32),
            pltpu.VMEM((2, tq, 1), F32),
            pltpu.VMEM((2, tq, w), F32),
        ],
        compiler_params=_cparams(("parallel", "parallel", "parallel", "arbitrary")),
        name="diff_flash",
    )(proj_rope, proj_rope, proj_plain, lam, subln[None, :])


def _rwkv_prep_kernel(p_ref, prev_ref, mu_ref, wwa_ref, gup_ref, w0_ref, a0_ref,
                      r_ref, k_ref, v_ref, lw_ref, a_ref, g_ref, *, tm, seq):
    i = pl.program_id(0)
    p = p_ref[...]
    row = lax.broadcasted_iota(jnp.int32, p.shape, 0)
    carry = jnp.where((i * tm) % seq == 0, 0.0, 1.0) * prev_ref[SUBLANES - 1:SUBLANES, :]
    prev = jnp.where(row == 0, carry, pltpu.roll(p, 1, 0))
    xs = p + (prev - p) * mu_ref[...]
    W = RWKV_WIDTH
    r_ref[...] = xs[:, 0:W]
    k_ref[...] = xs[:, W:2 * W]
    v_ref[...] = xs[:, 2 * W:3 * W]
    wa = xs[:, 3 * W:3 * W + RW_LORA_PAD]
    lane = lax.broadcasted_iota(jnp.int32, wa.shape, 1)
    wa = jnp.where(lane < RWKV_W_LORA, jnp.tanh(wa), wa)
    lora = _bdot(wa, wwa_ref[...])
    lw_ref[...] = -math.exp(-0.5) * jax.nn.sigmoid(w0_ref[...] + lora[:, 0:W])
    a_ref[...] = jax.nn.sigmoid(a0_ref[...] + lora[:, W:2 * W])
    gd = xs[:, 3 * W + RW_LORA_PAD:3 * W + RW_LORA_PAD + RW_G_PAD]
    g_ref[...] = _bdot(jax.nn.sigmoid(gd), gup_ref[...])


def _rwkv_prep(proj_f32, mu_p, wwa, gup, w0, a0, seq, tm=256):
    m = proj_f32.shape[0]
    tm = min(tm, seq)
    assert seq % tm == 0 and m % tm == 0
    W = RWKV_WIDTH
    row = pl.BlockSpec((tm, W), lambda i: (i, 0))
    full = lambda a: pl.BlockSpec(a.shape, lambda i: (0,) * a.ndim)
    out = jax.ShapeDtypeStruct((m, W), F32)
    return pl.pallas_call(
        functools.partial(_rwkv_prep_kernel, tm=tm, seq=seq),
        out_shape=(out,) * 6,
        grid=(m // tm,),
        in_specs=[
            pl.BlockSpec((tm, RW_COLS), lambda i: (i, 0)),
            pl.BlockSpec((SUBLANES, RW_COLS), lambda i: (jnp.maximum(i * (tm // SUBLANES) - 1, 0), 0)),
            full(mu_p), full(wwa), full(gup), full(w0), full(a0),
        ],
        out_specs=(row,) * 6,
        compiler_params=_cparams(("parallel",)),
        name="rwkv_prep",
    )(proj_f32, proj_f32, mu_p, wwa, gup, w0, a0)


def _bmm(a, b):
    return jnp.einsum("umk,ukn->umn", a.astype(BF16), b.astype(BF16), preferred_element_type=F32)


def _bmm_nt(a, b):
    return jnp.einsum("umk,unk->umn", a.astype(BF16), b.astype(BF16), preferred_element_type=F32)


def _rwkv_scan_kernel(r_ref, k_ref, v_ref, lw_ref, a_ref, g_ref, kk_ref, ka_ref, rk_ref, lnw_ref, lnb_ref,
                      o_ref, st_ref, *, ts, chunk):
    @pl.when(pl.program_id(2) == 0)
    def _():
        st_ref[...] = jnp.zeros_like(st_ref)

    c = chunk
    nc = ts // c
    r = r_ref[0]
    k = k_ref[0]
    v = v_ref[0]
    lw = lw_ref[0]
    a = a_ref[0]
    head0 = lax.broadcasted_iota(jnp.int32, (ts, LANES), 1) < RWKV_HEAD

    def seg_sum(x):
        s0 = jnp.sum(jnp.where(head0, x, 0.0), axis=1, keepdims=True)
        s1 = jnp.sum(jnp.where(head0, 0.0, x), axis=1, keepdims=True)
        return jnp.where(head0, s0, s1)

    kk = k * kk_ref[...]
    kap = kk / jnp.maximum(jnp.sqrt(seg_sum(kk * kk)), 1e-12)
    kmod = k * (1.0 + (a - 1.0) * ka_ref[...])
    bvec = kap * a

    ri = lax.broadcasted_iota(jnp.int32, (ts, ts), 0)
    ci = lax.broadcasted_iota(jnp.int32, (ts, ts), 1)
    same_chunk = jnp.right_shift(ri, c.bit_length() - 1) == jnp.right_shift(ci, c.bit_length() - 1)
    tri = jnp.where(same_chunk & (ri >= ci), 1.0, 0.0).astype(BF16)
    cum = _dot_exact_lhs(tri, lw)
    g_incl = jnp.exp(cum)
    g_inv = jnp.exp(-cum)
    split = lambda x: x.reshape(nc, c, LANES)
    g_last = [g_incl[(ch + 1) * c - 1:(ch + 1) * c, :] for ch in range(nc)]
    g_last_rows = jnp.concatenate([jnp.broadcast_to(gl, (c, LANES)) for gl in g_last], axis=0)
    kh = split(kmod * g_inv)
    bh = split(bvec * g_inv)
    kg = split(kmod * g_inv * g_last_rows)
    bg = split(bvec * g_inv * g_last_rows)
    kaph = kap * jnp.exp(cum - lw)
    rh = r * g_incl
    v3 = split(v)

    both = lambda x: jnp.concatenate([split(jnp.where(head0, x, 0.0)), split(jnp.where(head0, 0.0, x))], axis=0)
    twice = lambda x: jnp.concatenate([x, x], axis=0)
    kap_m = both(kaph)
    r_m = both(rh)
    gram = _bmm_nt(jnp.concatenate([kap_m, r_m], axis=1), twice(jnp.concatenate([bh, kh], axis=1)))
    row = lax.broadcasted_iota(jnp.int32, (c, 2 * c), 0)
    col = lax.broadcasted_iota(jnp.int32, (c, 2 * c), 1) & (c - 1)
    lbk = jnp.where(row > col, gram[:, 0:c, :], 0.0)
    abk = jnp.where(row >= col, gram[:, c:2 * c, :], 0.0)
    lb = lbk[:, :, 0:c]
    tm = -lb
    q = _bmm(lb, lb)
    n = 2
    while True:
        tm = tm + q + _bmm(tm, q)
        n *= 2
        if n >= c:
            break
        q = _bmm(q, q)
    v2 = twice(v3)
    kap_t = kap_m + _bmm(tm, kap_m)
    lkv = _bmm(lbk, jnp.concatenate([jnp.zeros_like(v2), v2], axis=1))
    z0 = -(lkv + _bmm(tm, lkv))
    r_t = r_m - _bmm(abk, jnp.concatenate([kap_t, jnp.zeros_like(kap_t)], axis=1))
    y0 = _bmm(abk, jnp.concatenate([z0, v2], axis=1))
    h0 = lax.broadcasted_iota(jnp.int32, (nc, c, LANES), 2) < RWKV_HEAD
    merge = lambda x: jnp.where(h0, x[0:nc], x[nc:2 * nc])
    kap_t = kap_t[0:nc] + kap_t[nc:2 * nc]
    r_t = r_t[0:nc] + r_t[nc:2 * nc]
    z0 = merge(z0)
    y0 = merge(y0)

    br = lax.broadcasted_iota(jnp.int32, (LANES, LANES), 0)
    bc = lax.broadcasted_iota(jnp.int32, (LANES, LANES), 1)
    blockdiag = (br < RWKV_HEAD) == (bc < RWKV_HEAD)
    state = st_ref[...]
    ys = []
    for ch in range(nc):
        trans = jnp.where(blockdiag, -_dot_tn(bg[ch], kap_t[ch]), 0.0)
        trans = trans + jnp.where(br == bc, jnp.broadcast_to(g_last[ch], (LANES, LANES)), 0.0)
        inject = jnp.where(blockdiag, _dot_tn(jnp.concatenate([bg[ch], kg[ch]], axis=0),
                                              jnp.concatenate([z0[ch], v3[ch]], axis=0)), 0.0)
        ys.append(_dot3(r_t[ch], state) + y0[ch])
        state = _dot3(trans, state) + inject
    st_ref[...] = state
    y = jnp.concatenate(ys, axis=0)

    mean = seg_sum(y) * (1.0 / RWKV_HEAD)
    yc = y - mean
    var = seg_sum(yc * yc) * (1.0 / RWKV_HEAD)
    yn = yc * lax.rsqrt(var + RWKV_LNX_EPS) * lnw_ref[...] + lnb_ref[...]
    bonus = seg_sum(r * kmod * rk_ref[...]) * v
    o_ref[0] = ((yn + bonus) * g_ref[0]).astype(o_ref.dtype)


def _rwkv_scan(r, k, v, lw, a, g, k_k, k_a, r_k, lnx_w, lnx_b, ts=512, chunk=64):
    b, s, w = r.shape
    ts = min(ts, s)
    assert s % ts == 0 and ts % chunk == 0
    seq = pl.BlockSpec((1, ts, LANES), lambda bi, j, t: (bi, t, j))
    vec = pl.BlockSpec((1, LANES), lambda bi, j, t: (0, j))
    return pl.pallas_call(
        functools.partial(_rwkv_scan_kernel, ts=ts, chunk=chunk),
        out_shape=jax.ShapeDtypeStruct((b, s, w), BF16),
        grid=(b, w // LANES, s // ts),
        in_specs=[seq] * 6 + [vec] * 5,
        out_specs=seq,
        scratch_shapes=[pltpu.VMEM((LANES, LANES), F32)],
        compiler_params=_cparams(("parallel", "parallel", "arbitrary")),
        name="rwkv_scan",
    )(r, k, v, lw, a, g, k_k[None, :], k_a[None, :], r_k[None, :], lnx_w[None, :], lnx_b[None, :])


def _merge_kernel(oc_ref, os_ref, ow_ref, ng_ref, yd_ref, yr_ref, wb_ref, g0_ref, g1_ref, g2_ref, bias_ref,
                  o_ref, yn_ref):
    @pl.when(pl.program_id(1) == 0)
    def _():
        gates = jax.nn.sigmoid(ng_ref[...])
        for hd in range(NSA_HEADS):
            sl = slice(hd * HEAD_DIM, (hd + 1) * HEAD_DIM)
            o = (gates[:, 3 * hd:3 * hd + 1] * oc_ref[:, sl]
                 + gates[:, 3 * hd + 1:3 * hd + 2] * os_ref[:, sl]
                 + gates[:, 3 * hd + 2:3 * hd + 3] * ow_ref[:, sl])
            yn_ref[:, sl] = o.astype(BF16)

    branches = (yn_ref, yd_ref, yr_ref)
    graw = (g0_ref, g1_ref, g2_ref)
    acc = None
    for bi in range(N_BRANCH):
        gate = jax.nn.sigmoid(graw[bi][...] + bias_ref[bi])
        term = gate * jnp.dot(branches[bi][...], wb_ref[bi], preferred_element_type=F32)
        acc = term if acc is None else acc + term
    o_ref[...] = acc.astype(o_ref.dtype)


def _merge(o_cmp, o_slc, o_win, proj_f32, y_diff, y_rwkv, w_branch, b_gate, tm=512, tn=512):
    m = o_cmp.shape[0]
    n = w_branch.shape[2]
    tm = min(tm, m)
    assert m % tm == 0 and n % tn == 0 and F_BG % tn == 0 and n % tn == 0
    bw = BRANCH_WIDTH
    rowblk = lambda: pl.BlockSpec((tm, bw), lambda i, j: (i, 0))
    gate_spec = lambda bi: pl.BlockSpec((tm, tn), lambda i, j: (i, F_BG // tn + bi * (n // tn) + j))
    return pl.pallas_call(
        _merge_kernel,
        out_shape=jax.ShapeDtypeStruct((m, n), BF16),
        grid=(m // tm, n // tn),
        in_specs=[
            rowblk(), rowblk(), rowblk(),
            pl.BlockSpec((tm, LANES), lambda i, j: (i, F_NG // LANES)),
            rowblk(), rowblk(),
            pl.BlockSpec((N_BRANCH, bw, tn), lambda i, j: (0, 0, j)),
            gate_spec(0), gate_spec(1), gate_spec(2),
            pl.BlockSpec((N_BRANCH, 1, tn), lambda i, j: (0, 0, j)),
        ],
        out_specs=pl.BlockSpec((tm, tn), lambda i, j: (i, j)),
        scratch_shapes=[pltpu.VMEM((tm, bw), BF16)],
        compiler_params=_cparams(("parallel", "arbitrary")),
        name="gated_merge",
    )(o_cmp, o_slc, o_win, proj_f32, y_diff, y_rwkv, w_branch, proj_f32, proj_f32, proj_f32, b_gate[:, None, :])


def _pack_layer_weights(w_in, w_gate, rwkv_mu, w_up_lora, a_up_lora, g_up_lora):
    d = w_in.shape[0]
    nsa_sizes = (NSA_WIDTH,) + (NSA_KV,) * 6 + (3 * NSA_HEADS,)
    diff_sizes = (DIFF_QK, DIFF_QK, DIFF_WIDTH)
    rw_sizes = (RWKV_WIDTH,) * 3 + (RWKV_W_LORA, RWKV_A_LORA, RWKV_G_LORA)
    offs = np.cumsum((0,) + nsa_sizes + diff_sizes + rw_sizes)
    seg = [w_in[:, offs[i]:offs[i + 1]] for i in range(len(offs) - 1)]
    q, kc, vc, ks, vs, kw, vw, ng, dq, dk, dv, rr, rk, rv, wd, ad, gd = seg
    w_rope = jnp.concatenate([q, dq, ks, kw, dk], axis=1).astype(BF16)
    w_plain = jnp.concatenate([kc, vc, vs, vw, dv], axis=1).astype(BF16)
    zpad = lambda n: jnp.zeros((d, n), w_in.dtype)
    w_f32 = jnp.concatenate(
        [rr, rk, rv, wd, ad, gd, zpad(RW_G_PAD - RWKV_G_LORA), ng, zpad(LANES - 3 * NSA_HEADS)]
        + [w_gate[bi] for bi in range(N_BRANCH)], axis=1).astype(BF16)
    mu_p = jnp.concatenate([rwkv_mu, jnp.zeros((RW_G_PAD - RWKV_G_LORA,), rwkv_mu.dtype)])[None, :]
    zz = jnp.zeros((RWKV_W_LORA, RWKV_WIDTH), w_up_lora.dtype)
    wwa = jnp.concatenate([jnp.concatenate([w_up_lora, zz], axis=1),
                           jnp.concatenate([zz, a_up_lora], axis=1)], axis=0).astype(BF16)
    gup = jnp.concatenate([g_up_lora, jnp.zeros((RW_G_PAD - RWKV_G_LORA, RWKV_WIDTH), g_up_lora.dtype)],
                          axis=0).astype(BF16)
    return w_rope, w_plain, w_f32, mu_p, wwa, gup


def _overlap_matrix(n_cp, n_sel):
    c_start = np.arange(n_cp) * NSA_CMP_STRIDE
    c_end = c_start + NSA_CMP_LEN - 1
    j_start = np.arange(n_sel) * NSA_SEL_BLOCK
    ov = (c_start[:, None] <= j_start[None, :] + NSA_SEL_BLOCK - 1) & (c_end[:, None] >= j_start[None, :])
    return jnp.asarray(ov.astype(np.float32)).astype(BF16)


def _chunk_rows(z, groups):
    b, s, _ = z.shape
    z = z.reshape(b, s // NSA_CMP_STRIDE, NSA_CMP_STRIDE, groups, HEAD_DIM)
    return z.transpose(0, 3, 1, 2, 4).reshape(b, groups, s // NSA_CMP_STRIDE, NSA_CMP_STRIDE * HEAD_DIM)


def _nsa_branches(proj_rope, proj_plain, positions, cmp_pos, cmp_w1, cmp_w2, tq):
    b, s, _ = proj_rope.shape
    n_chunk = s // NSA_CMP_STRIDE
    n_sel = s // NSA_SEL_BLOCK
    kc2 = _chunk_rows(proj_plain[:, :, PL_KC:PL_KC + NSA_KV], NSA_GROUPS)
    vc2 = _chunk_rows(proj_plain[:, :, PL_VC:PL_VC + NSA_KV], NSA_GROUPS)
    cmp_end = np.minimum(np.arange(n_chunk) * NSA_CMP_STRIDE + NSA_CMP_LEN - 1, s - 1)
    pos_c = jnp.take(positions, jnp.asarray(cmp_end), axis=1)
    cosc, sinc = _rope_tables(pos_c.reshape(-1))
    cosc = cosc.reshape(b, n_chunk, LANES)
    sinc = sinc.reshape(b, n_chunk, LANES)
    w1 = cmp_w1.astype(BF16)
    w2 = cmp_w2.astype(BF16)
    pe = cmp_pos.reshape(2, 1, NSA_CMP_LEN * HEAD_DIM)
    kcmp = _nsa_compress(kc2, w1[0], w2[0], pe[0], cosc, sinc, True)
    vcmp = _nsa_compress(vc2, w1[1], w2[1], pe[1], cosc, sinc, False)
    o_cmp, sel = _nsa_cmp(proj_rope, kcmp, vcmp, _overlap_matrix(n_chunk, n_sel), tq)
    o_slc = _nsa_flash(proj_rope, proj_plain, sel, "sel", tq, tq)
    o_win = _nsa_flash(proj_rope, proj_plain, None, "win", tq, tq)
    return o_cmp, o_slc, o_win


def _layer(x2, b, s, layer, cosf, sinf, positions, p):
    m = b * s
    w_rope, w_plain, w_f32, mu_p, wwa, gup = _pack_layer_weights(
        p["w_in"], p["w_gate"], p["rwkv_mu"], p["rwkv_w_up"], p["rwkv_a_up"], p["rwkv_g_up"])
    scale = HEAD_DIM ** -0.5
    colscale = jnp.concatenate([jnp.full((ROPE_KS,), scale, F32), jnp.ones((ROPE_COLS - ROPE_KS,), F32)])[None, :]
    g_pre = p["norm_pre_mix"]
    proj_rope = _norm_matmul(x2, g_pre, w_rope, "rope", BF16, rope=(cosf, sinf, colscale))
    proj_plain = _norm_matmul(x2, g_pre, w_plain, "plain", BF16)
    proj_f32 = _norm_matmul(x2, g_pre, w_f32, "plain", F32)
    pr3 = proj_rope.reshape(b, s, ROPE_COLS)
    pp3 = proj_plain.reshape(b, s, PLAIN_COLS)

    tq = min(512, s)
    o_cmp, o_slc, o_win = _nsa_branches(pr3, pp3, positions, p["nsa_cmp_pos"], p["nsa_cmp_w1"], p["nsa_cmp_w2"], tq)
    y_diff = _diff_flash(pr3, pp3, p["diff_lambda"], p["diff_subln"], layer, tq, tq)

    r, k, v, lw, a, g = _rwkv_prep(proj_f32, mu_p, wwa, gup, p["rwkv_w0"][None, :], p["rwkv_a0"][None, :], s)
    sh = lambda z: z.reshape(b, s, RWKV_WIDTH)
    y_rwkv = _rwkv_scan(sh(r), sh(k), sh(v), sh(lw), sh(a), sh(g), p["rwkv_k_k"], p["rwkv_k_a"],
                        p["rwkv_r_k"].reshape(-1), p["rwkv_lnx_w"], p["rwkv_lnx_b"])

    merged = _merge(o_cmp.reshape(m, NSA_WIDTH), o_slc.reshape(m, NSA_WIDTH), o_win.reshape(m, NSA_WIDTH),
                    proj_f32, y_diff.reshape(m, DIFF_WIDTH), y_rwkv.reshape(m, RWKV_WIDTH),
                    p["w_branch"].astype(BF16), p["b_gate"])
    x2 = _matmul_norm_res(merged, p["w_out"].astype(BF16), p["norm_post_mix"], x2)
    hidden = _norm_matmul(x2, p["norm_pre_mlp"], p["w_up"].astype(BF16), "relu2", BF16)
    x2 = _matmul_norm_res(hidden, p["w_down"].astype(BF16), p["norm_post_mlp"], x2)
    return x2


def kernel(x, positions, norm_pre_mix, norm_post_mix, norm_pre_mlp, norm_post_mlp, w_in, nsa_cmp_pos, nsa_cmp_w1, nsa_cmp_w2, diff_lambda, diff_subln, rwkv_mu, rwkv_w0, rwkv_w_up, rwkv_a0, rwkv_a_up, rwkv_g_up, rwkv_k_k, rwkv_k_a, rwkv_r_k, rwkv_lnx_w, rwkv_lnx_b, w_gate, b_gate, w_branch, w_out, w_up, w_down):
    b, s, d = x.shape
    depth = w_in.shape[0]
    stacked = dict(
        norm_pre_mix=norm_pre_mix, norm_post_mix=norm_post_mix, norm_pre_mlp=norm_pre_mlp,
        norm_post_mlp=norm_post_mlp, w_in=w_in, nsa_cmp_pos=nsa_cmp_pos, nsa_cmp_w1=nsa_cmp_w1,
        nsa_cmp_w2=nsa_cmp_w2, diff_lambda=diff_lambda, diff_subln=diff_subln, rwkv_mu=rwkv_mu,
        rwkv_w0=rwkv_w0, rwkv_w_up=rwkv_w_up, rwkv_a0=rwkv_a0, rwkv_a_up=rwkv_a_up, rwkv_g_up=rwkv_g_up,
        rwkv_k_k=rwkv_k_k, rwkv_k_a=rwkv_k_a, rwkv_r_k=rwkv_r_k, rwkv_lnx_w=rwkv_lnx_w,
        rwkv_lnx_b=rwkv_lnx_b, w_gate=w_gate, b_gate=b_gate, w_branch=w_branch, w_out=w_out,
        w_up=w_up, w_down=w_down)
    cosf, sinf = _rope_tables(positions.reshape(-1))
    x2 = x.reshape(b * s, d)
    for layer in range(depth):
        x2 = _layer(x2, b, s, layer, cosf, sinf, positions, {n: a[layer] for n, a in stacked.items()})
    return x2.reshape(b, s, d)
```

```python
import functools
import math

import jax
import jax.numpy as jnp
import numpy as np
from jax import lax
from jax.experimental import pallas as pl
from jax.experimental.pallas import tpu as pltpu

F32 = jnp.float32
BF16 = jnp.bfloat16

D_MODEL = 2048
RMS_EPS = 1e-6
ROPE_THETA = 500000.0
HEAD_DIM = 128
ROT_HALF = HEAD_DIM // 8
NSA_HEADS = 8
NSA_GROUPS = 2
NSA_HPG = NSA_HEADS // NSA_GROUPS
NSA_CMP_LEN = 32
NSA_CMP_STRIDE = 16
NSA_CMP_HIDDEN = 256
NSA_SEL_BLOCK = 64
SEL_SHIFT = NSA_SEL_BLOCK.bit_length() - 1
NSA_SEL_TOPK = 16
NSA_WINDOW = 512
NSA_WIDTH = NSA_HEADS * HEAD_DIM
NSA_KV = NSA_GROUPS * HEAD_DIM
DIFF_HEADS = 4
DIFF_VDIM = 2 * HEAD_DIM
DIFF_WIDTH = DIFF_HEADS * DIFF_VDIM
DIFF_QK = 2 * DIFF_HEADS * HEAD_DIM
RWKV_HEAD = 64
RWKV_WIDTH = 1024
RWKV_W_LORA = 64
RWKV_A_LORA = 64
RWKV_G_LORA = 160
RWKV_LNX_EPS = 64e-5
N_BRANCH = 3
BRANCH_WIDTH = 1024

LANES = 128
SUBLANES = 8
VMEM_LIMIT_BYTES = 56 * 1024 * 1024

NEG = -1e30

ROPE_Q, ROPE_DQ, ROPE_KS, ROPE_KW, ROPE_DK = 0, 1024, 2048, 2304, 2560
ROPE_COLS = 3584
PL_KC, PL_VC, PL_VS, PL_VW, PL_DV = 0, 256, 512, 768, 1024
PLAIN_COLS = 2048
RW_LORA_PAD = 128
RW_G_PAD = 256
RW_COLS = 3 * RWKV_WIDTH + RW_LORA_PAD + RW_G_PAD
F_RW, F_NG, F_BG = 0, RW_COLS, RW_COLS + 128
F32_COLS = RW_COLS + 128 + N_BRANCH * D_MODEL


def _cparams(sem):
    return pltpu.CompilerParams(dimension_semantics=sem, vmem_limit_bytes=VMEM_LIMIT_BYTES)


def _bdot(a, b):
    return jnp.dot(a.astype(BF16), b.astype(BF16), preferred_element_type=F32)


def _dot_nt(a, b):
    return lax.dot_general(a.astype(BF16), b.astype(BF16), (((1,), (1,)), ((), ())),
                           preferred_element_type=F32)


def _dot_tn(a, b):
    return lax.dot_general(a.astype(BF16), b.astype(BF16), (((0,), (0,)), ((), ())),
                           preferred_element_type=F32)


def _split2(x):
    hi = x.astype(BF16)
    lo = (x - hi.astype(F32)).astype(BF16)
    return hi, lo


def _split3(x):
    hi = x.astype(BF16)
    r1 = x - hi.astype(F32)
    mid = r1.astype(BF16)
    lo = (r1 - mid.astype(F32)).astype(BF16)
    return hi, mid, lo


def _dot_exact_lhs(a_bf, x):
    hi, mid, lo = _split3(x)
    d = lambda p: jnp.dot(a_bf, p, preferred_element_type=F32)
    return d(hi) + (d(mid) + d(lo))


def _dot_exact_rhs(x, b_bf):
    hi, mid, lo = _split3(x)
    d = lambda p: jnp.dot(p, b_bf, preferred_element_type=F32)
    return d(hi) + (d(mid) + d(lo))


def _dot3(a, b):
    ah, al = _split2(a)
    bh, bl = _split2(b)
    d = lambda p, q: jnp.dot(p, q, preferred_element_type=F32)
    return d(ah, bh) + (d(ah, bl) + d(al, bh))


def _rope_partner(z, lane):
    return jnp.where(lane < ROT_HALF, pltpu.roll(z, LANES - ROT_HALF, 1), pltpu.roll(z, ROT_HALF, 1))


def _rope_table_kernel(pos_ref, invf_ref, sign_ref, cos_ref, sin_ref):
    ang = pos_ref[...].astype(F32) * invf_ref[...]
    cos_ref[...] = jnp.cos(ang)
    sin_ref[...] = jnp.sin(ang) * sign_ref[...]


def _rope_tables(pos_flat):
    n = pos_flat.shape[0]
    half = ROT_HALF
    inv_freq = ROPE_THETA ** (-jnp.arange(half, dtype=F32) / half)
    zeros = jnp.zeros((LANES - 2 * half,), F32)
    invf = jnp.concatenate([inv_freq, inv_freq, zeros])[None, :]
    sign = jnp.concatenate([-jnp.ones((half,), F32), jnp.ones((half,), F32), zeros])[None, :]
    tm = min(n, 2048)
    assert n % tm == 0
    vec = pl.BlockSpec((1, LANES), lambda i: (0, 0))
    out = pl.BlockSpec((tm, LANES), lambda i: (i, 0))
    return pl.pallas_call(
        _rope_table_kernel,
        out_shape=(jax.ShapeDtypeStruct((n, LANES), F32),) * 2,
        grid=(n // tm,),
        in_specs=[pl.BlockSpec((tm, 1), lambda i: (i, 0)), vec, vec],
        out_specs=(out, out),
        compiler_params=_cparams(("parallel",)),
        name="rope_tables",
    )(pos_flat[:, None], invf, sign)


def _norm_matmul_kernel(x_ref, g_ref, w_ref, *rest, mode, tn):
    if mode == "rope":
        cos_ref, sin_ref, cs_ref, o_ref, u_ref = rest
    else:
        o_ref, u_ref = rest

    @pl.when(pl.program_id(1) == 0)
    def _():
        x = x_ref[...]
        ms = jnp.mean(x * x, axis=-1, keepdims=True)
        u_ref[...] = (x * lax.rsqrt(ms + RMS_EPS) * g_ref[...]).astype(BF16)

    acc = jnp.dot(u_ref[...], w_ref[...], preferred_element_type=F32)
    if mode == "rope":
        acc = acc * cs_ref[...]
        cosv = cos_ref[...]
        sinv = sin_ref[...]
        lane = lax.broadcasted_iota(jnp.int32, cosv.shape, 1)
        for h in range(tn // LANES):
            z = acc[:, h * LANES:(h + 1) * LANES]
            o_ref[:, h * LANES:(h + 1) * LANES] = (z * cosv + _rope_partner(z, lane) * sinv).astype(o_ref.dtype)
    elif mode == "relu2":
        o_ref[...] = jnp.square(jnp.maximum(acc, 0.0)).astype(o_ref.dtype)
    else:
        o_ref[...] = acc.astype(o_ref.dtype)


def _norm_matmul(x, g, w, mode, out_dtype, rope=None, tm=1024, tn=512):
    m, d = x.shape
    n = w.shape[1]
    tm = min(tm, m)
    assert m % tm == 0 and n % tn == 0
    in_specs = [
        pl.BlockSpec((tm, d), lambda i, j: (i, 0)),
        pl.BlockSpec((1, d), lambda i, j: (0, 0)),
        pl.BlockSpec((d, tn), lambda i, j: (0, j)),
    ]
    args = [x, g[None, :], w]
    if mode == "rope":
        cosf, sinf, colscale = rope
        in_specs += [
            pl.BlockSpec((tm, LANES), lambda i, j: (i, 0)),
            pl.BlockSpec((tm, LANES), lambda i, j: (i, 0)),
            pl.BlockSpec((1, tn), lambda i, j: (0, j)),
        ]
        args += [cosf, sinf, colscale]
    return pl.pallas_call(
        functools.partial(_norm_matmul_kernel, mode=mode, tn=tn),
        out_shape=jax.ShapeDtypeStruct((m, n), out_dtype),
        grid=(m // tm, n // tn),
        in_specs=in_specs,
        out_specs=pl.BlockSpec((tm, tn), lambda i, j: (i, j)),
        scratch_shapes=[pltpu.VMEM((tm, d), BF16)],
        compiler_params=_cparams(("parallel", "arbitrary")),
        name="norm_matmul_" + mode,
    )(*args)


def _matmul_norm_res_kernel(a_ref, w_ref, g_ref, res_ref, o_ref, acc_ref):
    k = pl.program_id(1)

    @pl.when(k == 0)
    def _():
        acc_ref[...] = jnp.zeros_like(acc_ref)

    acc_ref[...] += jnp.dot(a_ref[...], w_ref[...], preferred_element_type=F32)

    @pl.when(k == pl.num_programs(1) - 1)
    def _():
        y = acc_ref[...]
        ms = jnp.mean(y * y, axis=-1, keepdims=True)
        o_ref[...] = res_ref[...] + y * lax.rsqrt(ms + RMS_EPS) * g_ref[...]


def _matmul_norm_res(a, w, g, res, tm=512, tk=1024):
    m, kdim = a.shape
    n = w.shape[1]
    tm = min(tm, m)
    assert m % tm == 0 and kdim % tk == 0
    return pl.pallas_call(
        _matmul_norm_res_kernel,
        out_shape=jax.ShapeDtypeStruct((m, n), F32),
        grid=(m // tm, kdim // tk),
        in_specs=[
            pl.BlockSpec((tm, tk), lambda i, k: (i, k)),
            pl.BlockSpec((tk, n), lambda i, k: (k, 0)),
            pl.BlockSpec((1, n), lambda i, k: (0, 0)),
            pl.BlockSpec((tm, n), lambda i, k: (i, 0)),
        ],
        out_specs=pl.BlockSpec((tm, n), lambda i, k: (i, 0)),
        scratch_shapes=[pltpu.VMEM((tm, n), F32)],
        compiler_params=_cparams(("parallel", "arbitrary")),
        name="matmul_norm_res",
    )(a, w, g[None, :], res)


def _gelu_tanh(x):
    return 0.5 * x * (1.0 + jnp.tanh(math.sqrt(2.0 / math.pi) * (x + 0.044715 * (x * x * x))))


def _nsa_compress_kernel(x_ref, w1_ref, w2_ref, pe_ref, cos_ref, sin_ref, o_ref, *, use_rope):
    x = x_ref[0, 0]
    half = x.shape[1]
    n_chunk = x.shape[0]
    a = jnp.dot(x, w1_ref[0:half, :], preferred_element_type=F32)
    b = jnp.dot(x, w1_ref[half:2 * half, :], preferred_element_type=F32)
    pe = jnp.broadcast_to(pe_ref[...], (SUBLANES, pe_ref.shape[1]))
    peb = _dot_exact_rhs_general(pe, w1_ref[...])[0:1, :]
    h = a + pltpu.roll(b, n_chunk - 1, 0) + peb
    y = _bdot(_gelu_tanh(h), w2_ref[...])
    if use_rope:
        lane = lax.broadcasted_iota(jnp.int32, y.shape, 1)
        y = y * cos_ref[0] + _rope_partner(y, lane) * sin_ref[0]
    o_ref[0, 0] = y.astype(o_ref.dtype)


def _dot_exact_rhs_general(x, w_bf):
    hi, mid, lo = _split3(x)
    d = lambda p: jnp.dot(p, w_bf, preferred_element_type=F32)
    return d(hi) + (d(mid) + d(lo))


def _nsa_compress(x2, w1, w2, pe_flat, cosc, sinc, use_rope):
    b, g, n_chunk, width = x2.shape
    dh = w2.shape[1]
    return pl.pallas_call(
        functools.partial(_nsa_compress_kernel, use_rope=use_rope),
        out_shape=jax.ShapeDtypeStruct((b, g, n_chunk, dh), BF16),
        grid=(b, g),
        in_specs=[
            pl.BlockSpec((1, 1, n_chunk, width), lambda i, j: (i, j, 0, 0)),
            pl.BlockSpec(w1.shape, lambda i, j: (0, 0)),
            pl.BlockSpec(w2.shape, lambda i, j: (0, 0)),
            pl.BlockSpec(pe_flat.shape, lambda i, j: (0, 0)),
            pl.BlockSpec((1, n_chunk, dh), lambda i, j: (i, 0, 0)),
            pl.BlockSpec((1, n_chunk, dh), lambda i, j: (i, 0, 0)),
        ],
        out_specs=pl.BlockSpec((1, 1, n_chunk, dh), lambda i, j: (i, j, 0, 0)),
        compiler_params=_cparams(("parallel", "parallel")),
        name="nsa_compress",
    )(x2, w1, w2, pe_flat, cosc, sinc)


def _nsa_cmp_kernel(q_ref, kc_ref, vc_ref, ov_ref, o_ref, sel_ref, *, tq, top, n_sel):
    i = pl.program_id(2)
    kc = kc_ref[0, 0]
    vc = vc_ref[0, 0]
    ncp = kc.shape[0]
    t = i * tq + lax.broadcasted_iota(jnp.int32, (tq, ncp), 0)
    c = lax.broadcasted_iota(jnp.int32, (tq, ncp), 1)
    valid = (c * NSA_CMP_STRIDE + (NSA_CMP_LEN - 1)) <= t
    psum = jnp.zeros((tq, ncp), F32)
    for h in range(NSA_HPG):
        q = q_ref[0, :, h * HEAD_DIM:(h + 1) * HEAD_DIM]
        s = jnp.where(valid, _dot_nt(q, kc), NEG)
        m = jnp.max(s, axis=1, keepdims=True)
        e = jnp.where(valid, jnp.exp2(s - m), 0.0)
        l = jnp.sum(e, axis=1, keepdims=True)
        p = e / jnp.where(l > 0.0, l, 1.0)
        o_ref[0, :, h * HEAD_DIM:(h + 1) * HEAD_DIM] = _bdot(p, vc).astype(o_ref.dtype)
        psum = psum + p
    imp = jnp.transpose(_dot_exact_rhs(psum, ov_ref[...]))[0:n_sel]
    jj = lax.broadcasted_iota(jnp.int32, (n_sel, tq), 0)
    blk_t = jnp.right_shift(i * tq + lax.broadcasted_iota(jnp.int32, (n_sel, tq), 1), SEL_SHIFT)
    forced = (jj == 0) | (jj == blk_t) | (jj == blk_t - 1)
    imp = jnp.where(forced, 1e9, jnp.where(jj > blk_t, -1.0, imp))
    ng = n_sel // SUBLANES
    groups = [imp[g * SUBLANES:(g + 1) * SUBLANES] for g in range(ng)]
    ranks = [jnp.zeros((SUBLANES, tq), F32) for _ in range(ng)]
    sub = lax.broadcasted_iota(jnp.int32, (SUBLANES, tq), 0)
    for ii in range(n_sel):
        gi, ri = divmod(ii, SUBLANES)
        row = jnp.broadcast_to(imp[ii:ii + 1, :], (SUBLANES, tq))
        for g in range(ng):
            if g > gi:
                beats = row >= groups[g]
            elif g < gi:
                beats = row > groups[g]
            else:
                beats = (row > groups[g]) | ((row == groups[g]) & (sub > ri))
            ranks[g] = ranks[g] + jnp.where(beats, 1.0, 0.0)
    chosen = jnp.where(jnp.concatenate(ranks, axis=0) < float(top), 1.0, 0.0)
    pad = sel_ref.shape[3] - n_sel
    if pad:
        chosen = jnp.concatenate([chosen, jnp.zeros((pad, tq), F32)], axis=0)
    sel_ref[0, 0] = jnp.transpose(chosen).astype(sel_ref.dtype)


def _nsa_cmp(proj_rope, kcmp, vcmp, overlap, n_sel, tq):
    b, s, _ = proj_rope.shape
    g = kcmp.shape[1]
    ncp = kcmp.shape[2]
    n_pad = overlap.shape[1]
    top = min(NSA_SEL_TOPK, n_sel)
    qw = NSA_HPG * HEAD_DIM
    return pl.pallas_call(
        functools.partial(_nsa_cmp_kernel, tq=tq, top=top, n_sel=n_sel),
        out_shape=(jax.ShapeDtypeStruct((b, s, NSA_WIDTH), F32),
                   jax.ShapeDtypeStruct((b, g, s, n_pad), BF16)),
        grid=(b, g, s // tq),
        in_specs=[
            pl.BlockSpec((1, tq, qw), lambda bi, gi, i: (bi, i, ROPE_Q // qw + gi)),
            pl.BlockSpec((1, 1, ncp, HEAD_DIM), lambda bi, gi, i: (bi, gi, 0, 0)),
            pl.BlockSpec((1, 1, ncp, HEAD_DIM), lambda bi, gi, i: (bi, gi, 0, 0)),
            pl.BlockSpec(overlap.shape, lambda bi, gi, i: (0, 0)),
        ],
        out_specs=(pl.BlockSpec((1, tq, qw), lambda bi, gi, i: (bi, i, gi)),
                   pl.BlockSpec((1, 1, tq, n_pad), lambda bi, gi, i: (bi, gi, i, 0))),
        compiler_params=_cparams(("parallel", "parallel", "parallel")),
        name="nsa_cmp_select",
    )(proj_rope, kcmp, vcmp, overlap)


def _nsa_flash_kernel(*refs, mode, tq, tk, nkb):
    if mode == "sel":
        q_ref, k_ref, v_ref, sel_ref, o_ref, m_ref, l_ref, acc_ref = refs
    else:
        q_ref, k_ref, v_ref, o_ref, m_ref, l_ref, acc_ref = refs
    i = pl.program_id(2)
    kk = pl.program_id(3)
    if mode == "sel":
        kb = kk
        live = kb <= i
    else:
        kb = i - (nkb - 1) + kk
        live = kb >= 0

    @pl.when(kk == 0)
    def _():
        m_ref[...] = jnp.full_like(m_ref, NEG)
        l_ref[...] = jnp.zeros_like(l_ref)
        acc_ref[...] = jnp.zeros_like(acc_ref)

    @pl.when(live)
    def _():
        rows = i * tq + lax.broadcasted_iota(jnp.int32, (tq, tk), 0)
        cols = kb * tk + lax.broadcasted_iota(jnp.int32, (tq, tk), 1)
        d = rows - cols
        valid = d >= 0
        if mode == "win":
            valid = valid & (d < NSA_WINDOW)
        else:
            n_sel = sel_ref.shape[3]
            jrow = lax.broadcasted_iota(jnp.int32, (n_sel, tk), 0)
            jcol = kb * (tk // NSA_SEL_BLOCK) + jnp.right_shift(
                lax.broadcasted_iota(jnp.int32, (n_sel, tk), 1), SEL_SHIFT)
            expand = jnp.where(jrow == jcol, 1.0, 0.0).astype(BF16)
            chosen = jnp.dot(sel_ref[0, 0], expand, preferred_element_type=F32)
            valid = valid & (chosen > 0.5)
        k = k_ref[0]
        v_ones = jnp.concatenate([v_ref[0], jnp.ones((tk, LANES), BF16)], axis=1)
        for h in range(NSA_HPG):
            q = q_ref[0, :, h * HEAD_DIM:(h + 1) * HEAD_DIM]
            s = jnp.where(valid, _dot_nt(q, k), NEG)
            m_old = m_ref[h]
            m_new = jnp.maximum(m_old, jnp.max(s, axis=1, keepdims=True))
            alpha = jnp.exp2(m_old - m_new)
            p = jnp.exp2(s - jnp.concatenate([m_new] * (tk // LANES), axis=1))
            pv = jnp.dot(p.astype(BF16), v_ones, preferred_element_type=F32)
            l_ref[h] = alpha * l_ref[h] + pv[:, HEAD_DIM:]
            acc_ref[h] = alpha * acc_ref[h] + pv[:, 0:HEAD_DIM]
            m_ref[h] = m_new

    @pl.when(kk == pl.num_programs(3) - 1)
    def _():
        for h in range(NSA_HPG):
            o_ref[0, :, h * HEAD_DIM:(h + 1) * HEAD_DIM] = (acc_ref[h] / l_ref[h]).astype(o_ref.dtype)


def _nsa_flash(proj_rope, proj_plain, sel, mode, tq, tk):
    b, s, _ = proj_rope.shape
    g = NSA_GROUPS
    qw = NSA_HPG * HEAD_DIM
    nq = s // tq
    if mode == "sel":
        assert tq == tk
        nkb = nq
        kidx = lambda i, kk: jnp.minimum(kk, i)
        kcol, vcol = ROPE_KS // HEAD_DIM, PL_VS // HEAD_DIM
    else:
        assert NSA_WINDOW % tk == 0 and tq == tk
        nkb = min(NSA_WINDOW // tk + 1, nq)
        kidx = lambda i, kk: jnp.maximum(i - (nkb - 1) + kk, 0)
        kcol, vcol = ROPE_KW // HEAD_DIM, PL_VW // HEAD_DIM
    in_specs = [
        pl.BlockSpec((1, tq, qw), lambda bi, gi, i, kk: (bi, i, ROPE_Q // qw + gi)),
        pl.BlockSpec((1, tk, HEAD_DIM), lambda bi, gi, i, kk: (bi, kidx(i, kk), kcol + gi)),
        pl.BlockSpec((1, tk, HEAD_DIM), lambda bi, gi, i, kk: (bi, kidx(i, kk), vcol + gi)),
    ]
    args = [proj_rope, proj_rope, proj_plain]
    if mode == "sel":
        n_sel = sel.shape[3]
        in_specs.append(pl.BlockSpec((1, 1, tq, n_sel), lambda bi, gi, i, kk: (bi, gi, i, 0)))
        args.append(sel)
    return pl.pallas_call(
        functools.partial(_nsa_flash_kernel, mode=mode, tq=tq, tk=tk, nkb=nkb),
        out_shape=jax.ShapeDtypeStruct((b, s, NSA_WIDTH), F32),
        grid=(b, g, nq, nkb),
        in_specs=in_specs,
        out_specs=pl.BlockSpec((1, tq, qw), lambda bi, gi, i, kk: (bi, i, gi)),
        scratch_shapes=[
            pltpu.VMEM((NSA_HPG, tq, LANES), F32),
            pltpu.VMEM((NSA_HPG, tq, LANES), F32),
            pltpu.VMEM((NSA_HPG, tq, HEAD_DIM), F32),
        ],
        compiler_params=_cparams(("parallel", "parallel", "parallel", "arbitrary")),
        name="nsa_flash_" + mode,
    )(*args)


def _diff_flash_kernel(q_ref, k_ref, v_ref, lam_ref, sub_ref, o_ref, m_ref, l_ref, acc_ref, *, tq, tk, lam_init):
    i = pl.program_id(2)
    kb = pl.program_id(3)

    @pl.when(kb == 0)
    def _():
        m_ref[...] = jnp.full_like(m_ref, NEG)
        l_ref[...] = jnp.zeros_like(l_ref)
        acc_ref[...] = jnp.zeros_like(acc_ref)

    def step(masked):
        v = v_ref[0]
        if masked:
            valid = (lax.broadcasted_iota(jnp.int32, (tq, tk), 0) >= lax.broadcasted_iota(jnp.int32, (tq, tk), 1))
        for mp in range(2):
            q = q_ref[0, :, mp * HEAD_DIM:(mp + 1) * HEAD_DIM]
            k = k_ref[0, :, mp * HEAD_DIM:(mp + 1) * HEAD_DIM]
            s = _dot_nt(q, k)
            if masked:
                s = jnp.where(valid, s, NEG)
            m_old = m_ref[mp]
            m_new = jnp.maximum(m_old, jnp.max(s, axis=1, keepdims=True))
            alpha = jnp.exp2(m_old - m_new)
            p = jnp.exp2(s - jnp.concatenate([m_new] * (tk // LANES), axis=1))
            l_ref[mp] = alpha * l_ref[mp] + jnp.sum(p, axis=1, keepdims=True)
            acc_ref[mp] = jnp.concatenate([alpha] * (DIFF_VDIM // LANES), axis=1) * acc_ref[mp] + _bdot(p, v)
            m_ref[mp] = m_new

    pl.when(kb < i)(functools.partial(step, False))
    pl.when(kb == i)(functools.partial(step, True))

    @pl.when(kb == pl.num_programs(3) - 1)
    def _():
        lam = lam_ref[...]
        lam_full = (jnp.exp(jnp.sum(lam[0:1, :] * lam[1:2, :], axis=1, keepdims=True))
                    - jnp.exp(jnp.sum(lam[2:3, :] * lam[3:4, :], axis=1, keepdims=True)) + lam_init)
        wide = lambda x: jnp.concatenate([x] * (DIFF_VDIM // LANES), axis=1)
        o = acc_ref[0] / wide(l_ref[0]) - lam_full * (acc_ref[1] / wide(l_ref[1]))
        ms = jnp.mean(o * o, axis=-1, keepdims=True)
        o = o * lax.rsqrt(ms + 1e-5) * sub_ref[...]
        o_ref[0] = (o * (1.0 - lam_init)).astype(o_ref.dtype)


def _diff_flash(proj_rope, proj_plain, lam, subln, layer, tq, tk):
    b, s, _ = proj_rope.shape
    assert tq == tk
    nq = s // tq
    lam_init = 0.8 - 0.6 * math.exp(-0.3 * layer)
    w = DIFF_VDIM
    return pl.pallas_call(
        functools.partial(_diff_flash_kernel, tq=tq, tk=tk, lam_init=lam_init),
        out_shape=jax.ShapeDtypeStruct((b, s, DIFF_WIDTH), BF16),
        grid=(b, DIFF_HEADS, nq, nq),
        in_specs=[
            pl.BlockSpec((1, tq, w), lambda bi, h, i, kb: (bi, i, ROPE_DQ // w + h)),
            pl.BlockSpec((1, tk, w), lambda bi, h, i, kb: (bi, jnp.minimum(kb, i), ROPE_DK // w + h)),
            pl.BlockSpec((1, tk, w), lambda bi, h, i, kb: (bi, jnp.minimum(kb, i), PL_DV // w + h)),
            pl.BlockSpec(lam.shape, lambda bi, h, i, kb: (0, 0)),
            pl.BlockSpec((1, w), lambda bi, h, i, kb: (0, 0)),
        ],
        out_specs=pl.BlockSpec((1, tq, w), lambda bi, h, i, kb: (bi, i, h)),
        scratch_shapes=[
            pltpu.VMEM((2, tq, LANES), F32),
            pltpu.VMEM((2, tq, LANES), F32),
            pltpu.VMEM((2, tq, w), F32),
        ],
        compiler_params=_cparams(("parallel", "parallel", "parallel", "arbitrary")),
        name="diff_flash",
    )(proj_rope, proj_rope, proj_plain, lam, subln[None, :])


def _rwkv_prep_kernel(p_ref, prev_ref, mu_ref, wwa_ref, gup_ref, w0_ref, a0_ref,
                      r_ref, k_ref, v_ref, lw_ref, a_ref, g_ref, *, tm, seq):
    i = pl.program_id(0)
    p = p_ref[...]
    row = lax.broadcasted_iota(jnp.int32, p.shape, 0)
    carry = jnp.where((i * tm) % seq == 0, 0.0, 1.0) * prev_ref[SUBLANES - 1:SUBLANES, :]
    prev = jnp.where(row == 0, carry, pltpu.roll(p, 1, 0))
    xs = p + (prev - p) * mu_ref[...]
    W = RWKV_WIDTH
    r_ref[...] = xs[:, 0:W]
    k_ref[...] = xs[:, W:2 * W]
    v_ref[...] = xs[:, 2 * W:3 * W]
    wa = xs[:, 3 * W:3 * W + RW_LORA_PAD]
    lane = lax.broadcasted_iota(jnp.int32, wa.shape, 1)
    wa = jnp.where(lane < RWKV_W_LORA, jnp.tanh(wa), wa)
    lora = _bdot(wa, wwa_ref[...])
    lw_ref[...] = -math.exp(-0.5) * jax.nn.sigmoid(w0_ref[...] + lora[:, 0:W])
    a_ref[...] = jax.nn.sigmoid(a0_ref[...] + lora[:, W:2 * W])
    gd = xs[:, 3 * W + RW_LORA_PAD:3 * W + RW_LORA_PAD + RW_G_PAD]
    g_ref[...] = _bdot(jax.nn.sigmoid(gd), gup_ref[...])


def _rwkv_prep(proj_f32, mu_p, wwa, gup, w0, a0, seq, tm=256):
    m = proj_f32.shape[0]
    tm = min(tm, seq)
    assert seq % tm == 0 and m % tm == 0
    W = RWKV_WIDTH
    row = pl.BlockSpec((tm, W), lambda i: (i, 0))
    full = lambda a: pl.BlockSpec(a.shape, lambda i: (0,) * a.ndim)
    out = jax.ShapeDtypeStruct((m, W), F32)
    return pl.pallas_call(
        functools.partial(_rwkv_prep_kernel, tm=tm, seq=seq),
        out_shape=(out,) * 6,
        grid=(m // tm,),
        in_specs=[
            pl.BlockSpec((tm, RW_COLS), lambda i: (i, 0)),
            pl.BlockSpec((SUBLANES, RW_COLS), lambda i: (jnp.maximum(i * (tm // SUBLANES) - 1, 0), 0)),
            full(mu_p), full(wwa), full(gup), full(w0), full(a0),
        ],
        out_specs=(row,) * 6,
        compiler_params=_cparams(("parallel",)),
        name="rwkv_prep",
    )(proj_f32, proj_f32, mu_p, wwa, gup, w0, a0)


def _bmm(a, b):
    return jnp.einsum("umk,ukn->umn", a.astype(BF16), b.astype(BF16), preferred_element_type=F32)


def _bmm_nt(a, b):
    return jnp.einsum("umk,unk->umn", a.astype(BF16), b.astype(BF16), preferred_element_type=F32)


def _rwkv_scan_kernel(r_ref, k_ref, v_ref, lw_ref, a_ref, g_ref, kk_ref, ka_ref, rk_ref, lnw_ref, lnb_ref,
                      o_ref, st_ref, *, ts, chunk):
    @pl.when(pl.program_id(2) == 0)
    def _():
        st_ref[...] = jnp.zeros_like(st_ref)

    c = chunk
    nc = ts // c
    r = r_ref[0]
    k = k_ref[0]
    v = v_ref[0]
    lw = lw_ref[0]
    a = a_ref[0]
    head0 = lax.broadcasted_iota(jnp.int32, (ts, LANES), 1) < RWKV_HEAD

    def seg_sum(x):
        s0 = jnp.sum(jnp.where(head0, x, 0.0), axis=1, keepdims=True)
        s1 = jnp.sum(jnp.where(head0, 0.0, x), axis=1, keepdims=True)
        return jnp.where(head0, s0, s1)

    kk = k * kk_ref[...]
    kap = kk / jnp.maximum(jnp.sqrt(seg_sum(kk * kk)), 1e-12)
    kmod = k * (1.0 + (a - 1.0) * ka_ref[...])
    bvec = kap * a

    ri = lax.broadcasted_iota(jnp.int32, (ts, ts), 0)
    ci = lax.broadcasted_iota(jnp.int32, (ts, ts), 1)
    same_chunk = jnp.right_shift(ri, c.bit_length() - 1) == jnp.right_shift(ci, c.bit_length() - 1)
    tri = jnp.where(same_chunk & (ri >= ci), 1.0, 0.0).astype(BF16)
    cum = _dot_exact_lhs(tri, lw)
    g_incl = jnp.exp(cum)
    g_inv = jnp.exp(-cum)
    split = lambda x: x.reshape(nc, c, LANES)
    g_last = [g_incl[(ch + 1) * c - 1:(ch + 1) * c, :] for ch in range(nc)]
    g_last_rows = jnp.concatenate([jnp.broadcast_to(gl, (c, LANES)) for gl in g_last], axis=0)
    kh = split(kmod * g_inv)
    bh = split(bvec * g_inv)
    kg = split(kmod * g_inv * g_last_rows)
    bg = split(bvec * g_inv * g_last_rows)
    kaph = kap * jnp.exp(cum - lw)
    rh = r * g_incl
    v3 = split(v)

    both = lambda x: jnp.concatenate([split(jnp.where(head0, x, 0.0)), split(jnp.where(head0, 0.0, x))], axis=0)
    twice = lambda x: jnp.concatenate([x, x], axis=0)
    kap_m = both(kaph)
    r_m = both(rh)
    gram = _bmm_nt(jnp.concatenate([kap_m, r_m], axis=1), twice(jnp.concatenate([bh, kh], axis=1)))
    row = lax.broadcasted_iota(jnp.int32, (c, 2 * c), 0)
    col = lax.broadcasted_iota(jnp.int32, (c, 2 * c), 1) & (c - 1)
    lbk = jnp.where(row > col, gram[:, 0:c, :], 0.0)
    abk = jnp.where(row >= col, gram[:, c:2 * c, :], 0.0)
    lb = lbk[:, :, 0:c]
    tm = -lb
    q = _bmm(lb, lb)
    n = 2
    while True:
        tm = tm + q + _bmm(tm, q)
        n *= 2
        if n >= c:
            break
        q = _bmm(q, q)
    v2 = twice(v3)
    kap_t = kap_m + _bmm(tm, kap_m)
    lkv = _bmm(lbk, jnp.concatenate([jnp.zeros_like(v2), v2], axis=1))
    z0 = -(lkv + _bmm(tm, lkv))
    r_t = r_m - _bmm(abk, jnp.concatenate([kap_t, jnp.zeros_like(kap_t)], axis=1))
    y0 = _bmm(abk, jnp.concatenate([z0, v2], axis=1))
    h0 = lax.broadcasted_iota(jnp.int32, (nc, c, LANES), 2) < RWKV_HEAD
    merge = lambda x: jnp.where(h0, x[0:nc], x[nc:2 * nc])
    kap_t = kap_t[0:nc] + kap_t[nc:2 * nc]
    r_t = r_t[0:nc] + r_t[nc:2 * nc]
    z0 = merge(z0)
    y0 = merge(y0)

    br = lax.broadcasted_iota(jnp.int32, (LANES, LANES), 0)
    bc = lax.broadcasted_iota(jnp.int32, (LANES, LANES), 1)
    blockdiag = (br < RWKV_HEAD) == (bc < RWKV_HEAD)
    state = st_ref[...]
    ys = []
    for ch in range(nc):
        trans = jnp.where(blockdiag, -_dot_tn(bg[ch], kap_t[ch]), 0.0)
        trans = trans + jnp.where(br == bc, jnp.broadcast_to(g_last[ch], (LANES, LANES)), 0.0)
        inject = jnp.where(blockdiag, _dot_tn(jnp.concatenate([bg[ch], kg[ch]], axis=0),
                                              jnp.concatenate([z0[ch], v3[ch]], axis=0)), 0.0)
        ys.append(_dot3(r_t[ch], state) + y0[ch])
        state = _dot3(trans, state) + inject
    st_ref[...] = state
    y = jnp.concatenate(ys, axis=0)

    mean = seg_sum(y) * (1.0 / RWKV_HEAD)
    yc = y - mean
    var = seg_sum(yc * yc) * (1.0 / RWKV_HEAD)
    yn = yc * lax.rsqrt(var + RWKV_LNX_EPS) * lnw_ref[...] + lnb_ref[...]
    bonus = seg_sum(r * kmod * rk_ref[...]) * v
    o_ref[0] = ((yn + bonus) * g_ref[0]).astype(o_ref.dtype)


def _rwkv_scan(r, k, v, lw, a, g, k_k, k_a, r_k, lnx_w, lnx_b, ts=512, chunk=64):
    b, s, w = r.shape
    ts = min(ts, s)
    assert s % ts == 0 and ts % chunk == 0
    seq = pl.BlockSpec((1, ts, LANES), lambda bi, j, t: (bi, t, j))
    vec = pl.BlockSpec((1, LANES), lambda bi, j, t: (0, j))
    return pl.pallas_call(
        functools.partial(_rwkv_scan_kernel, ts=ts, chunk=chunk),
        out_shape=jax.ShapeDtypeStruct((b, s, w), BF16),
        grid=(b, w // LANES, s // ts),
        in_specs=[seq] * 6 + [vec] * 5,
        out_specs=seq,
        scratch_shapes=[pltpu.VMEM((LANES, LANES), F32)],
        compiler_params=_cparams(("parallel", "parallel", "arbitrary")),
        name="rwkv_scan",
    )(r, k, v, lw, a, g, k_k[None, :], k_a[None, :], r_k[None, :], lnx_w[None, :], lnx_b[None, :])


def _merge_kernel(oc_ref, os_ref, ow_ref, ng_ref, yd_ref, yr_ref, wb_ref, g0_ref, g1_ref, g2_ref, bias_ref,
                  o_ref, yn_ref):
    @pl.when(pl.program_id(1) == 0)
    def _():
        gates = jax.nn.sigmoid(ng_ref[...])
        for hd in range(NSA_HEADS):
            sl = slice(hd * HEAD_DIM, (hd + 1) * HEAD_DIM)
            o = (gates[:, 3 * hd:3 * hd + 1] * oc_ref[:, sl]
                 + gates[:, 3 * hd + 1:3 * hd + 2] * os_ref[:, sl]
                 + gates[:, 3 * hd + 2:3 * hd + 3] * ow_ref[:, sl])
            yn_ref[:, sl] = o.astype(BF16)

    branches = (yn_ref, yd_ref, yr_ref)
    graw = (g0_ref, g1_ref, g2_ref)
    acc = None
    for bi in range(N_BRANCH):
        gate = jax.nn.sigmoid(graw[bi][...] + bias_ref[bi])
        term = gate * jnp.dot(branches[bi][...], wb_ref[bi], preferred_element_type=F32)
        acc = term if acc is None else acc + term
    o_ref[...] = acc.astype(o_ref.dtype)


def _merge(o_cmp, o_slc, o_win, proj_f32, y_diff, y_rwkv, w_branch, b_gate, tm=512, tn=512):
    m = o_cmp.shape[0]
    n = w_branch.shape[2]
    tm = min(tm, m)
    assert m % tm == 0 and n % tn == 0 and F_BG % tn == 0 and n % tn == 0
    bw = BRANCH_WIDTH
    rowblk = lambda: pl.BlockSpec((tm, bw), lambda i, j: (i, 0))
    gate_spec = lambda bi: pl.BlockSpec((tm, tn), lambda i, j: (i, F_BG // tn + bi * (n // tn) + j))
    return pl.pallas_call(
        _merge_kernel,
        out_shape=jax.ShapeDtypeStruct((m, n), BF16),
        grid=(m // tm, n // tn),
        in_specs=[
            rowblk(), rowblk(), rowblk(),
            pl.BlockSpec((tm, LANES), lambda i, j: (i, F_NG // LANES)),
            rowblk(), rowblk(),
            pl.BlockSpec((N_BRANCH, bw, tn), lambda i, j: (0, 0, j)),
            gate_spec(0), gate_spec(1), gate_spec(2),
            pl.BlockSpec((N_BRANCH, 1, tn), lambda i, j: (0, 0, j)),
        ],
        out_specs=pl.BlockSpec((tm, tn), lambda i, j: (i, j)),
        scratch_shapes=[pltpu.VMEM((tm, bw), BF16)],
        compiler_params=_cparams(("parallel", "arbitrary")),
        name="gated_merge",
    )(o_cmp, o_slc, o_win, proj_f32, y_diff, y_rwkv, w_branch, proj_f32, proj_f32, proj_f32, b_gate[:, None, :])


def _pack_layer_weights(w_in, w_gate, rwkv_mu, w_up_lora, a_up_lora, g_up_lora):
    d = w_in.shape[0]
    nsa_sizes = (NSA_WIDTH,) + (NSA_KV,) * 6 + (3 * NSA_HEADS,)
    diff_sizes = (DIFF_QK, DIFF_QK, DIFF_WIDTH)
    rw_sizes = (RWKV_WIDTH,) * 3 + (RWKV_W_LORA, RWKV_A_LORA, RWKV_G_LORA)
    offs = np.cumsum((0,) + nsa_sizes + diff_sizes + rw_sizes)
    seg = [w_in[:, offs[i]:offs[i + 1]] for i in range(len(offs) - 1)]
    q, kc, vc, ks, vs, kw, vw, ng, dq, dk, dv, rr, rk, rv, wd, ad, gd = seg
    w_rope = jnp.concatenate([q, dq, ks, kw, dk], axis=1).astype(BF16)
    w_plain = jnp.concatenate([kc, vc, vs, vw, dv], axis=1).astype(BF16)
    zpad = lambda n: jnp.zeros((d, n), w_in.dtype)
    w_f32 = jnp.concatenate(
        [rr, rk, rv, wd, ad, gd, zpad(RW_G_PAD - RWKV_G_LORA), ng, zpad(LANES - 3 * NSA_HEADS)]
        + [w_gate[bi] for bi in range(N_BRANCH)], axis=1).astype(BF16)
    mu_p = jnp.concatenate([rwkv_mu, jnp.zeros((RW_G_PAD - RWKV_G_LORA,), rwkv_mu.dtype)])[None, :]
    zz = jnp.zeros((RWKV_W_LORA, RWKV_WIDTH), w_up_lora.dtype)
    wwa = jnp.concatenate([jnp.concatenate([w_up_lora, zz], axis=1),
                           jnp.concatenate([zz, a_up_lora], axis=1)], axis=0).astype(BF16)
    gup = jnp.concatenate([g_up_lora, jnp.zeros((RW_G_PAD - RWKV_G_LORA, RWKV_WIDTH), g_up_lora.dtype)],
                          axis=0).astype(BF16)
    return w_rope, w_plain, w_f32, mu_p, wwa, gup


def _overlap_matrix(n_cp, n_sel):
    c_start = np.arange(n_cp) * NSA_CMP_STRIDE
    c_end = c_start + NSA_CMP_LEN - 1
    j_start = np.arange(n_sel) * NSA_SEL_BLOCK
    ov = (c_start[:, None] <= j_start[None, :] + NSA_SEL_BLOCK - 1) & (c_end[:, None] >= j_start[None, :])
    ov = np.pad(ov.astype(np.float32), ((0, 0), (0, -n_sel % LANES)))
    return jnp.asarray(ov).astype(BF16)


def _chunk_rows(z, groups):
    b, s, _ = z.shape
    z = z.reshape(b, s // NSA_CMP_STRIDE, NSA_CMP_STRIDE, groups, HEAD_DIM)
    return z.transpose(0, 3, 1, 2, 4).reshape(b, groups, s // NSA_CMP_STRIDE, NSA_CMP_STRIDE * HEAD_DIM)


def _nsa_branches(proj_rope, proj_plain, positions, cmp_pos, cmp_w1, cmp_w2, tq):
    b, s, _ = proj_rope.shape
    n_chunk = s // NSA_CMP_STRIDE
    n_sel = s // NSA_SEL_BLOCK
    kc2 = _chunk_rows(proj_plain[:, :, PL_KC:PL_KC + NSA_KV], NSA_GROUPS)
    vc2 = _chunk_rows(proj_plain[:, :, PL_VC:PL_VC + NSA_KV], NSA_GROUPS)
    cmp_end = np.minimum(np.arange(n_chunk) * NSA_CMP_STRIDE + NSA_CMP_LEN - 1, s - 1)
    pos_c = jnp.take(positions, jnp.asarray(cmp_end), axis=1)
    cosc, sinc = _rope_tables(pos_c.reshape(-1))
    cosc = cosc.reshape(b, n_chunk, LANES)
    sinc = sinc.reshape(b, n_chunk, LANES)
    w1 = cmp_w1.astype(BF16)
    w2 = cmp_w2.astype(BF16)
    pe = cmp_pos.reshape(2, 1, NSA_CMP_LEN * HEAD_DIM)
    kcmp = _nsa_compress(kc2, w1[0], w2[0], pe[0], cosc, sinc, True)
    vcmp = _nsa_compress(vc2, w1[1], w2[1], pe[1], cosc, sinc, False)
    o_cmp, sel = _nsa_cmp(proj_rope, kcmp, vcmp, _overlap_matrix(n_chunk, n_sel), n_sel, tq)
    o_slc = _nsa_flash(proj_rope, proj_plain, sel, "sel", tq, tq)
    o_win = _nsa_flash(proj_rope, proj_plain, None, "win", tq, tq)
    return o_cmp, o_slc, o_win


def _layer(x2, b, s, layer, cosf, sinf, positions, p):
    m = b * s
    w_rope, w_plain, w_f32, mu_p, wwa, gup = _pack_layer_weights(
        p["w_in"], p["w_gate"], p["rwkv_mu"], p["rwkv_w_up"], p["rwkv_a_up"], p["rwkv_g_up"])
    scale = HEAD_DIM ** -0.5 * math.log2(math.e)
    colscale =jnp.concatenate([jnp.full((ROPE_KS,), scale, F32), jnp.ones((ROPE_COLS - ROPE_KS,), F32)])[None, :]
    g_pre = p["norm_pre_mix"]
    proj_rope = _norm_matmul(x2, g_pre, w_rope, "rope", BF16, rope=(cosf, sinf, colscale))
    proj_plain = _norm_matmul(x2, g_pre, w_plain, "plain", BF16)
    proj_f32 = _norm_matmul(x2, g_pre, w_f32, "plain", F32)
    pr3 = proj_rope.reshape(b, s, ROPE_COLS)
    pp3 = proj_plain.reshape(b, s, PLAIN_COLS)

    tq = min(512, s)
    o_cmp, o_slc, o_win = _nsa_branches(pr3, pp3, positions, p["nsa_cmp_pos"], p["nsa_cmp_w1"], p["nsa_cmp_w2"], tq)
    y_diff = _diff_flash(pr3, pp3, p["diff_lambda"], p["diff_subln"], layer, tq, tq)

    r, k, v, lw, a, g = _rwkv_prep(proj_f32, mu_p, wwa, gup, p["rwkv_w0"][None, :], p["rwkv_a0"][None, :], s)
    sh = lambda z: z.reshape(b, s, RWKV_WIDTH)
    y_rwkv = _rwkv_scan(sh(r), sh(k), sh(v), sh(lw), sh(a), sh(g), p["rwkv_k_k"], p["rwkv_k_a"],
                        p["rwkv_r_k"].reshape(-1), p["rwkv_lnx_w"], p["rwkv_lnx_b"])

    merged = _merge(o_cmp.reshape(m, NSA_WIDTH), o_slc.reshape(m, NSA_WIDTH), o_win.reshape(m, NSA_WIDTH),
                    proj_f32, y_diff.reshape(m, DIFF_WIDTH), y_rwkv.reshape(m, RWKV_WIDTH),
                    p["w_branch"].astype(BF16), p["b_gate"])
    x2 = _matmul_norm_res(merged, p["w_out"].astype(BF16), p["norm_post_mix"], x2)
    hidden = _norm_matmul(x2, p["norm_pre_mlp"], p["w_up"].astype(BF16), "relu2", BF16)
    x2 = _matmul_norm_res(hidden, p["w_down"].astype(BF16), p["norm_post_mlp"], x2)
    return x2


def kernel(x, positions, norm_pre_mix, norm_post_mix, norm_pre_mlp, norm_post_mlp, w_in, nsa_cmp_pos, nsa_cmp_w1, nsa_cmp_w2, diff_lambda, diff_subln, rwkv_mu, rwkv_w0, rwkv_w_up, rwkv_a0, rwkv_a_up, rwkv_g_up, rwkv_k_k, rwkv_k_a, rwkv_r_k, rwkv_lnx_w, rwkv_lnx_b, w_gate, b_gate, w_branch, w_out, w_up, w_down):
    b, s, d = x.shape
    depth = w_in.shape[0]
    stacked = dict(
        norm_pre_mix=norm_pre_mix, norm_post_mix=norm_post_mix, norm_pre_mlp=norm_pre_mlp,
        norm_post_mlp=norm_post_mlp, w_in=w_in, nsa_cmp_pos=nsa_cmp_pos, nsa_cmp_w1=nsa_cmp_w1,
        nsa_cmp_w2=nsa_cmp_w2, diff_lambda=diff_lambda, diff_subln=diff_subln, rwkv_mu=rwkv_mu,
        rwkv_w0=rwkv_w0, rwkv_w_up=rwkv_w_up, rwkv_a0=rwkv_a0, rwkv_a_up=rwkv_a_up, rwkv_g_up=rwkv_g_up,
        rwkv_k_k=rwkv_k_k, rwkv_k_a=rwkv_k_a, rwkv_r_k=rwkv_r_k, rwkv_lnx_w=rwkv_lnx_w,
        rwkv_lnx_b=rwkv_lnx_b, w_gate=w_gate, b_gate=b_gate, w_branch=w_branch, w_out=w_out,
        w_up=w_up, w_down=w_down)
    cosf, sinf = _rope_tables(positions.reshape(-1))
    x2 = x.reshape(b * s, d)
    for layer in range(depth):
        x2 = _layer(x2, b, s, layer, cosf, sinf, positions, {n: a[layer] for n, a in stacked.items()})
    return x2.reshape(b, s, d)
```

```python
import functools
import math

import jax
import jax.numpy as jnp
import numpy as np
from jax import lax
from jax.experimental import pallas as pl
from jax.experimental.pallas import tpu as pltpu

F32 = jnp.float32
BF16 = jnp.bfloat16

D_MODEL = 2048
RMS_EPS = 1e-6
ROPE_THETA = 500000.0
HEAD_DIM = 128
ROT_HALF = HEAD_DIM // 8
NSA_HEADS = 8
NSA_GROUPS = 2
NSA_HPG = NSA_HEADS // NSA_GROUPS
NSA_CMP_LEN = 32
NSA_CMP_STRIDE = 16
NSA_CMP_HIDDEN = 256
NSA_SEL_BLOCK = 64
SEL_SHIFT = NSA_SEL_BLOCK.bit_length() - 1
NSA_SEL_TOPK = 16
NSA_WINDOW = 512
NSA_WIDTH = NSA_HEADS * HEAD_DIM
NSA_KV = NSA_GROUPS * HEAD_DIM
DIFF_HEADS = 4
DIFF_VDIM = 2 * HEAD_DIM
DIFF_WIDTH = DIFF_HEADS * DIFF_VDIM
DIFF_QK = 2 * DIFF_HEADS * HEAD_DIM
RWKV_HEAD = 64
RWKV_WIDTH = 1024
RWKV_W_LORA = 64
RWKV_A_LORA = 64
RWKV_G_LORA = 160
RWKV_LNX_EPS = 64e-5
N_BRANCH = 3
BRANCH_WIDTH = 1024

LANES = 128
SUBLANES = 8
VMEM_LIMIT_BYTES = 56 * 1024 * 1024

NEG = -1e30

ROPE_Q, ROPE_DQ, ROPE_KS, ROPE_KW, ROPE_DK = 0, 1024, 2048, 2304, 2560
ROPE_COLS = 3584
PL_KC, PL_VC, PL_VS, PL_VW, PL_DV = 0, 256, 512, 768, 1024
PLAIN_COLS = 2048
RW_LORA_PAD = 128
RW_G_PAD = 256
RW_GD = 3 * RWKV_WIDTH
RW_WA = RW_GD + RW_G_PAD
RW_COLS = RW_WA + RW_LORA_PAD
F_RW, F_NG, F_BG = 0, RW_COLS, RW_COLS + 128
F32_COLS = RW_COLS + 128 + N_BRANCH * D_MODEL


def _cparams(sem):
    return pltpu.CompilerParams(dimension_semantics=sem, vmem_limit_bytes=VMEM_LIMIT_BYTES)


def _bdot(a, b):
    return jnp.dot(a.astype(BF16), b.astype(BF16), preferred_element_type=F32)


def _dot_nt(a, b):
    return lax.dot_general(a.astype(BF16), b.astype(BF16), (((1,), (1,)), ((), ())),
                           preferred_element_type=F32)


def _dot_tn(a, b):
    return lax.dot_general(a.astype(BF16), b.astype(BF16), (((0,), (0,)), ((), ())),
                           preferred_element_type=F32)


def _split2(x):
    hi = x.astype(BF16)
    lo = (x - hi.astype(F32)).astype(BF16)
    return hi, lo


def _split3(x):
    hi = x.astype(BF16)
    r1 = x - hi.astype(F32)
    mid = r1.astype(BF16)
    lo = (r1 - mid.astype(F32)).astype(BF16)
    return hi, mid, lo


def _dot_exact_lhs(a_bf, x):
    hi, mid, lo = _split3(x)
    d = lambda p: jnp.dot(a_bf, p, preferred_element_type=F32)
    return d(hi) + (d(mid) + d(lo))


def _dot_exact_rhs(x, b_bf):
    hi, mid, lo = _split3(x)
    d = lambda p: jnp.dot(p, b_bf, preferred_element_type=F32)
    return d(hi) + (d(mid) + d(lo))


def _dot3(a, b):
    ah, al = _split2(a)
    bh, bl = _split2(b)
    d = lambda p, q: jnp.dot(p, q, preferred_element_type=F32)
    return d(ah, bh) + (d(ah, bl) + d(al, bh))


def _rope_partner(z, lane):
    return jnp.where(lane < ROT_HALF, pltpu.roll(z, LANES - ROT_HALF, 1), pltpu.roll(z, ROT_HALF, 1))


def _rope_table_kernel(pos_ref, invf_ref, sign_ref, cos_ref, sin_ref):
    ang = pos_ref[...].astype(F32) * invf_ref[...]
    cos_ref[...] = jnp.cos(ang)
    sin_ref[...] = jnp.sin(ang) * sign_ref[...]


def _rope_tables(pos_flat):
    n = pos_flat.shape[0]
    half = ROT_HALF
    inv_freq = ROPE_THETA ** (-jnp.arange(half, dtype=F32) / half)
    zeros = jnp.zeros((LANES - 2 * half,), F32)
    invf = jnp.concatenate([inv_freq, inv_freq, zeros])[None, :]
    sign = jnp.concatenate([-jnp.ones((half,), F32), jnp.ones((half,), F32), zeros])[None, :]
    tm = min(n, 2048)
    assert n % tm == 0
    vec = pl.BlockSpec((1, LANES), lambda i: (0, 0))
    out = pl.BlockSpec((tm, LANES), lambda i: (i, 0))
    return pl.pallas_call(
        _rope_table_kernel,
        out_shape=(jax.ShapeDtypeStruct((n, LANES), F32),) * 2,
        grid=(n // tm,),
        in_specs=[pl.BlockSpec((tm, 1), lambda i: (i, 0)), vec, vec],
        out_specs=(out, out),
        compiler_params=_cparams(("parallel",)),
        name="rope_tables",
    )(pos_flat[:, None], invf, sign)


def _norm_matmul_kernel(x_ref, g_ref, w_ref, *rest, mode, tn):
    if mode == "rope":
        cos_ref, sin_ref, cs_ref, o_ref, u_ref = rest
    else:
        o_ref, u_ref = rest

    @pl.when(pl.program_id(1) == 0)
    def _():
        x = x_ref[...]
        ms = jnp.mean(x * x, axis=-1, keepdims=True)
        u_ref[...] = (x * lax.rsqrt(ms + RMS_EPS) * g_ref[...]).astype(BF16)

    acc = jnp.dot(u_ref[...], w_ref[...], preferred_element_type=F32)
    if mode == "rope":
        acc = acc * cs_ref[...]
        cosv = cos_ref[...]
        sinv = sin_ref[...]
        lane = lax.broadcasted_iota(jnp.int32, cosv.shape, 1)
        for h in range(tn // LANES):
            z = acc[:, h * LANES:(h + 1) * LANES]
            o_ref[:, h * LANES:(h + 1) * LANES] = (z * cosv + _rope_partner(z, lane) * sinv).astype(o_ref.dtype)
    elif mode == "relu2":
        o_ref[...] = jnp.square(jnp.maximum(acc, 0.0)).astype(o_ref.dtype)
    else:
        o_ref[...] = acc.astype(o_ref.dtype)


def _norm_matmul(x, g, w, mode, out_dtype, rope=None, tm=1024, tn=512):
    m, d = x.shape
    n = w.shape[1]
    tm = min(tm, m)
    assert m % tm == 0 and n % tn == 0
    in_specs = [
        pl.BlockSpec((tm, d), lambda i, j: (i, 0)),
        pl.BlockSpec((1, d), lambda i, j: (0, 0)),
        pl.BlockSpec((d, tn), lambda i, j: (0, j)),
    ]
    args = [x, g[None, :], w]
    if mode == "rope":
        cosf, sinf, colscale = rope
        in_specs += [
            pl.BlockSpec((tm, LANES), lambda i, j: (i, 0)),
            pl.BlockSpec((tm, LANES), lambda i, j: (i, 0)),
            pl.BlockSpec((1, tn), lambda i, j: (0, j)),
        ]
        args += [cosf, sinf, colscale]
    return pl.pallas_call(
        functools.partial(_norm_matmul_kernel, mode=mode, tn=tn),
        out_shape=jax.ShapeDtypeStruct((m, n), out_dtype),
        grid=(m // tm, n // tn),
        in_specs=in_specs,
        out_specs=pl.BlockSpec((tm, tn), lambda i, j: (i, j)),
        scratch_shapes=[pltpu.VMEM((tm, d), BF16)],
        compiler_params=_cparams(("parallel", "arbitrary")),
        name="norm_matmul_" + mode,
    )(*args)


def _matmul_norm_res_kernel(a_ref, w_ref, g_ref, res_ref, o_ref, acc_ref):
    k = pl.program_id(1)

    @pl.when(k == 0)
    def _():
        acc_ref[...] = jnp.zeros_like(acc_ref)

    acc_ref[...] += jnp.dot(a_ref[...], w_ref[...], preferred_element_type=F32)

    @pl.when(k == pl.num_programs(1) - 1)
    def _():
        y = acc_ref[...]
        ms = jnp.mean(y * y, axis=-1, keepdims=True)
        o_ref[...] = res_ref[...] + y * lax.rsqrt(ms + RMS_EPS) * g_ref[...]


def _matmul_norm_res(a, w, g, res, tm=512, tk=1024):
    m, kdim = a.shape
    n = w.shape[1]
    tm = min(tm, m)
    assert m % tm == 0 and kdim % tk == 0
    return pl.pallas_call(
        _matmul_norm_res_kernel,
        out_shape=jax.ShapeDtypeStruct((m, n), F32),
        grid=(m // tm, kdim // tk),
        in_specs=[
            pl.BlockSpec((tm, tk), lambda i, k: (i, k)),
            pl.BlockSpec((tk, n), lambda i, k: (k, 0)),
            pl.BlockSpec((1, n), lambda i, k: (0, 0)),
            pl.BlockSpec((tm, n), lambda i, k: (i, 0)),
        ],
        out_specs=pl.BlockSpec((tm, n), lambda i, k: (i, 0)),
        scratch_shapes=[pltpu.VMEM((tm, n), F32)],
        compiler_params=_cparams(("parallel", "arbitrary")),
        name="matmul_norm_res",
    )(a, w, g[None, :], res)


def _gelu_tanh(x):
    return 0.5 * x * (1.0 + jnp.tanh(math.sqrt(2.0 / math.pi) * (x + 0.044715 * (x * x * x))))


def _nsa_compress_kernel(x_ref, w1_ref, w2_ref, pe_ref, cos_ref, sin_ref, o_ref, *, use_rope):
    x = x_ref[0, 0]
    half = x.shape[1]
    n_chunk = x.shape[0]
    a = jnp.dot(x, w1_ref[0:half, :], preferred_element_type=F32)
    b = jnp.dot(x, w1_ref[half:2 * half, :], preferred_element_type=F32)
    pe = jnp.broadcast_to(pe_ref[...], (SUBLANES, pe_ref.shape[1]))
    peb = _dot_exact_rhs_general(pe, w1_ref[...])[0:1, :]
    h = a + pltpu.roll(b, n_chunk - 1, 0) + peb
    y = _bdot(_gelu_tanh(h), w2_ref[...])
    if use_rope:
        lane = lax.broadcasted_iota(jnp.int32, y.shape, 1)
        y = y * cos_ref[0] + _rope_partner(y, lane) * sin_ref[0]
    o_ref[0, 0] = y.astype(o_ref.dtype)


def _dot_exact_rhs_general(x, w_bf):
    hi, mid, lo = _split3(x)
    d = lambda p: jnp.dot(p, w_bf, preferred_element_type=F32)
    return d(hi) + (d(mid) + d(lo))


def _nsa_compress(x2, w1, w2, pe_flat, cosc, sinc, use_rope):
    b, g, n_chunk, width = x2.shape
    dh = w2.shape[1]
    return pl.pallas_call(
        functools.partial(_nsa_compress_kernel, use_rope=use_rope),
        out_shape=jax.ShapeDtypeStruct((b, g, n_chunk, dh), BF16),
        grid=(b, g),
        in_specs=[
            pl.BlockSpec((1, 1, n_chunk, width), lambda i, j: (i, j, 0, 0)),
            pl.BlockSpec(w1.shape, lambda i, j: (0, 0)),
            pl.BlockSpec(w2.shape, lambda i, j: (0, 0)),
            pl.BlockSpec(pe_flat.shape, lambda i, j: (0, 0)),
            pl.BlockSpec((1, n_chunk, dh), lambda i, j: (i, 0, 0)),
            pl.BlockSpec((1, n_chunk, dh), lambda i, j: (i, 0, 0)),
        ],
        out_specs=pl.BlockSpec((1, 1, n_chunk, dh), lambda i, j: (i, j, 0, 0)),
        compiler_params=_cparams(("parallel", "parallel")),
        name="nsa_compress",
    )(x2, w1, w2, pe_flat, cosc, sinc)


def _nsa_cmp_kernel(q_ref, kc_ref, vc_ref, ov_ref, o_ref, sel_ref, *, tq, top, n_sel):
    i = pl.program_id(2)
    kc = kc_ref[0, 0]
    vc = vc_ref[0, 0]
    ncp = kc.shape[0]
    t = i * tq + lax.broadcasted_iota(jnp.int32, (tq, ncp), 0)
    c = lax.broadcasted_iota(jnp.int32, (tq, ncp), 1)
    valid = (c * NSA_CMP_STRIDE + (NSA_CMP_LEN - 1)) <= t
    psum = jnp.zeros((tq, ncp), F32)
    for h in range(NSA_HPG):
        q = q_ref[0, :, h * HEAD_DIM:(h + 1) * HEAD_DIM]
        s = jnp.where(valid, _dot_nt(q, kc), NEG)
        m = jnp.max(s, axis=1, keepdims=True)
        e = jnp.where(valid, jnp.exp2(s - m), 0.0)
        l = jnp.sum(e, axis=1, keepdims=True)
        p = e / jnp.where(l > 0.0, l, 1.0)
        o_ref[0, :, h * HEAD_DIM:(h + 1) * HEAD_DIM] = _bdot(p, vc).astype(o_ref.dtype)
        psum = psum + p
    imp = jnp.transpose(_dot_exact_rhs(psum, ov_ref[...]))[0:n_sel]
    jj = lax.broadcasted_iota(jnp.int32, (n_sel, tq), 0)
    blk_t = jnp.right_shift(i * tq + lax.broadcasted_iota(jnp.int32, (n_sel, tq), 1), SEL_SHIFT)
    forced = (jj == 0) | (jj == blk_t) | (jj == blk_t - 1)
    imp = jnp.where(forced, 1e9, jnp.where(jj > blk_t, -1.0, imp))
    ng = n_sel // SUBLANES
    groups = [imp[g * SUBLANES:(g + 1) * SUBLANES] for g in range(ng)]
    ranks = [jnp.zeros((SUBLANES, tq), F32) for _ in range(ng)]
    sub = lax.broadcasted_iota(jnp.int32, (SUBLANES, tq), 0)
    for ii in range(n_sel):
        gi, ri = divmod(ii, SUBLANES)
        row = jnp.broadcast_to(imp[ii:ii + 1, :], (SUBLANES, tq))
        for g in range(ng):
            if g > gi:
                beats = row >= groups[g]
            elif g < gi:
                beats = row > groups[g]
            else:
                beats = (row > groups[g]) | ((row == groups[g]) & (sub > ri))
            ranks[g] = ranks[g] + jnp.where(beats, 1.0, 0.0)
    chosen = jnp.where(jnp.concatenate(ranks, axis=0) < float(top), 1.0, 0.0)
    pad = sel_ref.shape[3] - n_sel
    if pad:
        chosen = jnp.concatenate([chosen, jnp.zeros((pad, tq), F32)], axis=0)
    sel_ref[0, 0] = jnp.transpose(chosen).astype(sel_ref.dtype)


def _nsa_cmp(proj_rope, kcmp, vcmp, overlap, n_sel, tq):
    b, s, _ = proj_rope.shape
    g = kcmp.shape[1]
    ncp = kcmp.shape[2]
    n_pad = overlap.shape[1]
    top = min(NSA_SEL_TOPK, n_sel)
    qw = NSA_HPG * HEAD_DIM
    return pl.pallas_call(
        functools.partial(_nsa_cmp_kernel, tq=tq, top=top, n_sel=n_sel),
        out_shape=(jax.ShapeDtypeStruct((b, s, NSA_WIDTH), BF16),
                   jax.ShapeDtypeStruct((b, g, s, n_pad), BF16)),
        grid=(b, g, s // tq),
        in_specs=[
            pl.BlockSpec((1, tq, qw), lambda bi, gi, i: (bi, i, ROPE_Q // qw + gi)),
            pl.BlockSpec((1, 1, ncp, HEAD_DIM), lambda bi, gi, i: (bi, gi, 0, 0)),
            pl.BlockSpec((1, 1, ncp, HEAD_DIM), lambda bi, gi, i: (bi, gi, 0, 0)),
            pl.BlockSpec(overlap.shape, lambda bi, gi, i: (0, 0)),
        ],
        out_specs=(pl.BlockSpec((1, tq, qw), lambda bi, gi, i: (bi, i, gi)),
                   pl.BlockSpec((1, 1, tq, n_pad), lambda bi, gi, i: (bi, gi, i, 0))),
        compiler_params=_cparams(("parallel", "parallel", "parallel")),
        name="nsa_cmp_select",
    )(proj_rope, kcmp, vcmp, overlap)


def _nsa_flash_kernel(*refs, mode, tq, tk, nkb):
    if mode == "sel":
        q_ref, k_ref, v_ref, sel_ref, hot_ref, o_ref, m_ref, l_ref, acc_ref = refs
    else:
        q_ref, k_ref, v_ref, o_ref, m_ref, l_ref, acc_ref = refs
    i = pl.program_id(2)
    kk = pl.program_id(3)
    kb = kk if mode == "sel" else i - (nkb - 1) + kk

    @pl.when(kk == 0)
    def _():
        m_ref[...] = jnp.full_like(m_ref, NEG)
        l_ref[...] = jnp.zeros_like(l_ref)
        acc_ref[...] = jnp.zeros_like(acc_ref)

    def step(diag):
        k = k_ref[0]
        valid = None
        if mode == "win":
            rows = i * tq + lax.broadcasted_iota(jnp.int32, (tq, tk), 0)
            cols = kb * tk + lax.broadcasted_iota(jnp.int32, (tq, tk), 1)
            d = rows - cols
            valid = (d >= 0) & (d < NSA_WINDOW)
        else:
            penalty = ((sel_ref[0, 0].astype(F32) - 1.0) * (-NEG)).astype(BF16)
            k = jnp.concatenate([k, hot_ref[...]], axis=1)
            if diag:
                valid = (lax.broadcasted_iota(jnp.int32, (tq, tk), 0) >= lax.broadcasted_iota(jnp.int32, (tq, tk), 1))
        v_ones = jnp.concatenate([v_ref[0], jnp.ones((tk, LANES), BF16)], axis=1)
        for h in range(NSA_HPG):
            q = q_ref[0, :, h * HEAD_DIM:(h + 1) * HEAD_DIM]
            if mode == "sel":
                q = jnp.concatenate([q, penalty], axis=1)
            s = _dot_nt(q, k)
            if valid is not None:
                s = jnp.where(valid, s, NEG)
            m_old = m_ref[h]
            m_new = jnp.maximum(m_old, jnp.max(s, axis=1, keepdims=True))
            alpha = jnp.exp2(m_old - m_new)
            p = jnp.exp2(s - jnp.concatenate([m_new] * (tk // LANES), axis=1))
            pv = jnp.dot(p.astype(BF16), v_ones, preferred_element_type=F32)
            l_ref[h] = alpha * l_ref[h] + pv[:, HEAD_DIM:]
            acc_ref[h] = alpha * acc_ref[h] + pv[:, 0:HEAD_DIM]
            m_ref[h] = m_new

    if mode == "sel":
        pl.when(kb < i)(functools.partial(step, False))
        pl.when(kb == i)(functools.partial(step, True))
    else:
        pl.when(kb >= 0)(functools.partial(step, True))

    @pl.when(kk == pl.num_programs(3) - 1)
    def _():
        for h in range(NSA_HPG):
            o_ref[0, :, h * HEAD_DIM:(h + 1) * HEAD_DIM] = (acc_ref[h] / l_ref[h]).astype(o_ref.dtype)


def _nsa_flash(proj_rope, proj_plain, sel, mode, tq, tk):
    b, s, _ = proj_rope.shape
    g = NSA_GROUPS
    qw = NSA_HPG * HEAD_DIM
    nq = s // tq
    if mode == "sel":
        assert tq == tk
        nkb = nq
        kidx = lambda i, kk: jnp.minimum(kk, i)
        kcol, vcol = ROPE_KS // HEAD_DIM, PL_VS // HEAD_DIM
    else:
        assert NSA_WINDOW % tk == 0 and tq == tk
        nkb = min(NSA_WINDOW // tk + 1, nq)
        kidx = lambda i, kk: jnp.maximum(i - (nkb - 1) + kk, 0)
        kcol, vcol = ROPE_KW // HEAD_DIM, PL_VW // HEAD_DIM
    in_specs = [
        pl.BlockSpec((1, tq, qw), lambda bi, gi, i, kk: (bi, i, ROPE_Q // qw + gi)),
        pl.BlockSpec((1, tk, HEAD_DIM), lambda bi, gi, i, kk: (bi, kidx(i, kk), kcol + gi)),
        pl.BlockSpec((1, tk, HEAD_DIM), lambda bi, gi, i, kk: (bi, kidx(i, kk), vcol + gi)),
    ]
    args = [proj_rope, proj_rope, proj_plain]
    if mode == "sel":
        n_pad = sel.shape[3]
        in_specs.append(pl.BlockSpec((1, 1, tq, n_pad), lambda bi, gi, i, kk: (bi, gi, i, 0)))
        in_specs.append(pl.BlockSpec((tk, n_pad), lambda bi, gi, i, kk: (kidx(i, kk), 0)))
        block_of_key = jnp.arange(s, dtype=jnp.int32)[:, None] // NSA_SEL_BLOCK
        onehot = (block_of_key == jnp.arange(n_pad, dtype=jnp.int32)[None, :]).astype(BF16)
        args += [sel, onehot]
    return pl.pallas_call(
        functools.partial(_nsa_flash_kernel, mode=mode, tq=tq, tk=tk, nkb=nkb),
        out_shape=jax.ShapeDtypeStruct((b, s, NSA_WIDTH), BF16),
        grid=(b, g, nq, nkb),
        in_specs=in_specs,
        out_specs=pl.BlockSpec((1, tq, qw), lambda bi, gi, i, kk: (bi, i, gi)),
        scratch_shapes=[
            pltpu.VMEM((NSA_HPG, tq, LANES), F32),
            pltpu.VMEM((NSA_HPG, tq, LANES), F32),
            pltpu.VMEM((NSA_HPG, tq, HEAD_DIM), F32),
        ],
        compiler_params=_cparams(("parallel", "parallel", "parallel", "arbitrary")),
        name="nsa_flash_" + mode,
    )(*args)


def _diff_flash_kernel(q_ref, k_ref, v_ref, lam_ref, sub_ref, o_ref, m_ref, l_ref, acc_ref, *, tq, tk, lam_init):
    i = pl.program_id(2)
    kb = pl.program_id(3)

    @pl.when(kb == 0)
    def _():
        m_ref[...] = jnp.full_like(m_ref, NEG)
        l_ref[...] = jnp.zeros_like(l_ref)
        acc_ref[...] = jnp.zeros_like(acc_ref)

    def step(masked):
        v = v_ref[0]
        if masked:
            valid = (lax.broadcasted_iota(jnp.int32, (tq, tk), 0) >= lax.broadcasted_iota(jnp.int32, (tq, tk), 1))
        for mp in range(2):
            q = q_ref[0, :, mp * HEAD_DIM:(mp + 1) * HEAD_DIM]
            k = k_ref[0, :, mp * HEAD_DIM:(mp + 1) * HEAD_DIM]
            s = _dot_nt(q, k)
            if masked:
                s = jnp.where(valid, s, NEG)
            m_old = m_ref[mp]
            m_new = jnp.maximum(m_old, jnp.max(s, axis=1, keepdims=True))
            alpha = jnp.exp2(m_old - m_new)
            p = jnp.exp2(s - jnp.concatenate([m_new] * (tk // LANES), axis=1))
            l_ref[mp] = alpha * l_ref[mp] + jnp.sum(p, axis=1, keepdims=True)
            acc_ref[mp] = jnp.concatenate([alpha] * (DIFF_VDIM // LANES), axis=1) * acc_ref[mp] + _bdot(p, v)
            m_ref[mp] = m_new

    pl.when(kb < i)(functools.partial(step, False))
    pl.when(kb == i)(functools.partial(step, True))

    @pl.when(kb == pl.num_programs(3) - 1)
    def _():
        lam = lam_ref[...]
        lam_full = (jnp.exp(jnp.sum(lam[0:1, :] * lam[1:2, :], axis=1, keepdims=True))
                    - jnp.exp(jnp.sum(lam[2:3, :] * lam[3:4, :], axis=1, keepdims=True)) + lam_init)
        wide = lambda x: jnp.concatenate([x] * (DIFF_VDIM // LANES), axis=1)
        o = acc_ref[0] / wide(l_ref[0]) - lam_full * (acc_ref[1] / wide(l_ref[1]))
        ms = jnp.mean(o * o, axis=-1, keepdims=True)
        o = o * lax.rsqrt(ms + 1e-5) * sub_ref[...]
        o_ref[0] = (o * (1.0 - lam_init)).astype(o_ref.dtype)


def _diff_flash(proj_rope, proj_plain, lam, subln, layer, tq, tk):
    b, s, _ = proj_rope.shape
    assert tq == tk
    nq = s // tq
    lam_init = 0.8 - 0.6 * math.exp(-0.3 * layer)
    w = DIFF_VDIM
    return pl.pallas_call(
        functools.partial(_diff_flash_kernel, tq=tq, tk=tk, lam_init=lam_init),
        out_shape=jax.ShapeDtypeStruct((b, s, DIFF_WIDTH), BF16),
        grid=(b, DIFF_HEADS, nq, nq),
        in_specs=[
            pl.BlockSpec((1, tq, w), lambda bi, h, i, kb: (bi, i, ROPE_DQ // w + h)),
            pl.BlockSpec((1, tk, w), lambda bi, h, i, kb: (bi, jnp.minimum(kb, i), ROPE_DK // w + h)),
            pl.BlockSpec((1, tk, w), lambda bi, h, i, kb: (bi, jnp.minimum(kb, i), PL_DV // w + h)),
            pl.BlockSpec(lam.shape, lambda bi, h, i, kb: (0, 0)),
            pl.BlockSpec((1, w), lambda bi, h, i, kb: (0, 0)),
        ],
        out_specs=pl.BlockSpec((1, tq, w), lambda bi, h, i, kb: (bi, i, h)),
        scratch_shapes=[
            pltpu.VMEM((2, tq, LANES), F32),
            pltpu.VMEM((2, tq, LANES), F32),
            pltpu.VMEM((2, tq, w), F32),
        ],
        compiler_params=_cparams(("parallel", "parallel", "parallel", "arbitrary")),
        name="diff_flash",
    )(proj_rope, proj_rope, proj_plain, lam, subln[None, :])


def _bmm(a, b):
    return jnp.einsum("umk,ukn->umn", a.astype(BF16), b.astype(BF16), preferred_element_type=F32)


def _bmm_nt(a, b):
    return jnp.einsum("umk,unk->umn", a.astype(BF16), b.astype(BF16), preferred_element_type=F32)


def _rwkv_scan_kernel(r_ref, k_ref, v_ref, gd_ref, wa_ref, mur_ref, muk_ref, muv_ref, mugd_ref, muwa_ref,
                      wdec_ref, wrate_ref, gup_ref, w0_ref, a0_ref, kk_ref, ka_ref, rk_ref, lnw_ref, lnb_ref,
                      o_ref, st_ref, last_ref, lastgd_ref, *, ts, chunk):
    @pl.when(pl.program_id(2) == 0)
    def _():
        st_ref[...] = jnp.zeros_like(st_ref)
        last_ref[...] = jnp.zeros_like(last_ref)
        lastgd_ref[...] = jnp.zeros_like(lastgd_ref)

    def shifted(x_ref, mu_ref, carry):
        x = x_ref[0]
        row = lax.broadcasted_iota(jnp.int32, x.shape, 0)
        prev = jnp.where(row == 0, carry, pltpu.roll(x, 1, 0))
        return x + (prev - x) * mu_ref[...]

    r = shifted(r_ref, mur_ref, last_ref[0, 0:1, :])
    k = shifted(k_ref, muk_ref, last_ref[1, 0:1, :])
    v = shifted(v_ref, muv_ref, last_ref[2, 0:1, :])
    wa = shifted(wa_ref, muwa_ref, last_ref[3, 0:1, :])
    gd = shifted(gd_ref, mugd_ref, lastgd_ref[0:1, :])
    for n, ref in enumerate((r_ref, k_ref, v_ref, wa_ref)):
        last_ref[n, 0:1, :] = ref[0, ts - 1:ts, :]
    lastgd_ref[0:1, :] = gd_ref[0, ts - 1:ts, :]

    wa = jnp.where(lax.broadcasted_iota(jnp.int32, wa.shape, 1) < RWKV_W_LORA, jnp.tanh(wa), wa)
    lw = -math.exp(-0.5) * jax.nn.sigmoid(w0_ref[...] + _bdot(wa, wdec_ref[...]))
    a = jax.nn.sigmoid(a0_ref[...] + _bdot(wa, wrate_ref[...]))
    g = _bdot(jax.nn.sigmoid(gd), gup_ref[...])

    c = chunk
    nc = ts // c
    head0 =lax.broadcasted_iota(jnp.int32, (ts, LANES), 1) < RWKV_HEAD

    def seg_sum(x):
        s0 = jnp.sum(jnp.where(head0, x, 0.0), axis=1, keepdims=True)
        s1 = jnp.sum(jnp.where(head0, 0.0, x), axis=1, keepdims=True)
        return jnp.where(head0, s0, s1)

    kk = k * kk_ref[...]
    kap = kk / jnp.maximum(jnp.sqrt(seg_sum(kk * kk)), 1e-12)
    kmod = k * (1.0 + (a - 1.0) * ka_ref[...])
    bvec = kap * a

    ri = lax.broadcasted_iota(jnp.int32, (ts, ts), 0)
    ci = lax.broadcasted_iota(jnp.int32, (ts, ts), 1)
    same_chunk = jnp.right_shift(ri, c.bit_length() - 1) == jnp.right_shift(ci, c.bit_length() - 1)
    tri = jnp.where(same_chunk & (ri >= ci), 1.0, 0.0).astype(BF16)
    cum = _dot_exact_lhs(tri, lw)
    g_incl = jnp.exp(cum)
    g_inv = jnp.exp(-cum)
    split = lambda x: x.reshape(nc, c, LANES)
    g_last = [g_incl[(ch + 1) * c - 1:(ch + 1) * c, :] for ch in range(nc)]
    g_last_rows = jnp.concatenate([jnp.broadcast_to(gl, (c, LANES)) for gl in g_last], axis=0)
    kh = split(kmod * g_inv)
    bh = split(bvec * g_inv)
    kg = split(kmod * g_inv * g_last_rows)
    bg = split(bvec * g_inv * g_last_rows)
    kaph = kap * jnp.exp(cum - lw)
    rh = r * g_incl
    v3 = split(v)

    both = lambda x: jnp.concatenate([split(jnp.where(head0, x, 0.0)), split(jnp.where(head0, 0.0, x))], axis=0)
    twice = lambda x: jnp.concatenate([x, x], axis=0)
    kap_m = both(kaph)
    r_m = both(rh)
    gram = _bmm_nt(jnp.concatenate([kap_m, r_m], axis=1), twice(jnp.concatenate([bh, kh], axis=1)))
    row = lax.broadcasted_iota(jnp.int32, (c, 2 * c), 0)
    col = lax.broadcasted_iota(jnp.int32, (c, 2 * c), 1) & (c - 1)
    lbk = jnp.where(row > col, gram[:, 0:c, :], 0.0)
    abk = jnp.where(row >= col, gram[:, c:2 * c, :], 0.0)
    lb = lbk[:, :, 0:c]
    tm = -lb
    q = _bmm(lb, lb)
    n = 2
    while True:
        tm = tm + q + _bmm(tm, q)
        n *= 2
        if n >= c:
            break
        q = _bmm(q, q)
    v2 = twice(v3)
    kap_t = kap_m + _bmm(tm, kap_m)
    lkv = _bmm(lbk, jnp.concatenate([jnp.zeros_like(v2), v2], axis=1))
    z0 = -(lkv + _bmm(tm, lkv))
    r_t = r_m - _bmm(abk, jnp.concatenate([kap_t, jnp.zeros_like(kap_t)], axis=1))
    y0 = _bmm(abk, jnp.concatenate([z0, v2], axis=1))
    h0 = lax.broadcasted_iota(jnp.int32, (nc, c, LANES), 2) < RWKV_HEAD
    merge = lambda x: jnp.where(h0, x[0:nc], x[nc:2 * nc])
    kap_t = kap_t[0:nc] + kap_t[nc:2 * nc]
    r_t = r_t[0:nc] + r_t[nc:2 * nc]
    z0 = merge(z0)
    y0 = merge(y0)

    br = lax.broadcasted_iota(jnp.int32, (LANES, LANES), 0)
    bc = lax.broadcasted_iota(jnp.int32, (LANES, LANES), 1)
    blockdiag = (br < RWKV_HEAD) == (bc < RWKV_HEAD)
    state = st_ref[...]
    ys = []
    for ch in range(nc):
        trans = jnp.where(blockdiag, -_dot_tn(bg[ch], kap_t[ch]), 0.0)
        trans = trans + jnp.where(br == bc, jnp.broadcast_to(g_last[ch], (LANES, LANES)), 0.0)
        inject = jnp.where(blockdiag, _dot_tn(jnp.concatenate([bg[ch], kg[ch]], axis=0),
                                              jnp.concatenate([z0[ch], v3[ch]], axis=0)), 0.0)
        ys.append(_dot3(r_t[ch], state) + y0[ch])
        state = _dot3(trans, state) + inject
    st_ref[...] = state
    y = jnp.concatenate(ys, axis=0)

    mean = seg_sum(y) * (1.0 / RWKV_HEAD)
    yc = y - mean
    var = seg_sum(yc * yc) * (1.0 / RWKV_HEAD)
    yn = yc * lax.rsqrt(var + RWKV_LNX_EPS) * lnw_ref[...] + lnb_ref[...]
    bonus = seg_sum(r * kmod * rk_ref[...]) * v
    o_ref[0] = ((yn + bonus) * g).astype(o_ref.dtype)


def _rwkv_scan(proj3, mu_p, wwa, gup, w0, a0, k_k, k_a, r_k, lnx_w, lnx_b, ts=512, chunk=64):
    b, s, _ = proj3.shape
    w = RWKV_WIDTH
    ts = min(ts, s)
    assert s % ts == 0 and ts % chunk == 0
    npair = w // LANES
    col = lambda off: pl.BlockSpec((1, ts, LANES), lambda bi, j, t: (bi, t, off // LANES + j))
    mucol = lambda off: pl.BlockSpec((1, LANES), lambda bi, j, t: (0, off // LANES + j))
    vec = pl.BlockSpec((1, LANES), lambda bi, j, t: (0, j))
    in_specs = [
        col(0), col(w), col(2 * w),
        pl.BlockSpec((1, ts, RW_G_PAD), lambda bi, j, t: (bi, t, RW_GD // RW_G_PAD)),
        pl.BlockSpec((1, ts, RW_LORA_PAD), lambda bi, j, t: (bi, t, RW_WA // RW_LORA_PAD)),
        mucol(0), mucol(w), mucol(2 * w),
        pl.BlockSpec((1, RW_G_PAD), lambda bi, j, t: (0, RW_GD // RW_G_PAD)),
        pl.BlockSpec((1, RW_LORA_PAD), lambda bi, j, t: (0, RW_WA // RW_LORA_PAD)),
        pl.BlockSpec((RW_LORA_PAD, LANES), lambda bi, j, t: (0, j)),
        pl.BlockSpec((RW_LORA_PAD, LANES), lambda bi, j, t: (0, npair + j)),
        pl.BlockSpec((RW_G_PAD, LANES), lambda bi, j, t: (0, j)),
    ] + [vec] * 7
    row = lambda z: z[None, :]
    return pl.pallas_call(
        functools.partial(_rwkv_scan_kernel, ts=ts, chunk=chunk),
        out_shape=jax.ShapeDtypeStruct((b, s, w), BF16),
        grid=(b, npair, s // ts),
        in_specs=in_specs,
        out_specs=pl.BlockSpec((1, ts, LANES), lambda bi, j, t: (bi, t, j)),
        scratch_shapes=[
            pltpu.VMEM((LANES, LANES), F32),
            pltpu.VMEM((4, SUBLANES, LANES), F32),
            pltpu.VMEM((SUBLANES, RW_G_PAD), F32),
        ],
        compiler_params=_cparams(("parallel", "parallel", "arbitrary")),
        name="rwkv_scan",
    )(proj3, proj3, proj3, proj3, proj3, mu_p, mu_p, mu_p, mu_p, mu_p, wwa, wwa, gup,
      row(w0), row(a0), row(k_k), row(k_a), row(r_k), row(lnx_w), row(lnx_b))


def _merge_kernel(oc_ref, os_ref, ow_ref, ng_ref, yd_ref, yr_ref, wb_ref, g0_ref, g1_ref, g2_ref, bias_ref,
                  o_ref, yn_ref):
    @pl.when(pl.program_id(1) == 0)
    def _():
        gates = jax.nn.sigmoid(ng_ref[...])
        for hd in range(NSA_HEADS):
            sl = slice(hd * HEAD_DIM, (hd + 1) * HEAD_DIM)
            o = (gates[:, 3 * hd:3 * hd + 1] * oc_ref[:, sl]
                 + gates[:, 3 * hd + 1:3 * hd + 2] * os_ref[:, sl]
                 + gates[:, 3 * hd + 2:3 * hd + 3] * ow_ref[:, sl])
            yn_ref[:, sl] = o.astype(BF16)

    branches = (yn_ref, yd_ref, yr_ref)
    graw = (g0_ref, g1_ref, g2_ref)
    acc = None
    for bi in range(N_BRANCH):
        gate = jax.nn.sigmoid(graw[bi][...] + bias_ref[bi])
        term = gate * jnp.dot(branches[bi][...], wb_ref[bi], preferred_element_type=F32)
        acc = term if acc is None else acc + term
    o_ref[...] = acc.astype(o_ref.dtype)


def _merge(o_cmp, o_slc, o_win, proj_f32, y_diff, y_rwkv, w_branch, b_gate, tm=1024, tn=512):
    m = o_cmp.shape[0]
    n = w_branch.shape[2]
    tm = min(tm, m)
    assert m % tm == 0 and n % tn == 0 and F_BG % tn == 0 and n % tn == 0
    bw = BRANCH_WIDTH
    rowblk = lambda: pl.BlockSpec((tm, bw), lambda i, j: (i, 0))
    gate_spec = lambda bi: pl.BlockSpec((tm, tn), lambda i, j: (i, F_BG // tn + bi * (n // tn) + j))
    return pl.pallas_call(
        _merge_kernel,
        out_shape=jax.ShapeDtypeStruct((m, n), BF16),
        grid=(m // tm, n // tn),
        in_specs=[
            rowblk(), rowblk(), rowblk(),
            pl.BlockSpec((tm, LANES), lambda i, j: (i, F_NG // LANES)),
            rowblk(), rowblk(),
            pl.BlockSpec((N_BRANCH, bw, tn), lambda i, j: (0, 0, j)),
            gate_spec(0), gate_spec(1), gate_spec(2),
            pl.BlockSpec((N_BRANCH, 1, tn), lambda i, j: (0, 0, j)),
        ],
        out_specs=pl.BlockSpec((tm, tn), lambda i, j: (i, j)),
        scratch_shapes=[pltpu.VMEM((tm, bw), BF16)],
        compiler_params=_cparams(("parallel", "arbitrary")),
        name="gated_merge",
    )(o_cmp, o_slc, o_win, proj_f32, y_diff, y_rwkv, w_branch, proj_f32, proj_f32, proj_f32, b_gate[:, None, :])


def _pack_layer_weights(w_in, w_gate, rwkv_mu, w_up_lora, a_up_lora, g_up_lora):
    d = w_in.shape[0]
    nsa_sizes = (NSA_WIDTH,) + (NSA_KV,) * 6 + (3 * NSA_HEADS,)
    diff_sizes = (DIFF_QK, DIFF_QK, DIFF_WIDTH)
    rw_sizes = (RWKV_WIDTH,) * 3 + (RWKV_W_LORA, RWKV_A_LORA, RWKV_G_LORA)
    offs = np.cumsum((0,) + nsa_sizes + diff_sizes + rw_sizes)
    seg = [w_in[:, offs[i]:offs[i + 1]] for i in range(len(offs) - 1)]
    q, kc, vc, ks, vs, kw, vw, ng, dq, dk, dv, rr, rk, rv, wd, ad, gd = seg
    w_rope = jnp.concatenate([q, dq, ks, kw, dk], axis=1).astype(BF16)
    w_plain = jnp.concatenate([kc, vc, vs, vw, dv], axis=1).astype(BF16)
    zpad = lambda n: jnp.zeros((d, n), w_in.dtype)
    w_f32 = jnp.concatenate(
        [rr, rk, rv, gd, zpad(RW_G_PAD - RWKV_G_LORA), wd, ad, ng, zpad(LANES - 3 * NSA_HEADS)]
        + [w_gate[bi] for bi in range(N_BRANCH)], axis=1).astype(BF16)
    mu_rkv, mu_wa, mu_gd = jnp.split(rwkv_mu, [3 * RWKV_WIDTH, 3 * RWKV_WIDTH + RW_LORA_PAD])
    mu_p = jnp.concatenate([mu_rkv, mu_gd, jnp.zeros((RW_G_PAD - RWKV_G_LORA,), rwkv_mu.dtype), mu_wa])[None, :]
    zz = jnp.zeros((RWKV_W_LORA, RWKV_WIDTH), w_up_lora.dtype)
    wwa = jnp.concatenate([jnp.concatenate([w_up_lora, zz], axis=1),
                           jnp.concatenate([zz, a_up_lora], axis=1)], axis=0).astype(BF16)
    gup = jnp.concatenate([g_up_lora, jnp.zeros((RW_G_PAD - RWKV_G_LORA, RWKV_WIDTH), g_up_lora.dtype)],
                          axis=0).astype(BF16)
    return w_rope, w_plain, w_f32, mu_p, wwa, gup


def _overlap_matrix(n_cp, n_sel):
    c_start = np.arange(n_cp) * NSA_CMP_STRIDE
    c_end = c_start + NSA_CMP_LEN - 1
    j_start = np.arange(n_sel) * NSA_SEL_BLOCK
    ov = (c_start[:, None] <= j_start[None, :] + NSA_SEL_BLOCK - 1) & (c_end[:, None] >= j_start[None, :])
    ov = np.pad(ov.astype(np.float32), ((0, 0), (0, -n_sel % LANES)))
    return jnp.asarray(ov).astype(BF16)


def _chunk_rows(z, groups):
    b, s, _ = z.shape
    z = z.reshape(b, s // NSA_CMP_STRIDE, NSA_CMP_STRIDE, groups, HEAD_DIM)
    return z.transpose(0, 3, 1, 2, 4).reshape(b, groups, s // NSA_CMP_STRIDE, NSA_CMP_STRIDE * HEAD_DIM)


def _nsa_branches(proj_rope, proj_plain, positions, cmp_pos, cmp_w1, cmp_w2, tq):
    b, s, _ = proj_rope.shape
    n_chunk = s // NSA_CMP_STRIDE
    n_sel = s // NSA_SEL_BLOCK
    kc2 = _chunk_rows(proj_plain[:, :, PL_KC:PL_KC + NSA_KV], NSA_GROUPS)
    vc2 = _chunk_rows(proj_plain[:, :, PL_VC:PL_VC + NSA_KV], NSA_GROUPS)
    cmp_end = np.minimum(np.arange(n_chunk) * NSA_CMP_STRIDE + NSA_CMP_LEN - 1, s - 1)
    pos_c = jnp.take(positions, jnp.asarray(cmp_end), axis=1)
    cosc, sinc = _rope_tables(pos_c.reshape(-1))
    cosc = cosc.reshape(b, n_chunk, LANES)
    sinc = sinc.reshape(b, n_chunk, LANES)
    w1 = cmp_w1.astype(BF16)
    w2 = cmp_w2.astype(BF16)
    pe = cmp_pos.reshape(2, 1, NSA_CMP_LEN * HEAD_DIM)
    kcmp = _nsa_compress(kc2, w1[0], w2[0], pe[0], cosc, sinc, True)
    vcmp = _nsa_compress(vc2, w1[1], w2[1], pe[1], cosc, sinc, False)
    o_cmp, sel = _nsa_cmp(proj_rope, kcmp, vcmp, _overlap_matrix(n_chunk, n_sel), n_sel, tq)
    o_slc = _nsa_flash(proj_rope, proj_plain, sel, "sel", tq, tq)
    o_win = _nsa_flash(proj_rope, proj_plain, None, "win", tq, tq)
    return o_cmp, o_slc, o_win


def _layer(x2, b, s, layer, cosf, sinf, positions, p):
    m = b * s
    w_rope, w_plain, w_f32, mu_p, wwa, gup = _pack_layer_weights(
        p["w_in"], p["w_gate"], p["rwkv_mu"], p["rwkv_w_up"], p["rwkv_a_up"], p["rwkv_g_up"])
    scale = HEAD_DIM ** -0.5 * math.log2(math.e)
    colscale =jnp.concatenate([jnp.full((ROPE_KS,), scale, F32), jnp.ones((ROPE_COLS - ROPE_KS,), F32)])[None, :]
    g_pre = p["norm_pre_mix"]
    proj_rope = _norm_matmul(x2, g_pre, w_rope, "rope", BF16, rope=(cosf, sinf, colscale))
    proj_plain = _norm_matmul(x2, g_pre, w_plain, "plain", BF16)
    proj_f32 = _norm_matmul(x2, g_pre, w_f32, "plain", F32)
    pr3 = proj_rope.reshape(b, s, ROPE_COLS)
    pp3 = proj_plain.reshape(b, s, PLAIN_COLS)

    tq = min(512, s)
    o_cmp, o_slc, o_win = _nsa_branches(pr3, pp3, positions, p["nsa_cmp_pos"], p["nsa_cmp_w1"], p["nsa_cmp_w2"], tq)
    y_diff = _diff_flash(pr3, pp3, p["diff_lambda"], p["diff_subln"], layer, tq, tq)

    y_rwkv = _rwkv_scan(proj_f32.reshape(b, s, F32_COLS), mu_p, wwa, gup, p["rwkv_w0"], p["rwkv_a0"],
                        p["rwkv_k_k"], p["rwkv_k_a"], p["rwkv_r_k"].reshape(-1), p["rwkv_lnx_w"], p["rwkv_lnx_b"])

    merged = _merge(o_cmp.reshape(m, NSA_WIDTH), o_slc.reshape(m, NSA_WIDTH), o_win.reshape(m, NSA_WIDTH),
                    proj_f32, y_diff.reshape(m, DIFF_WIDTH), y_rwkv.reshape(m, RWKV_WIDTH),
                    p["w_branch"].astype(BF16), p["b_gate"])
    x2 = _matmul_norm_res(merged, p["w_out"].astype(BF16), p["norm_post_mix"], x2)
    hidden = _norm_matmul(x2, p["norm_pre_mlp"], p["w_up"].astype(BF16), "relu2", BF16)
    x2 = _matmul_norm_res(hidden, p["w_down"].astype(BF16), p["norm_post_mlp"], x2)
    return x2


def kernel(x, positions, norm_pre_mix, norm_post_mix, norm_pre_mlp, norm_post_mlp, w_in, nsa_cmp_pos, nsa_cmp_w1, nsa_cmp_w2, diff_lambda, diff_subln, rwkv_mu, rwkv_w0, rwkv_w_up, rwkv_a0, rwkv_a_up, rwkv_g_up, rwkv_k_k, rwkv_k_a, rwkv_r_k, rwkv_lnx_w, rwkv_lnx_b, w_gate, b_gate, w_branch, w_out, w_up, w_down):
    b, s, d = x.shape
    depth = w_in.shape[0]
    stacked = dict(
        norm_pre_mix=norm_pre_mix, norm_post_mix=norm_post_mix, norm_pre_mlp=norm_pre_mlp,
        norm_post_mlp=norm_post_mlp, w_in=w_in, nsa_cmp_pos=nsa_cmp_pos, nsa_cmp_w1=nsa_cmp_w1,
        nsa_cmp_w2=nsa_cmp_w2, diff_lambda=diff_lambda, diff_subln=diff_subln, rwkv_mu=rwkv_mu,
        rwkv_w0=rwkv_w0, rwkv_w_up=rwkv_w_up, rwkv_a0=rwkv_a0, rwkv_a_up=rwkv_a_up, rwkv_g_up=rwkv_g_up,
        rwkv_k_k=rwkv_k_k, rwkv_k_a=rwkv_k_a, rwkv_r_k=rwkv_r_k, rwkv_lnx_w=rwkv_lnx_w,
        rwkv_lnx_b=rwkv_lnx_b, w_gate=w_gate, b_gate=b_gate, w_branch=w_branch, w_out=w_out,
        w_up=w_up, w_down=w_down)
    cosf, sinf = _rope_tables(positions.reshape(-1))
    x2 = x.reshape(b * s, d)
    for layer in range(depth):
        x2 = _layer(x2, b, s, layer, cosf, sinf, positions, {n: a[layer] for n, a in stacked.items()})
    return x2.reshape(b, s, d)
```

```python
import functools
import math

import jax
import jax.numpy as jnp
import numpy as np
from jax import lax
from jax.experimental import pallas as pl
from jax.experimental.pallas import tpu as pltpu

F32 = jnp.float32
BF16 = jnp.bfloat16

D_MODEL = 2048
RMS_EPS = 1e-6
ROPE_THETA = 500000.0
HEAD_DIM = 128
ROT_HALF = HEAD_DIM // 8
NSA_HEADS = 8
NSA_GROUPS = 2
NSA_HPG = NSA_HEADS // NSA_GROUPS
NSA_CMP_LEN = 32
NSA_CMP_STRIDE = 16
NSA_CMP_HIDDEN = 256
NSA_SEL_BLOCK = 64
SEL_SHIFT = NSA_SEL_BLOCK.bit_length() - 1
NSA_SEL_TOPK = 16
NSA_WINDOW = 512
NSA_WIDTH = NSA_HEADS * HEAD_DIM
NSA_KV = NSA_GROUPS * HEAD_DIM
DIFF_HEADS = 4
DIFF_VDIM = 2 * HEAD_DIM
DIFF_WIDTH = DIFF_HEADS * DIFF_VDIM
DIFF_QK = 2 * DIFF_HEADS * HEAD_DIM
RWKV_HEAD = 64
RWKV_WIDTH = 1024
RWKV_W_LORA = 64
RWKV_A_LORA = 64
RWKV_G_LORA = 160
RWKV_LNX_EPS = 64e-5
N_BRANCH = 3
BRANCH_WIDTH = 1024

LANES = 128
SUBLANES = 8
VMEM_LIMIT_BYTES = 56 * 1024 * 1024

NEG = -1e30

ROPE_Q, ROPE_DQ, ROPE_KS, ROPE_KW, ROPE_DK = 0, 1024, 2048, 2304, 2560
ROPE_COLS = 3584
PL_KC, PL_VC, PL_VS, PL_VW, PL_DV, PL_BG = 0, 256, 512, 768, 1024, 2048
PLAIN_COLS = PL_BG + N_BRANCH * D_MODEL
RW_LORA_PAD = 128
RW_G_PAD = 256
RW_GD = 3 * RWKV_WIDTH
RW_WA = RW_GD + RW_G_PAD
RW_COLS = RW_WA + RW_LORA_PAD
F_NG = RW_COLS
F32_COLS = RW_COLS + LANES


def _cparams(sem):
    return pltpu.CompilerParams(dimension_semantics=sem, vmem_limit_bytes=VMEM_LIMIT_BYTES)


def _bdot(a, b):
    return jnp.dot(a.astype(BF16), b.astype(BF16), preferred_element_type=F32)


def _dot_nt(a, b):
    return lax.dot_general(a.astype(BF16), b.astype(BF16), (((1,), (1,)), ((), ())),
                           preferred_element_type=F32)


def _dot_tn(a, b):
    return lax.dot_general(a.astype(BF16), b.astype(BF16), (((0,), (0,)), ((), ())),
                           preferred_element_type=F32)


def _split2(x):
    hi = x.astype(BF16)
    lo = (x - hi.astype(F32)).astype(BF16)
    return hi, lo


def _split3(x):
    hi = x.astype(BF16)
    r1 = x - hi.astype(F32)
    mid = r1.astype(BF16)
    lo = (r1 - mid.astype(F32)).astype(BF16)
    return hi, mid, lo


def _dot_exact_lhs(a_bf, x):
    hi, mid, lo = _split3(x)
    d = lambda p: jnp.dot(a_bf, p, preferred_element_type=F32)
    return d(hi) + (d(mid) + d(lo))


def _dot_exact_rhs(x, b_bf):
    hi, mid, lo = _split3(x)
    d = lambda p: jnp.dot(p, b_bf, preferred_element_type=F32)
    return d(hi) + (d(mid) + d(lo))


def _dot3(a, b):
    ah, al = _split2(a)
    bh, bl = _split2(b)
    d = lambda p, q: jnp.dot(p, q, preferred_element_type=F32)
    return d(ah, bh) + (d(ah, bl) + d(al, bh))


def _rope_partner(z, lane):
    return jnp.where(lane < ROT_HALF, pltpu.roll(z, LANES - ROT_HALF, 1), pltpu.roll(z, ROT_HALF, 1))


def _rope_table_kernel(pos_ref, invf_ref, sign_ref, cos_ref, sin_ref):
    ang = pos_ref[...].astype(F32) * invf_ref[...]
    cos_ref[...] = jnp.cos(ang)
    sin_ref[...] = jnp.sin(ang) * sign_ref[...]


def _rope_tables(pos_flat):
    n = pos_flat.shape[0]
    half = ROT_HALF
    inv_freq = ROPE_THETA ** (-jnp.arange(half, dtype=F32) / half)
    zeros = jnp.zeros((LANES - 2 * half,), F32)
    invf = jnp.concatenate([inv_freq, inv_freq, zeros])[None, :]
    sign = jnp.concatenate([-jnp.ones((half,), F32), jnp.ones((half,), F32), zeros])[None, :]
    tm = min(n, 2048)
    assert n % tm == 0
    vec = pl.BlockSpec((1, LANES), lambda i: (0, 0))
    out = pl.BlockSpec((tm, LANES), lambda i: (i, 0))
    return pl.pallas_call(
        _rope_table_kernel,
        out_shape=(jax.ShapeDtypeStruct((n, LANES), F32),) * 2,
        grid=(n // tm,),
        in_specs=[pl.BlockSpec((tm, 1), lambda i: (i, 0)), vec, vec],
        out_specs=(out, out),
        compiler_params=_cparams(("parallel",)),
        name="rope_tables",
    )(pos_flat[:, None], invf, sign)


def _norm_matmul_kernel(x_ref, g_ref, w_ref, *rest, mode, tn):
    if mode == "rope":
        cos_ref, sin_ref, cs_ref, o_ref, u_ref = rest
    else:
        o_ref, u_ref = rest

    @pl.when(pl.program_id(1) == 0)
    def _():
        x = x_ref[...]
        ms = jnp.mean(x * x, axis=-1, keepdims=True)
        u_ref[...] = (x * lax.rsqrt(ms + RMS_EPS) * g_ref[...]).astype(BF16)

    acc = jnp.dot(u_ref[...], w_ref[...], preferred_element_type=F32)
    if mode == "rope":
        acc = acc * cs_ref[...]
        cosv = cos_ref[...]
        sinv = sin_ref[...]
        lane = lax.broadcasted_iota(jnp.int32, cosv.shape, 1)
        for h in range(tn // LANES):
            z = acc[:, h * LANES:(h + 1) * LANES]
            o_ref[:, h * LANES:(h + 1) * LANES] = (z * cosv + _rope_partner(z, lane) * sinv).astype(o_ref.dtype)
    elif mode == "relu2":
        o_ref[...] = jnp.square(jnp.maximum(acc, 0.0)).astype(o_ref.dtype)
    else:
        o_ref[...] = acc.astype(o_ref.dtype)


def _norm_matmul(x, g, w, mode, out_dtype, rope=None, tm=1024, tn=512):
    m, d = x.shape
    n = w.shape[1]
    tm = min(tm, m)
    assert m % tm == 0 and n % tn == 0
    in_specs = [
        pl.BlockSpec((tm, d), lambda i, j: (i, 0)),
        pl.BlockSpec((1, d), lambda i, j: (0, 0)),
        pl.BlockSpec((d, tn), lambda i, j: (0, j)),
    ]
    args = [x, g[None, :], w]
    if mode == "rope":
        cosf, sinf, colscale = rope
        in_specs += [
            pl.BlockSpec((tm, LANES), lambda i, j: (i, 0)),
            pl.BlockSpec((tm, LANES), lambda i, j: (i, 0)),
            pl.BlockSpec((1, tn), lambda i, j: (0, j)),
        ]
        args += [cosf, sinf, colscale]
    return pl.pallas_call(
        functools.partial(_norm_matmul_kernel, mode=mode, tn=tn),
        out_shape=jax.ShapeDtypeStruct((m, n), out_dtype),
        grid=(m // tm, n // tn),
        in_specs=in_specs,
        out_specs=pl.BlockSpec((tm, tn), lambda i, j: (i, j)),
        scratch_shapes=[pltpu.VMEM((tm, d), BF16)],
        compiler_params=_cparams(("parallel", "arbitrary")),
        name="norm_matmul_" + mode,
    )(*args)


def _matmul_norm_res_kernel(a_ref, w_ref, g_ref, res_ref, o_ref, acc_ref):
    k = pl.program_id(1)

    @pl.when(k == 0)
    def _():
        acc_ref[...] = jnp.zeros_like(acc_ref)

    acc_ref[...] += jnp.dot(a_ref[...], w_ref[...], preferred_element_type=F32)

    @pl.when(k == pl.num_programs(1) - 1)
    def _():
        y = acc_ref[...]
        ms = jnp.mean(y * y, axis=-1, keepdims=True)
        o_ref[...] = res_ref[...] + y * lax.rsqrt(ms + RMS_EPS) * g_ref[...]


def _matmul_norm_res(a, w, g, res, tm=512, tk=2048):
    m, kdim = a.shape
    n = w.shape[1]
    tm = min(tm, m)
    assert m % tm == 0 and kdim % tk == 0
    return pl.pallas_call(
        _matmul_norm_res_kernel,
        out_shape=jax.ShapeDtypeStruct((m, n), F32),
        grid=(m // tm, kdim // tk),
        in_specs=[
            pl.BlockSpec((tm, tk), lambda i, k: (i, k)),
            pl.BlockSpec((tk, n), lambda i, k: (k, 0)),
            pl.BlockSpec((1, n), lambda i, k: (0, 0)),
            pl.BlockSpec((tm, n), lambda i, k: (i, 0)),
        ],
        out_specs=pl.BlockSpec((tm, n), lambda i, k: (i, 0)),
        scratch_shapes=[pltpu.VMEM((tm, n), F32)],
        compiler_params=_cparams(("parallel", "arbitrary")),
        name="matmul_norm_res",
    )(a, w, g[None, :], res)


def _gelu_tanh(x):
    return 0.5 * x * (1.0 + jnp.tanh(math.sqrt(2.0 / math.pi) * (x + 0.044715 * (x * x * x))))


def _nsa_compress_kernel(x_ref, w1_ref, w2_ref, pe_ref, cos_ref, sin_ref, o_ref, *, use_rope):
    x = x_ref[0, 0]
    half = x.shape[1]
    n_chunk = x.shape[0]
    a = jnp.dot(x, w1_ref[0:half, :], preferred_element_type=F32)
    b = jnp.dot(x, w1_ref[half:2 * half, :], preferred_element_type=F32)
    pe = jnp.broadcast_to(pe_ref[...], (SUBLANES, pe_ref.shape[1]))
    peb = _dot_exact_rhs_general(pe, w1_ref[...])[0:1, :]
    h = a + pltpu.roll(b, n_chunk - 1, 0) + peb
    y = _bdot(_gelu_tanh(h), w2_ref[...])
    if use_rope:
        lane = lax.broadcasted_iota(jnp.int32, y.shape, 1)
        y = y * cos_ref[0] + _rope_partner(y, lane) * sin_ref[0]
    o_ref[0, 0] = y.astype(o_ref.dtype)


def _dot_exact_rhs_general(x, w_bf):
    hi, mid, lo = _split3(x)
    d = lambda p: jnp.dot(p, w_bf, preferred_element_type=F32)
    return d(hi) + (d(mid) + d(lo))


def _nsa_compress(x2, w1, w2, pe_flat, cosc, sinc, use_rope):
    b, g, n_chunk, width = x2.shape
    dh = w2.shape[1]
    return pl.pallas_call(
        functools.partial(_nsa_compress_kernel, use_rope=use_rope),
        out_shape=jax.ShapeDtypeStruct((b, g, n_chunk, dh), BF16),
        grid=(b, g),
        in_specs=[
            pl.BlockSpec((1, 1, n_chunk, width), lambda i, j: (i, j, 0, 0)),
            pl.BlockSpec(w1.shape, lambda i, j: (0, 0)),
            pl.BlockSpec(w2.shape, lambda i, j: (0, 0)),
            pl.BlockSpec(pe_flat.shape, lambda i, j: (0, 0)),
            pl.BlockSpec((1, n_chunk, dh), lambda i, j: (i, 0, 0)),
            pl.BlockSpec((1, n_chunk, dh), lambda i, j: (i, 0, 0)),
        ],
        out_specs=pl.BlockSpec((1, 1, n_chunk, dh), lambda i, j: (i, j, 0, 0)),
        compiler_params=_cparams(("parallel", "parallel")),
        name="nsa_compress",
    )(x2, w1, w2, pe_flat, cosc, sinc)


def _nsa_cmp_kernel(q_ref, kc_ref, vc_ref, ov_ref, o_ref, sel_ref, *, tq, top, n_sel):
    i = pl.program_id(2)
    kc = kc_ref[0, 0]
    vc = vc_ref[0, 0]
    ncp = kc.shape[0]
    t = i * tq + lax.broadcasted_iota(jnp.int32, (tq, ncp), 0)
    c = lax.broadcasted_iota(jnp.int32, (tq, ncp), 1)
    valid = (c * NSA_CMP_STRIDE + (NSA_CMP_LEN - 1)) <= t
    psum = jnp.zeros((tq, ncp), F32)
    for h in range(NSA_HPG):
        q = q_ref[0, :, h * HEAD_DIM:(h + 1) * HEAD_DIM]
        s = jnp.where(valid, _dot_nt(q, kc), NEG)
        m = jnp.max(s, axis=1, keepdims=True)
        e = jnp.where(valid, jnp.exp2(s - m), 0.0)
        l = jnp.sum(e, axis=1, keepdims=True)
        p = e / jnp.where(l > 0.0, l, 1.0)
        o_ref[0, :, h * HEAD_DIM:(h + 1) * HEAD_DIM] = _bdot(p, vc).astype(o_ref.dtype)
        psum = psum + p
    imp = jnp.transpose(_dot_exact_rhs(psum, ov_ref[...]))[0:n_sel]
    jj = lax.broadcasted_iota(jnp.int32, (n_sel, tq), 0)
    blk_t = jnp.right_shift(i * tq + lax.broadcasted_iota(jnp.int32, (n_sel, tq), 1), SEL_SHIFT)
    forced = (jj == 0) | (jj == blk_t) | (jj == blk_t - 1)
    imp = jnp.where(forced, 1e9, jnp.where(jj > blk_t, -1.0, imp))
    ng = n_sel // SUBLANES
    groups = [imp[g * SUBLANES:(g + 1) * SUBLANES] for g in range(ng)]
    ranks = [jnp.zeros((SUBLANES, tq), F32) for _ in range(ng)]
    sub = lax.broadcasted_iota(jnp.int32, (SUBLANES, tq), 0)
    for ii in range(n_sel):
        gi, ri = divmod(ii, SUBLANES)
        row = jnp.broadcast_to(imp[ii:ii + 1, :], (SUBLANES, tq))
        for g in range(ng):
            if g > gi:
                beats = row >= groups[g]
            elif g < gi:
                beats = row > groups[g]
            else:
                beats = (row > groups[g]) | ((row == groups[g]) & (sub > ri))
            ranks[g] = ranks[g] + jnp.where(beats, 1.0, 0.0)
    chosen = jnp.where(jnp.concatenate(ranks, axis=0) < float(top), 1.0, 0.0)
    pad = sel_ref.shape[3] - n_sel
    if pad:
        chosen = jnp.concatenate([chosen, jnp.zeros((pad, tq), F32)], axis=0)
    sel_ref[0, 0] = jnp.transpose(chosen).astype(sel_ref.dtype)


def _nsa_cmp(proj_rope, kcmp, vcmp, overlap, n_sel, tq):
    b, s, _ = proj_rope.shape
    g = kcmp.shape[1]
    ncp = kcmp.shape[2]
    n_pad = overlap.shape[1]
    top = min(NSA_SEL_TOPK, n_sel)
    qw = NSA_HPG * HEAD_DIM
    return pl.pallas_call(
        functools.partial(_nsa_cmp_kernel, tq=tq, top=top, n_sel=n_sel),
        out_shape=(jax.ShapeDtypeStruct((b, s, NSA_WIDTH), BF16),
                   jax.ShapeDtypeStruct((b, g, s, n_pad), BF16)),
        grid=(b, g, s // tq),
        in_specs=[
            pl.BlockSpec((1, tq, qw), lambda bi, gi, i: (bi, i, ROPE_Q // qw + gi)),
            pl.BlockSpec((1, 1, ncp, HEAD_DIM), lambda bi, gi, i: (bi, gi, 0, 0)),
            pl.BlockSpec((1, 1, ncp, HEAD_DIM), lambda bi, gi, i: (bi, gi, 0, 0)),
            pl.BlockSpec(overlap.shape, lambda bi, gi, i: (0, 0)),
        ],
        out_specs=(pl.BlockSpec((1, tq, qw), lambda bi, gi, i: (bi, i, gi)),
                   pl.BlockSpec((1, 1, tq, n_pad), lambda bi, gi, i: (bi, gi, i, 0))),
        compiler_params=_cparams(("parallel", "parallel", "parallel")),
        name="nsa_cmp_select",
    )(proj_rope, kcmp, vcmp, overlap)


def _nsa_flash_kernel(qi_ref, kb_ref, first_ref, *refs, mode, tq, tk):
    if mode == "sel":
        q_ref, k_ref, v_ref, sel_ref, hot_ref, o_ref, m_ref, l_ref, acc_ref = refs
    else:
        q_ref, k_ref, v_ref, o_ref, m_ref, l_ref, acc_ref = refs
    n = pl.program_id(2)
    i = qi_ref[n]
    kb = kb_ref[n]

    @pl.when(first_ref[n] == 1)
    def _():
        m_ref[...] = jnp.full_like(m_ref, NEG)
        l_ref[...] = jnp.zeros_like(l_ref)
        acc_ref[...] = jnp.zeros_like(acc_ref)

    def step(diag):
        k = k_ref[0]
        valid = None
        if mode == "win":
            rows = i * tq + lax.broadcasted_iota(jnp.int32, (tq, tk), 0)
            cols = kb * tk + lax.broadcasted_iota(jnp.int32, (tq, tk), 1)
            d = rows - cols
            valid = (d >= 0) & (d < NSA_WINDOW)
        else:
            penalty = ((sel_ref[0, 0].astype(F32) - 1.0) * (-NEG)).astype(BF16)
            k = jnp.concatenate([k, hot_ref[...]], axis=1)
            if diag:
                valid = (lax.broadcasted_iota(jnp.int32, (tq, tk), 0) >= lax.broadcasted_iota(jnp.int32, (tq, tk), 1))
        v_ones = jnp.concatenate([v_ref[0], jnp.ones((tk, LANES), BF16)], axis=1)
        for h in range(NSA_HPG):
            q = q_ref[0, :, h * HEAD_DIM:(h + 1) * HEAD_DIM]
            if mode == "sel":
                q = jnp.concatenate([q, penalty], axis=1)
            s = _dot_nt(q, k)
            if valid is not None:
                s = jnp.where(valid, s, NEG)
            m_old = m_ref[h]
            m_new = jnp.maximum(m_old, jnp.max(s, axis=1, keepdims=True))
            alpha = jnp.exp2(m_old - m_new)
            p = jnp.exp2(s - jnp.concatenate([m_new] * (tk // LANES), axis=1))
            pv = jnp.dot(p.astype(BF16), v_ones, preferred_element_type=F32)
            l_ref[h] = alpha * l_ref[h] + pv[:, HEAD_DIM:]
            acc_ref[h] = alpha * acc_ref[h] + pv[:, 0:HEAD_DIM]
            m_ref[h] = m_new

    if mode == "sel":
        pl.when(kb < i)(functools.partial(step, False))
        pl.when(kb == i)(functools.partial(step, True))
    else:
        step(True)

    @pl.when(kb == i)
    def _():
        for h in range(NSA_HPG):
            o_ref[0, :, h * HEAD_DIM:(h + 1) * HEAD_DIM] = (acc_ref[h] / l_ref[h]).astype(o_ref.dtype)


def _nsa_flash(proj_rope, proj_plain, sel, mode, tq, tk):
    b, s, _ = proj_rope.shape
    g = NSA_GROUPS
    qw = NSA_HPG * HEAD_DIM
    nq = s // tq
    assert tq == tk
    if mode == "sel":
        back = nq
        kcol, vcol = ROPE_KS // HEAD_DIM, PL_VS // HEAD_DIM
    else:
        assert NSA_WINDOW % tk == 0
        back = NSA_WINDOW // tk
        kcol, vcol = ROPE_KW // HEAD_DIM, PL_VW // HEAD_DIM
    qi, kb, first = _causal_pairs(nq, back)
    in_specs = [
        pl.BlockSpec((1, tq, qw), lambda bi, gi, n, qi, kb, fs: (bi, qi[n], ROPE_Q // qw + gi)),
        pl.BlockSpec((1, tk, HEAD_DIM), lambda bi, gi, n, qi, kb, fs: (bi, kb[n], kcol + gi)),
        pl.BlockSpec((1, tk, HEAD_DIM), lambda bi, gi, n, qi, kb, fs: (bi, kb[n], vcol + gi)),
    ]
    args = [proj_rope, proj_rope, proj_plain]
    if mode == "sel":
        n_pad = sel.shape[3]
        in_specs.append(pl.BlockSpec((1, 1, tq, n_pad), lambda bi, gi, n, qi, kb, fs: (bi, gi, qi[n], 0)))
        in_specs.append(pl.BlockSpec((tk, n_pad), lambda bi, gi, n, qi, kb, fs: (kb[n], 0)))
        block_of_key = jnp.arange(s, dtype=jnp.int32)[:, None] // NSA_SEL_BLOCK
        onehot = (block_of_key == jnp.arange(n_pad, dtype=jnp.int32)[None, :]).astype(BF16)
        args += [sel, onehot]
    return pl.pallas_call(
        functools.partial(_nsa_flash_kernel, mode=mode, tq=tq, tk=tk),
        out_shape=jax.ShapeDtypeStruct((b, s, NSA_WIDTH), BF16),
        grid_spec=pltpu.PrefetchScalarGridSpec(
            num_scalar_prefetch=3,
            grid=(b, g, qi.shape[0]),
            in_specs=in_specs,
            out_specs=pl.BlockSpec((1, tq, qw), lambda bi, gi, n, qi, kb, fs: (bi, qi[n], gi)),
            scratch_shapes=[
                pltpu.VMEM((NSA_HPG, tq, LANES), F32),
                pltpu.VMEM((NSA_HPG, tq, LANES), F32),
                pltpu.VMEM((NSA_HPG, tq, HEAD_DIM), F32),
            ],
        ),
        compiler_params=_cparams(("parallel", "parallel", "arbitrary")),
        name="nsa_flash_" + mode,
    )(qi, kb, first, *args)


def _causal_pairs(nq, back):
    qi, kb, first = [], [], []
    for i in range(nq):
        lo = max(0, i - back)
        for j in range(lo, i + 1):
            qi.append(i)
            kb.append(j)
            first.append(1 if j == lo else 0)
    as_i32 = lambda z: jnp.asarray(np.asarray(z, np.int32))
    return as_i32(qi), as_i32(kb), as_i32(first)


def _diff_flash_kernel(qi_ref, kb_ref, first_ref, q_ref, k_ref, v_ref, lam_ref, sub_ref, o_ref, m_ref, l_ref,
                       acc_ref, *, tq, tk, lam_init):
    n = pl.program_id(2)
    i = qi_ref[n]
    kb = kb_ref[n]

    @pl.when(first_ref[n] == 1)
    def _():
        m_ref[...] = jnp.full_like(m_ref, NEG)
        l_ref[...] = jnp.zeros_like(l_ref)
        acc_ref[...] = jnp.zeros_like(acc_ref)

    hpb = o_ref.shape[2] // DIFF_VDIM

    def step(masked):
        if masked:
            valid = (lax.broadcasted_iota(jnp.int32, (tq, tk), 0) >= lax.broadcasted_iota(jnp.int32, (tq, tk), 1))
        for hd in range(hpb):
            v = v_ref[0, :, hd * DIFF_VDIM:(hd + 1) * DIFF_VDIM]
            for mp in range(2):
                slot = 2 * hd + mp
                q = q_ref[0, :, slot * HEAD_DIM:(slot + 1) * HEAD_DIM]
                k = k_ref[0, :, slot * HEAD_DIM:(slot + 1) * HEAD_DIM]
                s = _dot_nt(q, k)
                if masked:
                    s = jnp.where(valid, s, NEG)
                m_old = m_ref[slot]
                m_new = jnp.maximum(m_old, jnp.max(s, axis=1, keepdims=True))
                alpha = jnp.exp2(m_old - m_new)
                p = jnp.exp2(s - jnp.concatenate([m_new] * (tk // LANES), axis=1))
                l_ref[slot] = alpha * l_ref[slot] + jnp.sum(p, axis=1, keepdims=True)
                acc_ref[slot] = (jnp.concatenate([alpha] * (DIFF_VDIM // LANES), axis=1) * acc_ref[slot]
                                 + _bdot(p, v))
                m_ref[slot] = m_new

    pl.when(kb < i)(functools.partial(step, False))
    pl.when(kb == i)(functools.partial(step, True))

    @pl.when(kb == i)
    def _():
        lam = lam_ref[...]
        lam_full = (jnp.exp(jnp.sum(lam[0:1, :] * lam[1:2, :], axis=1, keepdims=True))
                    - jnp.exp(jnp.sum(lam[2:3, :] * lam[3:4, :], axis=1, keepdims=True)) + lam_init)
        wide = lambda x: jnp.concatenate([x] * (DIFF_VDIM // LANES), axis=1)
        for hd in range(hpb):
            o = (acc_ref[2 * hd] / wide(l_ref[2 * hd])
                 - lam_full * (acc_ref[2 * hd + 1] / wide(l_ref[2 * hd + 1])))
            ms = jnp.mean(o * o, axis=-1, keepdims=True)
            o = o * lax.rsqrt(ms + 1e-5) * sub_ref[...]
            o_ref[0, :, hd * DIFF_VDIM:(hd + 1) * DIFF_VDIM] = (o * (1.0 - lam_init)).astype(o_ref.dtype)


def _diff_flash(proj_rope, proj_plain, lam, subln, layer, tq, tk):
    b, s, _ = proj_rope.shape
    assert tq == tk
    nq = s // tq
    lam_init = 0.8 - 0.6 * math.exp(-0.3 * layer)
    hpb = 2
    w = hpb * DIFF_VDIM
    assert DIFF_HEADS % hpb == 0 and ROPE_DQ % w == 0 and ROPE_DK % w == 0 and PL_DV % w == 0
    qi, kb, first = _causal_pairs(nq, nq)
    return pl.pallas_call(
        functools.partial(_diff_flash_kernel, tq=tq, tk=tk, lam_init=lam_init),
        out_shape=jax.ShapeDtypeStruct((b, s, DIFF_WIDTH), BF16),
        grid_spec=pltpu.PrefetchScalarGridSpec(
            num_scalar_prefetch=3,
            grid=(b, DIFF_HEADS // hpb, qi.shape[0]),
            in_specs=[
                pl.BlockSpec((1, tq, w), lambda bi, h, n, qi, kb, fs: (bi, qi[n], ROPE_DQ // w + h)),
                pl.BlockSpec((1, tk, w), lambda bi, h, n, qi, kb, fs: (bi, kb[n], ROPE_DK // w + h)),
                pl.BlockSpec((1, tk, w), lambda bi, h, n, qi, kb, fs: (bi, kb[n], PL_DV // w + h)),
                pl.BlockSpec(lam.shape, lambda bi, h, n, qi, kb, fs: (0, 0)),
                pl.BlockSpec((1, DIFF_VDIM), lambda bi, h, n, qi, kb, fs: (0, 0)),
            ],
            out_specs=pl.BlockSpec((1, tq, w), lambda bi, h, n, qi, kb, fs: (bi, qi[n], h)),
            scratch_shapes=[
                pltpu.VMEM((2 * hpb, tq, LANES), F32),
                pltpu.VMEM((2 * hpb, tq, LANES), F32),
                pltpu.VMEM((2 * hpb, tq, DIFF_VDIM), F32),
            ],
        ),
        compiler_params=_cparams(("parallel", "parallel", "arbitrary")),
        name="diff_flash",
    )(qi, kb, first, proj_rope, proj_rope, proj_plain, lam, subln[None, :])


def _bmm(a, b):
    return jnp.einsum("umk,ukn->umn", a.astype(BF16), b.astype(BF16), preferred_element_type=F32)


def _bmm_nt(a, b):
    return jnp.einsum("umk,unk->umn", a.astype(BF16), b.astype(BF16), preferred_element_type=F32)


def _rwkv_scan_kernel(r_ref, k_ref, v_ref, gd_ref, wa_ref, mur_ref, muk_ref, muv_ref, mugd_ref, muwa_ref,
                      wdec_ref, wrate_ref, gup_ref, w0_ref, a0_ref, kk_ref, ka_ref, rk_ref, lnw_ref, lnb_ref,
                      o_ref, st_ref, last_ref, lastgd_ref, *, ts, chunk):
    @pl.when(pl.program_id(2) == 0)
    def _():
        st_ref[...] = jnp.zeros_like(st_ref)
        last_ref[...] = jnp.zeros_like(last_ref)
        lastgd_ref[...] = jnp.zeros_like(lastgd_ref)

    def shifted(x_ref, mu_ref, carry):
        x = x_ref[0]
        row = lax.broadcasted_iota(jnp.int32, x.shape, 0)
        prev = jnp.where(row == 0, carry, pltpu.roll(x, 1, 0))
        return x + (prev - x) * mu_ref[...]

    r = shifted(r_ref, mur_ref, last_ref[0, 0:1, :])
    k = shifted(k_ref, muk_ref, last_ref[1, 0:1, :])
    v = shifted(v_ref, muv_ref, last_ref[2, 0:1, :])
    wa = shifted(wa_ref, muwa_ref, last_ref[3, 0:1, :])
    gd = shifted(gd_ref, mugd_ref, lastgd_ref[0:1, :])
    for n, ref in enumerate((r_ref, k_ref, v_ref, wa_ref)):
        last_ref[n, 0:1, :] = ref[0, ts - 1:ts, :]
    lastgd_ref[0:1, :] = gd_ref[0, ts - 1:ts, :]

    wa = jnp.where(lax.broadcasted_iota(jnp.int32, wa.shape, 1) < RWKV_W_LORA, jnp.tanh(wa), wa)
    lw = -math.exp(-0.5) * jax.nn.sigmoid(w0_ref[...] + _bdot(wa, wdec_ref[...]))
    a = jax.nn.sigmoid(a0_ref[...] + _bdot(wa, wrate_ref[...]))
    g = _bdot(jax.nn.sigmoid(gd), gup_ref[...])

    c = chunk
    nc = ts // c
    head0 =lax.broadcasted_iota(jnp.int32, (ts, LANES), 1) < RWKV_HEAD

    def seg_sum(x):
        s0 = jnp.sum(jnp.where(head0, x, 0.0), axis=1, keepdims=True)
        s1 = jnp.sum(jnp.where(head0, 0.0, x), axis=1, keepdims=True)
        return jnp.where(head0, s0, s1)

    kk = k * kk_ref[...]
    kap = kk / jnp.maximum(jnp.sqrt(seg_sum(kk * kk)), 1e-12)
    kmod = k * (1.0 + (a - 1.0) * ka_ref[...])
    bvec = kap * a

    ri = lax.broadcasted_iota(jnp.int32, (ts, ts), 0)
    ci = lax.broadcasted_iota(jnp.int32, (ts, ts), 1)
    same_chunk = jnp.right_shift(ri, c.bit_length() - 1) == jnp.right_shift(ci, c.bit_length() - 1)
    tri = jnp.where(same_chunk & (ri >= ci), 1.0, 0.0).astype(BF16)
    cum = _dot_exact_lhs(tri, lw)
    g_incl = jnp.exp(cum)
    g_inv = jnp.exp(-cum)
    split = lambda x: x.reshape(nc, c, LANES)
    g_last = [g_incl[(ch + 1) * c - 1:(ch + 1) * c, :] for ch in range(nc)]
    g_last_rows = jnp.concatenate([jnp.broadcast_to(gl, (c, LANES)) for gl in g_last], axis=0)
    kh = split(kmod * g_inv)
    bh = split(bvec * g_inv)
    kg = split(kmod * g_inv * g_last_rows)
    bg = split(bvec * g_inv * g_last_rows)
    kaph = kap * jnp.exp(cum - lw)
    rh = r * g_incl
    v3 = split(v)

    both = lambda x: jnp.concatenate([split(jnp.where(head0, x, 0.0)), split(jnp.where(head0, 0.0, x))], axis=0)
    twice = lambda x: jnp.concatenate([x, x], axis=0)
    kap_m = both(kaph)
    r_m = both(rh)
    gram = _bmm_nt(jnp.concatenate([kap_m, r_m], axis=1), twice(jnp.concatenate([bh, kh], axis=1)))
    row = lax.broadcasted_iota(jnp.int32, (c, 2 * c), 0)
    col = lax.broadcasted_iota(jnp.int32, (c, 2 * c), 1) & (c - 1)
    lbk = jnp.where(row > col, gram[:, 0:c, :], 0.0)
    abk = jnp.where(row >= col, gram[:, c:2 * c, :], 0.0)
    lb = lbk[:, :, 0:c]
    tm = -lb
    q = _bmm(lb, lb)
    n = 2
    while True:
        tm = tm + q + _bmm(tm, q)
        n *= 2
        if n >= c:
            break
        q = _bmm(q, q)
    v2 = twice(v3)
    kap_t = kap_m + _bmm(tm, kap_m)
    lkv = _bmm(lbk, jnp.concatenate([jnp.zeros_like(v2), v2], axis=1))
    z0 = -(lkv + _bmm(tm, lkv))
    r_t = r_m - _bmm(abk, jnp.concatenate([kap_t, jnp.zeros_like(kap_t)], axis=1))
    y0 = _bmm(abk, jnp.concatenate([z0, v2], axis=1))
    h0 = lax.broadcasted_iota(jnp.int32, (nc, c, LANES), 2) < RWKV_HEAD
    merge = lambda x: jnp.where(h0, x[0:nc], x[nc:2 * nc])
    kap_t = kap_t[0:nc] + kap_t[nc:2 * nc]
    r_t = r_t[0:nc] + r_t[nc:2 * nc]
    z0 = merge(z0)
    y0 = merge(y0)

    br = lax.broadcasted_iota(jnp.int32, (LANES, LANES), 0)
    bc = lax.broadcasted_iota(jnp.int32, (LANES, LANES), 1)
    blockdiag = (br < RWKV_HEAD) == (bc < RWKV_HEAD)
    state = st_ref[...]
    ys = []
    for ch in range(nc):
        trans = jnp.where(blockdiag, -_dot_tn(bg[ch], kap_t[ch]), 0.0)
        trans = trans + jnp.where(br == bc, jnp.broadcast_to(g_last[ch], (LANES, LANES)), 0.0)
        inject = jnp.where(blockdiag, _dot_tn(jnp.concatenate([bg[ch], kg[ch]], axis=0),
                                              jnp.concatenate([z0[ch], v3[ch]], axis=0)), 0.0)
        ys.append(_dot3(r_t[ch], state) + y0[ch])
        state = _dot3(trans, state) + inject
    st_ref[...] = state
    y = jnp.concatenate(ys, axis=0)

    mean = seg_sum(y) * (1.0 / RWKV_HEAD)
    yc = y - mean
    var = seg_sum(yc * yc) * (1.0 / RWKV_HEAD)
    yn = yc * lax.rsqrt(var + RWKV_LNX_EPS) * lnw_ref[...] + lnb_ref[...]
    bonus = seg_sum(r * kmod * rk_ref[...]) * v
    o_ref[0] = ((yn + bonus) * g).astype(o_ref.dtype)


def _rwkv_scan(proj3, mu_p, wwa, gup, w0, a0, k_k, k_a, r_k, lnx_w, lnx_b, ts=512, chunk=64):
    b, s, _ = proj3.shape
    w = RWKV_WIDTH
    ts = min(ts, s)
    assert s % ts == 0 and ts % chunk == 0
    npair = w // LANES
    col = lambda off: pl.BlockSpec((1, ts, LANES), lambda bi, j, t: (bi, t, off // LANES + j))
    mucol = lambda off: pl.BlockSpec((1, LANES), lambda bi, j, t: (0, off // LANES + j))
    vec = pl.BlockSpec((1, LANES), lambda bi, j, t: (0, j))
    in_specs = [
        col(0), col(w), col(2 * w),
        pl.BlockSpec((1, ts, RW_G_PAD), lambda bi, j, t: (bi, t, RW_GD // RW_G_PAD)),
        pl.BlockSpec((1, ts, RW_LORA_PAD), lambda bi, j, t: (bi, t, RW_WA // RW_LORA_PAD)),
        mucol(0), mucol(w), mucol(2 * w),
        pl.BlockSpec((1, RW_G_PAD), lambda bi, j, t: (0, RW_GD // RW_G_PAD)),
        pl.BlockSpec((1, RW_LORA_PAD), lambda bi, j, t: (0, RW_WA // RW_LORA_PAD)),
        pl.BlockSpec((RW_LORA_PAD, LANES), lambda bi, j, t: (0, j)),
        pl.BlockSpec((RW_LORA_PAD, LANES), lambda bi, j, t: (0, npair + j)),
        pl.BlockSpec((RW_G_PAD, LANES), lambda bi, j, t: (0, j)),
    ] + [vec] * 7
    row = lambda z: z[None, :]
    return pl.pallas_call(
        functools.partial(_rwkv_scan_kernel, ts=ts, chunk=chunk),
        out_shape=jax.ShapeDtypeStruct((b, s, w), BF16),
        grid=(b, npair, s // ts),
        in_specs=in_specs,
        out_specs=pl.BlockSpec((1, ts, LANES), lambda bi, j, t: (bi, t, j)),
        scratch_shapes=[
            pltpu.VMEM((LANES, LANES), F32),
            pltpu.VMEM((4, SUBLANES, LANES), F32),
            pltpu.VMEM((SUBLANES, RW_G_PAD), F32),
        ],
        compiler_params=_cparams(("parallel", "parallel", "arbitrary")),
        name="rwkv_scan",
    )(proj3, proj3, proj3, proj3, proj3, mu_p, mu_p, mu_p, mu_p, mu_p, wwa, wwa, gup,
      row(w0), row(a0), row(k_k), row(k_a), row(r_k), row(lnx_w), row(lnx_b))


def _merge_kernel(oc_ref, os_ref, ow_ref, ng_ref, yd_ref, yr_ref, wb_ref, g0_ref, g1_ref, g2_ref, bias_ref,
                  o_ref, yn_ref):
    @pl.when(pl.program_id(1) == 0)
    def _():
        gates = jax.nn.sigmoid(ng_ref[...])
        for hd in range(NSA_HEADS):
            sl = slice(hd * HEAD_DIM, (hd + 1) * HEAD_DIM)
            o = (gates[:, 3 * hd:3 * hd + 1] * oc_ref[:, sl]
                 + gates[:, 3 * hd + 1:3 * hd + 2] * os_ref[:, sl]
                 + gates[:, 3 * hd + 2:3 * hd + 3] * ow_ref[:, sl])
            yn_ref[:, sl] = o.astype(BF16)

    branches = (yn_ref, yd_ref, yr_ref)
    graw = (g0_ref, g1_ref, g2_ref)
    acc = None
    for bi in range(N_BRANCH):
        gate = jax.nn.sigmoid(graw[bi][...] + bias_ref[bi])
        term = gate * jnp.dot(branches[bi][...], wb_ref[bi], preferred_element_type=F32)
        acc = term if acc is None else acc + term
    o_ref[...] = acc.astype(o_ref.dtype)


def _merge(o_cmp, o_slc, o_win, proj_f32, proj_plain, y_diff, y_rwkv, w_branch, b_gate, tm=1024, tn=512):
    m = o_cmp.shape[0]
    n = w_branch.shape[2]
    tm = min(tm, m)
    assert m % tm == 0 and n % tn == 0 and PL_BG % tn == 0
    bw = BRANCH_WIDTH
    rowblk = lambda: pl.BlockSpec((tm, bw), lambda i, j: (i, 0))
    gate_spec = lambda bi: pl.BlockSpec((tm, tn), lambda i, j: (i, PL_BG // tn + bi * (n // tn) + j))
    return pl.pallas_call(
        _merge_kernel,
        out_shape=jax.ShapeDtypeStruct((m, n), BF16),
        grid=(m // tm, n // tn),
        in_specs=[
            rowblk(), rowblk(), rowblk(),
            pl.BlockSpec((tm, LANES), lambda i, j: (i, F_NG // LANES)),
            rowblk(), rowblk(),
            pl.BlockSpec((N_BRANCH, bw, tn), lambda i, j: (0, 0, j)),
            gate_spec(0), gate_spec(1), gate_spec(2),
            pl.BlockSpec((N_BRANCH, 1, tn), lambda i, j: (0, 0, j)),
        ],
        out_specs=pl.BlockSpec((tm, tn), lambda i, j: (i, j)),
        scratch_shapes=[pltpu.VMEM((tm, bw), BF16)],
        compiler_params=_cparams(("parallel", "arbitrary")),
        name="gated_merge",
    )(o_cmp, o_slc, o_win, proj_f32, y_diff, y_rwkv, w_branch, proj_plain, proj_plain, proj_plain,
      b_gate[:, None, :])


def _pack_layer_weights(w_in, w_gate, rwkv_mu, w_up_lora, a_up_lora, g_up_lora):
    d = w_in.shape[0]
    nsa_sizes = (NSA_WIDTH,) + (NSA_KV,) * 6 + (3 * NSA_HEADS,)
    diff_sizes = (DIFF_QK, DIFF_QK, DIFF_WIDTH)
    rw_sizes = (RWKV_WIDTH,) * 3 + (RWKV_W_LORA, RWKV_A_LORA, RWKV_G_LORA)
    offs = np.cumsum((0,) + nsa_sizes + diff_sizes + rw_sizes)
    seg = [w_in[:, offs[i]:offs[i + 1]] for i in range(len(offs) - 1)]
    q, kc, vc, ks, vs, kw, vw, ng, dq, dk, dv, rr, rk, rv, wd, ad, gd = seg
    w_rope = jnp.concatenate([q, dq, ks, kw, dk], axis=1).astype(BF16)
    w_plain = jnp.concatenate([kc, vc, vs, vw, dv] + [w_gate[bi] for bi in range(N_BRANCH)], axis=1).astype(BF16)
    zpad = lambda n: jnp.zeros((d, n), w_in.dtype)
    w_f32 = jnp.concatenate(
        [rr, rk, rv, gd, zpad(RW_G_PAD - RWKV_G_LORA), wd, ad, ng, zpad(LANES - 3 * NSA_HEADS)],
        axis=1).astype(BF16)
    mu_rkv, mu_wa, mu_gd = jnp.split(rwkv_mu, [3 * RWKV_WIDTH, 3 * RWKV_WIDTH + RW_LORA_PAD])
    mu_p = jnp.concatenate([mu_rkv, mu_gd, jnp.zeros((RW_G_PAD - RWKV_G_LORA,), rwkv_mu.dtype), mu_wa])[None, :]
    zz = jnp.zeros((RWKV_W_LORA, RWKV_WIDTH), w_up_lora.dtype)
    wwa = jnp.concatenate([jnp.concatenate([w_up_lora, zz], axis=1),
                           jnp.concatenate([zz, a_up_lora], axis=1)], axis=0).astype(BF16)
    gup = jnp.concatenate([g_up_lora, jnp.zeros((RW_G_PAD - RWKV_G_LORA, RWKV_WIDTH), g_up_lora.dtype)],
                          axis=0).astype(BF16)
    return w_rope, w_plain, w_f32, mu_p, wwa, gup


def _overlap_matrix(n_cp, n_sel):
    c_start = np.arange(n_cp) * NSA_CMP_STRIDE
    c_end = c_start + NSA_CMP_LEN - 1
    j_start = np.arange(n_sel) * NSA_SEL_BLOCK
    ov = (c_start[:, None] <= j_start[None, :] + NSA_SEL_BLOCK - 1) & (c_end[:, None] >= j_start[None, :])
    ov = np.pad(ov.astype(np.float32), ((0, 0), (0, -n_sel % LANES)))
    return jnp.asarray(ov).astype(BF16)


def _chunk_rows(z, groups):
    b, s, _ = z.shape
    z = z.reshape(b, s // NSA_CMP_STRIDE, NSA_CMP_STRIDE, groups, HEAD_DIM)
    return z.transpose(0, 3, 1, 2, 4).reshape(b, groups, s // NSA_CMP_STRIDE, NSA_CMP_STRIDE * HEAD_DIM)


def _nsa_branches(proj_rope, proj_plain, positions, cmp_pos, cmp_w1, cmp_w2, tq):
    b, s, _ = proj_rope.shape
    n_chunk = s // NSA_CMP_STRIDE
    n_sel = s // NSA_SEL_BLOCK
    kc2 = _chunk_rows(proj_plain[:, :, PL_KC:PL_KC + NSA_KV], NSA_GROUPS)
    vc2 = _chunk_rows(proj_plain[:, :, PL_VC:PL_VC + NSA_KV], NSA_GROUPS)
    cmp_end = np.minimum(np.arange(n_chunk) * NSA_CMP_STRIDE + NSA_CMP_LEN - 1, s - 1)
    pos_c = jnp.take(positions, jnp.asarray(cmp_end), axis=1)
    cosc, sinc = _rope_tables(pos_c.reshape(-1))
    cosc = cosc.reshape(b, n_chunk, LANES)
    sinc = sinc.reshape(b, n_chunk, LANES)
    w1 = cmp_w1.astype(BF16)
    w2 = cmp_w2.astype(BF16)
    pe = cmp_pos.reshape(2, 1, NSA_CMP_LEN * HEAD_DIM)
    kcmp = _nsa_compress(kc2, w1[0], w2[0], pe[0], cosc, sinc, True)
    vcmp = _nsa_compress(vc2, w1[1], w2[1], pe[1], cosc, sinc, False)
    o_cmp, sel = _nsa_cmp(proj_rope, kcmp, vcmp, _overlap_matrix(n_chunk, n_sel), n_sel, tq)
    o_slc = _nsa_flash(proj_rope, proj_plain, sel, "sel", tq, tq)
    o_win = _nsa_flash(proj_rope, proj_plain, None, "win", tq, tq)
    return o_cmp, o_slc, o_win


def _layer(x2, b, s, layer, cosf, sinf, positions, p):
    m = b * s
    w_rope, w_plain, w_f32, mu_p, wwa, gup = _pack_layer_weights(
        p["w_in"], p["w_gate"], p["rwkv_mu"], p["rwkv_w_up"], p["rwkv_a_up"], p["rwkv_g_up"])
    scale = HEAD_DIM ** -0.5 * math.log2(math.e)
    colscale =jnp.concatenate([jnp.full((ROPE_KS,), scale, F32), jnp.ones((ROPE_COLS - ROPE_KS,), F32)])[None, :]
    g_pre = p["norm_pre_mix"]
    proj_rope = _norm_matmul(x2, g_pre, w_rope, "rope", BF16, rope=(cosf, sinf, colscale), tn=ROPE_COLS // 4)
    proj_plain = _norm_matmul(x2, g_pre, w_plain, "plain", BF16, tn=1024)
    proj_f32 = _norm_matmul(x2, g_pre, w_f32, "plain", F32, tn=F32_COLS // 4)
    pr3 = proj_rope.reshape(b, s, ROPE_COLS)
    pp3 = proj_plain.reshape(b, s, PLAIN_COLS)

    tq = min(512, s)
    o_cmp, o_slc, o_win = _nsa_branches(pr3, pp3, positions, p["nsa_cmp_pos"], p["nsa_cmp_w1"], p["nsa_cmp_w2"], tq)
    y_diff = _diff_flash(pr3, pp3, p["diff_lambda"], p["diff_subln"], layer, tq, tq)

    y_rwkv = _rwkv_scan(proj_f32.reshape(b, s, F32_COLS), mu_p, wwa, gup, p["rwkv_w0"], p["rwkv_a0"],
                        p["rwkv_k_k"], p["rwkv_k_a"], p["rwkv_r_k"].reshape(-1), p["rwkv_lnx_w"], p["rwkv_lnx_b"])

    merged = _merge(o_cmp.reshape(m, NSA_WIDTH), o_slc.reshape(m, NSA_WIDTH), o_win.reshape(m, NSA_WIDTH),
                    proj_f32, proj_plain, y_diff.reshape(m, DIFF_WIDTH), y_rwkv.reshape(m, RWKV_WIDTH),
                    p["w_branch"].astype(BF16), p["b_gate"])
    x2 = _matmul_norm_res(merged, p["w_out"].astype(BF16), p["norm_post_mix"], x2)
    hidden = _norm_matmul(x2, p["norm_pre_mlp"], p["w_up"].astype(BF16), "relu2", BF16, tn=1024)
    x2 = _matmul_norm_res(hidden, p["w_down"].astype(BF16), p["norm_post_mlp"], x2)
    return x2


def kernel(x, positions, norm_pre_mix, norm_post_mix, norm_pre_mlp, norm_post_mlp, w_in, nsa_cmp_pos, nsa_cmp_w1, nsa_cmp_w2, diff_lambda, diff_subln, rwkv_mu, rwkv_w0, rwkv_w_up, rwkv_a0, rwkv_a_up, rwkv_g_up, rwkv_k_k, rwkv_k_a, rwkv_r_k, rwkv_lnx_w, rwkv_lnx_b, w_gate, b_gate, w_branch, w_out, w_up, w_down):
    b, s, d = x.shape
    depth = w_in.shape[0]
    stacked = dict(
        norm_pre_mix=norm_pre_mix, norm_post_mix=norm_post_mix, norm_pre_mlp=norm_pre_mlp,
        norm_post_mlp=norm_post_mlp, w_in=w_in, nsa_cmp_pos=nsa_cmp_pos, nsa_cmp_w1=nsa_cmp_w1,
        nsa_cmp_w2=nsa_cmp_w2, diff_lambda=diff_lambda, diff_subln=diff_subln, rwkv_mu=rwkv_mu,
        rwkv_w0=rwkv_w0, rwkv_w_up=rwkv_w_up, rwkv_a0=rwkv_a0, rwkv_a_up=rwkv_a_up, rwkv_g_up=rwkv_g_up,
        rwkv_k_k=rwkv_k_k, rwkv_k_a=rwkv_k_a, rwkv_r_k=rwkv_r_k, rwkv_lnx_w=rwkv_lnx_w,
        rwkv_lnx_b=rwkv_lnx_b, w_gate=w_gate, b_gate=b_gate, w_branch=w_branch, w_out=w_out,
        w_up=w_up, w_down=w_down)
    cosf, sinf = _rope_tables(positions.reshape(-1))
    x2 = x.reshape(b * s, d)
    for layer in range(depth):
        x2 = _layer(x2, b, s, layer, cosf, sinf, positions, {n: a[layer] for n, a in stacked.items()})
    return x2.reshape(b, s, d)
```

```python
import functools
import math

import jax
import jax.numpy as jnp
import numpy as np
from jax import lax
from jax.experimental import pallas as pl
from jax.experimental.pallas import tpu as pltpu

F32 = jnp.float32
BF16 = jnp.bfloat16

D_MODEL = 2048
RMS_EPS = 1e-6
ROPE_THETA = 500000.0
HEAD_DIM = 128
ROT_HALF = HEAD_DIM // 8
NSA_HEADS = 8
NSA_GROUPS = 2
NSA_HPG = NSA_HEADS // NSA_GROUPS
NSA_CMP_LEN = 32
NSA_CMP_STRIDE = 16
NSA_CMP_HIDDEN = 256
NSA_SEL_BLOCK = 64
SEL_SHIFT = NSA_SEL_BLOCK.bit_length() - 1
NSA_SEL_TOPK = 16
NSA_WINDOW = 512
NSA_WIDTH = NSA_HEADS * HEAD_DIM
NSA_KV = NSA_GROUPS * HEAD_DIM
DIFF_HEADS = 4
DIFF_VDIM = 2 * HEAD_DIM
DIFF_WIDTH = DIFF_HEADS * DIFF_VDIM
DIFF_QK = 2 * DIFF_HEADS * HEAD_DIM
RWKV_HEAD = 64
RWKV_WIDTH = 1024
RWKV_W_LORA = 64
RWKV_A_LORA = 64
RWKV_G_LORA = 160
RWKV_LNX_EPS = 64e-5
N_BRANCH = 3
BRANCH_WIDTH = 1024

LANES = 128
SUBLANES = 8
VMEM_LIMIT_BYTES = 56 * 1024 * 1024

NEG = -1e30

ROPE_Q, ROPE_DQ, ROPE_DK, ROPE_KS, ROPE_KW = 0, 1024, 2048, 3072, 3328
ROPE_SCALED = ROPE_DK
ROPE_COLS = 3584
PL_KC, PL_VC, PL_VS, PL_VW, PL_DV, PL_BG = 0, 256, 512, 768, 1024, 2048
PLAIN_COLS = PL_BG + N_BRANCH * D_MODEL
RW_LORA_PAD = 128
RW_G_PAD = 256
RW_GD = 3 * RWKV_WIDTH
RW_WA = RW_GD + RW_G_PAD
RW_COLS = RW_WA + RW_LORA_PAD
F_NG = RW_COLS
F32_COLS = RW_COLS + LANES


def _cparams(sem):
    return pltpu.CompilerParams(dimension_semantics=sem, vmem_limit_bytes=VMEM_LIMIT_BYTES)


def _bdot(a, b):
    return jnp.dot(a.astype(BF16), b.astype(BF16), preferred_element_type=F32)


def _dot_nt(a, b):
    return lax.dot_general(a.astype(BF16), b.astype(BF16), (((1,), (1,)), ((), ())),
                           preferred_element_type=F32)


def _dot_tn(a, b):
    return lax.dot_general(a.astype(BF16), b.astype(BF16), (((0,), (0,)), ((), ())),
                           preferred_element_type=F32)


def _split2(x):
    hi = x.astype(BF16)
    lo = (x - hi.astype(F32)).astype(BF16)
    return hi, lo


def _split3(x):
    hi = x.astype(BF16)
    r1 = x - hi.astype(F32)
    mid = r1.astype(BF16)
    lo = (r1 - mid.astype(F32)).astype(BF16)
    return hi, mid, lo


def _dot_exact_lhs(a_bf, x):
    hi, mid, lo = _split3(x)
    d = lambda p: jnp.dot(a_bf, p, preferred_element_type=F32)
    return d(hi) + (d(mid) + d(lo))


def _dot_exact_rhs(x, b_bf):
    hi, mid, lo = _split3(x)
    d = lambda p: jnp.dot(p, b_bf, preferred_element_type=F32)
    return d(hi) + (d(mid) + d(lo))


def _dot3(a, b):
    ah, al = _split2(a)
    bh, bl = _split2(b)
    d = lambda p, q: jnp.dot(p, q, preferred_element_type=F32)
    return d(ah, bh) + (d(ah, bl) + d(al, bh))


def _rope_partner(z, lane):
    return jnp.where(lane < ROT_HALF, pltpu.roll(z, LANES - ROT_HALF, 1), pltpu.roll(z, ROT_HALF, 1))


def _rope_table_kernel(pos_ref, invf_ref, sign_ref, cos_ref, sin_ref):
    ang = pos_ref[...].astype(F32) * invf_ref[...]
    cos_ref[...] = jnp.cos(ang)
    sin_ref[...] = jnp.sin(ang) * sign_ref[...]


def _rope_tables(pos_flat):
    n = pos_flat.shape[0]
    half = ROT_HALF
    inv_freq = ROPE_THETA ** (-jnp.arange(half, dtype=F32) / half)
    zeros = jnp.zeros((LANES - 2 * half,), F32)
    invf = jnp.concatenate([inv_freq, inv_freq, zeros])[None, :]
    sign = jnp.concatenate([-jnp.ones((half,), F32), jnp.ones((half,), F32), zeros])[None, :]
    tm = min(n, 2048)
    assert n % tm == 0
    vec = pl.BlockSpec((1, LANES), lambda i: (0, 0))
    out = pl.BlockSpec((tm, LANES), lambda i: (i, 0))
    return pl.pallas_call(
        _rope_table_kernel,
        out_shape=(jax.ShapeDtypeStruct((n, LANES), F32),) * 2,
        grid=(n // tm,),
        in_specs=[pl.BlockSpec((tm, 1), lambda i: (i, 0)), vec, vec],
        out_specs=(out, out),
        compiler_params=_cparams(("parallel",)),
        name="rope_tables",
    )(pos_flat[:, None], invf, sign)


def _norm_matmul_kernel(x_ref, g_ref, w_ref, *rest, mode, tn):
    if mode == "rope":
        cos_ref, sin_ref, cs_ref, o_ref, u_ref = rest
    else:
        o_ref, u_ref = rest

    @pl.when(pl.program_id(1) == 0)
    def _():
        x = x_ref[...]
        ms = jnp.mean(x * x, axis=-1, keepdims=True)
        u_ref[...] = (x * lax.rsqrt(ms + RMS_EPS) * g_ref[...]).astype(BF16)

    acc = jnp.dot(u_ref[...], w_ref[...], preferred_element_type=F32)
    if mode == "rope":
        acc = acc * cs_ref[...]
        cosv = cos_ref[...]
        sinv = sin_ref[...]
        lane = lax.broadcasted_iota(jnp.int32, cosv.shape, 1)
        for h in range(tn // LANES):
            z = acc[:, h * LANES:(h + 1) * LANES]
            o_ref[:, h * LANES:(h + 1) * LANES] = (z * cosv + _rope_partner(z, lane) * sinv).astype(o_ref.dtype)
    elif mode == "relu2":
        o_ref[...] = jnp.square(jnp.maximum(acc, 0.0)).astype(o_ref.dtype)
    else:
        o_ref[...] = acc.astype(o_ref.dtype)


def _norm_matmul(x, g, w, mode, out_dtype, rope=None, tm=1024, tn=512):
    m, d = x.shape
    n = w.shape[1]
    tm = min(tm, m)
    assert m % tm == 0 and n % tn == 0
    in_specs = [
        pl.BlockSpec((tm, d), lambda i, j: (i, 0)),
        pl.BlockSpec((1, d), lambda i, j: (0, 0)),
        pl.BlockSpec((d, tn), lambda i, j: (0, j)),
    ]
    args = [x, g[None, :], w]
    if mode == "rope":
        cosf, sinf, colscale = rope
        in_specs += [
            pl.BlockSpec((tm, LANES), lambda i, j: (i, 0)),
            pl.BlockSpec((tm, LANES), lambda i, j: (i, 0)),
            pl.BlockSpec((1, tn), lambda i, j: (0, j)),
        ]
        args += [cosf, sinf, colscale]
    return pl.pallas_call(
        functools.partial(_norm_matmul_kernel, mode=mode, tn=tn),
        out_shape=jax.ShapeDtypeStruct((m, n), out_dtype),
        grid=(m // tm, n // tn),
        in_specs=in_specs,
        out_specs=pl.BlockSpec((tm, tn), lambda i, j: (i, j)),
        scratch_shapes=[pltpu.VMEM((tm, d), BF16)],
        compiler_params=_cparams(("parallel", "arbitrary")),
        name="norm_matmul_" + mode,
    )(*args)


def _matmul_norm_res_kernel(a_ref, w_ref, g_ref, res_ref, o_ref, acc_ref):
    k = pl.program_id(1)

    @pl.when(k == 0)
    def _():
        acc_ref[...] = jnp.zeros_like(acc_ref)

    acc_ref[...] += jnp.dot(a_ref[...], w_ref[...], preferred_element_type=F32)

    @pl.when(k == pl.num_programs(1) - 1)
    def _():
        y = acc_ref[...]
        ms = jnp.mean(y * y, axis=-1, keepdims=True)
        o_ref[...] = res_ref[...] + y * lax.rsqrt(ms + RMS_EPS) * g_ref[...]


def _matmul_norm_res(a, w, g, res, tm=512, tk=2048):
    m, kdim = a.shape
    n = w.shape[1]
    tm = min(tm, m)
    assert m % tm == 0 and kdim % tk == 0
    return pl.pallas_call(
        _matmul_norm_res_kernel,
        out_shape=jax.ShapeDtypeStruct((m, n), F32),
        grid=(m // tm, kdim // tk),
        in_specs=[
            pl.BlockSpec((tm, tk), lambda i, k: (i, k)),
            pl.BlockSpec((tk, n), lambda i, k: (k, 0)),
            pl.BlockSpec((1, n), lambda i, k: (0, 0)),
            pl.BlockSpec((tm, n), lambda i, k: (i, 0)),
        ],
        out_specs=pl.BlockSpec((tm, n), lambda i, k: (i, 0)),
        scratch_shapes=[pltpu.VMEM((tm, n), F32)],
        compiler_params=_cparams(("parallel", "arbitrary")),
        name="matmul_norm_res",
    )(a, w, g[None, :], res)


def _gelu_tanh(x):
    return 0.5 * x * (1.0 + jnp.tanh(math.sqrt(2.0 / math.pi) * (x + 0.044715 * (x * x * x))))


def _nsa_compress_kernel(x_ref, w1_ref, w2_ref, pe_ref, cos_ref, sin_ref, o_ref, *, use_rope):
    x = x_ref[0, 0]
    half = x.shape[1]
    n_chunk = x.shape[0]
    a = jnp.dot(x, w1_ref[0:half, :], preferred_element_type=F32)
    b = jnp.dot(x, w1_ref[half:2 * half, :], preferred_element_type=F32)
    pe = jnp.broadcast_to(pe_ref[...], (SUBLANES, pe_ref.shape[1]))
    peb = _dot_exact_rhs_general(pe, w1_ref[...])[0:1, :]
    h = a + pltpu.roll(b, n_chunk - 1, 0) + peb
    y = _bdot(_gelu_tanh(h), w2_ref[...])
    if use_rope:
        lane = lax.broadcasted_iota(jnp.int32, y.shape, 1)
        y = y * cos_ref[0] + _rope_partner(y, lane) * sin_ref[0]
    o_ref[0, 0] = y.astype(o_ref.dtype)


def _dot_exact_rhs_general(x, w_bf):
    hi, mid, lo = _split3(x)
    d = lambda p: jnp.dot(p, w_bf, preferred_element_type=F32)
    return d(hi) + (d(mid) + d(lo))


def _nsa_compress(x2, w1, w2, pe_flat, cosc, sinc, use_rope):
    b, g, n_chunk, width = x2.shape
    dh = w2.shape[1]
    return pl.pallas_call(
        functools.partial(_nsa_compress_kernel, use_rope=use_rope),
        out_shape=jax.ShapeDtypeStruct((b, g, n_chunk, dh), BF16),
        grid=(b, g),
        in_specs=[
            pl.BlockSpec((1, 1, n_chunk, width), lambda i, j: (i, j, 0, 0)),
            pl.BlockSpec(w1.shape, lambda i, j: (0, 0)),
            pl.BlockSpec(w2.shape, lambda i, j: (0, 0)),
            pl.BlockSpec(pe_flat.shape, lambda i, j: (0, 0)),
            pl.BlockSpec((1, n_chunk, dh), lambda i, j: (i, 0, 0)),
            pl.BlockSpec((1, n_chunk, dh), lambda i, j: (i, 0, 0)),
        ],
        out_specs=pl.BlockSpec((1, 1, n_chunk, dh), lambda i, j: (i, j, 0, 0)),
        compiler_params=_cparams(("parallel", "parallel")),
        name="nsa_compress",
    )(x2, w1, w2, pe_flat, cosc, sinc)


def _nsa_cmp_kernel(q_ref, kc_ref, vc_ref, ov_ref, o_ref, sel_ref, *, tq, top, n_sel):
    i = pl.program_id(2)
    kc = kc_ref[0, 0]
    vc = vc_ref[0, 0]
    ncp = kc.shape[0]
    t = i * tq + lax.broadcasted_iota(jnp.int32, (tq, ncp), 0)
    c = lax.broadcasted_iota(jnp.int32, (tq, ncp), 1)
    valid = (c * NSA_CMP_STRIDE + (NSA_CMP_LEN - 1)) <= t
    psum = jnp.zeros((tq, ncp), F32)
    for h in range(NSA_HPG):
        q = q_ref[0, :, h * HEAD_DIM:(h + 1) * HEAD_DIM]
        s = jnp.where(valid, _dot_nt(q, kc), NEG)
        m = jnp.max(s, axis=1, keepdims=True)
        e = jnp.where(valid, jnp.exp2(s - m), 0.0)
        l = jnp.sum(e, axis=1, keepdims=True)
        p = e / jnp.where(l > 0.0, l, 1.0)
        o_ref[0, :, h * HEAD_DIM:(h + 1) * HEAD_DIM] = _bdot(p, vc).astype(o_ref.dtype)
        psum = psum + p
    imp = jnp.transpose(_dot_exact_rhs(psum, ov_ref[...]))[0:n_sel]
    jj = lax.broadcasted_iota(jnp.int32, (n_sel, tq), 0)
    blk_t = jnp.right_shift(i * tq + lax.broadcasted_iota(jnp.int32, (n_sel, tq), 1), SEL_SHIFT)
    forced = (jj == 0) | (jj == blk_t) | (jj == blk_t - 1)
    imp = jnp.where(forced, 1e9, jnp.where(jj > blk_t, -1.0, imp))
    ng = n_sel // SUBLANES
    groups = [imp[g * SUBLANES:(g + 1) * SUBLANES] for g in range(ng)]
    ranks = [jnp.zeros((SUBLANES, tq), F32) for _ in range(ng)]
    sub = lax.broadcasted_iota(jnp.int32, (SUBLANES, tq), 0)
    for ii in range(n_sel):
        gi, ri = divmod(ii, SUBLANES)
        row = jnp.broadcast_to(imp[ii:ii + 1, :], (SUBLANES, tq))
        for g in range(ng):
            if g > gi:
                beats = row >= groups[g]
            elif g < gi:
                beats = row > groups[g]
            else:
                beats = (row > groups[g]) | ((row == groups[g]) & (sub > ri))
            ranks[g] = ranks[g] + jnp.where(beats, 1.0, 0.0)
    chosen = jnp.where(jnp.concatenate(ranks, axis=0) < float(top), 1.0, 0.0)
    pad = sel_ref.shape[3] - n_sel
    if pad:
        chosen = jnp.concatenate([chosen, jnp.zeros((pad, tq), F32)], axis=0)
    sel_ref[0, 0] = jnp.transpose(chosen).astype(sel_ref.dtype)


def _nsa_cmp(proj_rope, kcmp, vcmp, overlap, n_sel, tq):
    b, s, _ = proj_rope.shape
    g = kcmp.shape[1]
    ncp = kcmp.shape[2]
    n_pad = overlap.shape[1]
    top = min(NSA_SEL_TOPK, n_sel)
    qw = NSA_HPG * HEAD_DIM
    return pl.pallas_call(
        functools.partial(_nsa_cmp_kernel, tq=tq, top=top, n_sel=n_sel),
        out_shape=(jax.ShapeDtypeStruct((b, s, NSA_WIDTH), BF16),
                   jax.ShapeDtypeStruct((b, g, s, n_pad), BF16)),
        grid=(b, g, s // tq),
        in_specs=[
            pl.BlockSpec((1, tq, qw), lambda bi, gi, i: (bi, i, ROPE_Q // qw + gi)),
            pl.BlockSpec((1, 1, ncp, HEAD_DIM), lambda bi, gi, i: (bi, gi, 0, 0)),
            pl.BlockSpec((1, 1, ncp, HEAD_DIM), lambda bi, gi, i: (bi, gi, 0, 0)),
            pl.BlockSpec(overlap.shape, lambda bi, gi, i: (0, 0)),
        ],
        out_specs=(pl.BlockSpec((1, tq, qw), lambda bi, gi, i: (bi, i, gi)),
                   pl.BlockSpec((1, 1, tq, n_pad), lambda bi, gi, i: (bi, gi, i, 0))),
        compiler_params=_cparams(("parallel", "parallel", "parallel")),
        name="nsa_cmp_select",
    )(proj_rope, kcmp, vcmp, overlap)


def _nsa_flash_kernel(qi_ref, kb_ref, first_ref, *refs, mode, tq, tk):
    if mode == "sel":
        q_ref, k_ref, v_ref, sel_ref, hot_ref, o_ref, m_ref, l_ref, acc_ref = refs
    else:
        q_ref, k_ref, v_ref, o_ref, m_ref, l_ref, acc_ref = refs
    n = pl.program_id(2)
    i = qi_ref[n]
    kb = kb_ref[n]

    @pl.when(first_ref[n] == 1)
    def _():
        m_ref[...] = jnp.full_like(m_ref, NEG)
        l_ref[...] = jnp.zeros_like(l_ref)
        acc_ref[...] = jnp.zeros_like(acc_ref)

    ngrp = k_ref.shape[2] // HEAD_DIM

    def step(diag):
        valid = None
        if mode == "win":
            rows = i * tq + lax.broadcasted_iota(jnp.int32, (tq, tk), 0)
            cols = kb * tk + lax.broadcasted_iota(jnp.int32, (tq, tk), 1)
            d = rows - cols
            valid = (d >= 0) & (d < NSA_WINDOW)
        elif diag:
            valid = (lax.broadcasted_iota(jnp.int32, (tq, tk), 0) >= lax.broadcasted_iota(jnp.int32, (tq, tk), 1))
        for gi in range(ngrp):
            k = k_ref[0, :, gi * HEAD_DIM:(gi + 1) * HEAD_DIM]
            if mode == "sel":
                penalty = ((sel_ref[0, gi].astype(F32) - 1.0) * (-NEG)).astype(BF16)
                k = jnp.concatenate([k, hot_ref[...]], axis=1)
            v_ones = jnp.concatenate([v_ref[0, :, gi * HEAD_DIM:(gi + 1) * HEAD_DIM],
                                      jnp.ones((tk, LANES), BF16)], axis=1)
            for hd in range(NSA_HPG):
                slot = gi * NSA_HPG + hd
                q = q_ref[0, :, slot * HEAD_DIM:(slot + 1) * HEAD_DIM]
                if mode == "sel":
                    q = jnp.concatenate([q, penalty], axis=1)
                s = _dot_nt(q, k)
                if valid is not None:
                    s = jnp.where(valid, s, NEG)
                m_old = m_ref[slot]
                m_new = jnp.maximum(m_old, jnp.max(s, axis=1, keepdims=True))
                alpha = jnp.exp2(m_old - m_new)
                p = jnp.exp2(s - jnp.concatenate([m_new] * (tk // LANES), axis=1))
                pv = jnp.dot(p.astype(BF16), v_ones, preferred_element_type=F32)
                l_ref[slot] = alpha * l_ref[slot] + pv[:, HEAD_DIM:]
                acc_ref[slot] = alpha * acc_ref[slot] + pv[:, 0:HEAD_DIM]
                m_ref[slot] = m_new

    if mode == "sel":
        pl.when(kb < i)(functools.partial(step, False))
        pl.when(kb == i)(functools.partial(step, True))
    else:
        step(True)

    @pl.when(kb == i)
    def _():
        for slot in range(ngrp * NSA_HPG):
            o_ref[0, :, slot * HEAD_DIM:(slot + 1) * HEAD_DIM] = (acc_ref[slot] / l_ref[slot]).astype(o_ref.dtype)


def _nsa_flash(proj_rope, proj_plain, sel, mode, tq, tk):
    b, s, _ = proj_rope.shape
    g = NSA_GROUPS
    gpb = g
    qw = gpb * NSA_HPG * HEAD_DIM
    kw = gpb * HEAD_DIM
    nq = s // tq
    assert tq == tk
    if mode == "sel":
        back = nq
        koff, voff = ROPE_KS, PL_VS
    else:
        assert NSA_WINDOW % tk == 0
        back = NSA_WINDOW // tk
        koff, voff = ROPE_KW, PL_VW
    assert g % gpb == 0 and ROPE_Q % qw == 0 and koff % kw == 0 and voff % kw == 0
    qi, kb, first = _causal_pairs(nq, back)
    in_specs = [
        pl.BlockSpec((1, tq, qw), lambda bi, gi, n, qi, kb, fs: (bi, qi[n], ROPE_Q // qw + gi)),
        pl.BlockSpec((1, tk, kw), lambda bi, gi, n, qi, kb, fs: (bi, kb[n], koff // kw + gi)),
        pl.BlockSpec((1, tk, kw), lambda bi, gi, n, qi, kb, fs: (bi, kb[n], voff // kw + gi)),
    ]
    args = [proj_rope, proj_rope, proj_plain]
    if mode == "sel":
        n_pad = sel.shape[3]
        in_specs.append(pl.BlockSpec((1, gpb, tq, n_pad), lambda bi, gi, n, qi, kb, fs: (bi, gi, qi[n], 0)))
        in_specs.append(pl.BlockSpec((tk, n_pad), lambda bi, gi, n, qi, kb, fs: (kb[n], 0)))
        block_of_key = jnp.arange(s, dtype=jnp.int32)[:, None] // NSA_SEL_BLOCK
        onehot = (block_of_key == jnp.arange(n_pad, dtype=jnp.int32)[None, :]).astype(BF16)
        args += [sel, onehot]
    return pl.pallas_call(
        functools.partial(_nsa_flash_kernel, mode=mode, tq=tq, tk=tk),
        out_shape=jax.ShapeDtypeStruct((b, s, NSA_WIDTH), BF16),
        grid_spec=pltpu.PrefetchScalarGridSpec(
            num_scalar_prefetch=3,
            grid=(b, g // gpb, qi.shape[0]),
            in_specs=in_specs,
            out_specs=pl.BlockSpec((1, tq, qw), lambda bi, gi, n, qi, kb, fs: (bi, qi[n], gi)),
            scratch_shapes=[
                pltpu.VMEM((gpb * NSA_HPG, tq, LANES), F32),
                pltpu.VMEM((gpb * NSA_HPG, tq, LANES), F32),
                pltpu.VMEM((gpb * NSA_HPG, tq, HEAD_DIM), F32),
            ],
        ),
        compiler_params=_cparams(("parallel", "parallel", "arbitrary")),
        name="nsa_flash_" + mode,
    )(qi, kb, first, *args)


def _causal_pairs(nq, back):
    qi, kb, first = [], [], []
    for i in range(nq):
        lo = max(0, i - back)
        for j in range(lo, i + 1):
            qi.append(i)
            kb.append(j)
            first.append(1 if j == lo else 0)
    as_i32 = lambda z: jnp.asarray(np.asarray(z, np.int32))
    return as_i32(qi), as_i32(kb), as_i32(first)


def _diff_flash_kernel(qi_ref, kb_ref, first_ref, q_ref, k_ref, v_ref, lam_ref, sub_ref, o_ref, m_ref, l_ref,
                       acc_ref, *, tq, tk, lam_init):
    n = pl.program_id(2)
    i = qi_ref[n]
    kb = kb_ref[n]

    @pl.when(first_ref[n] == 1)
    def _():
        m_ref[...] = jnp.full_like(m_ref, NEG)
        l_ref[...] = jnp.zeros_like(l_ref)
        acc_ref[...] = jnp.zeros_like(acc_ref)

    hpb = o_ref.shape[2] // DIFF_VDIM

    def step(masked):
        if masked:
            valid = (lax.broadcasted_iota(jnp.int32, (tq, tk), 0) >= lax.broadcasted_iota(jnp.int32, (tq, tk), 1))
        for hd in range(hpb):
            v = v_ref[0, :, hd * DIFF_VDIM:(hd + 1) * DIFF_VDIM]
            for mp in range(2):
                slot = 2 * hd + mp
                q = q_ref[0, :, slot * HEAD_DIM:(slot + 1) * HEAD_DIM]
                k = k_ref[0, :, slot * HEAD_DIM:(slot + 1) * HEAD_DIM]
                s = _dot_nt(q, k)
                if masked:
                    s = jnp.where(valid, s, NEG)
                m_old = m_ref[slot]
                m_new = jnp.maximum(m_old, jnp.max(s, axis=1, keepdims=True))
                alpha = jnp.exp2(m_old - m_new)
                p = jnp.exp2(s - jnp.concatenate([m_new] * (tk // LANES), axis=1))
                l_ref[slot] = alpha * l_ref[slot] + jnp.sum(p, axis=1, keepdims=True)
                acc_ref[slot] = (jnp.concatenate([alpha] * (DIFF_VDIM // LANES), axis=1) * acc_ref[slot]
                                 + _bdot(p, v))
                m_ref[slot] = m_new

    pl.when(kb < i)(functools.partial(step, False))
    pl.when(kb == i)(functools.partial(step, True))

    @pl.when(kb == i)
    def _():
        lam = lam_ref[...]
        lam_full = (jnp.exp(jnp.sum(lam[0:1, :] * lam[1:2, :], axis=1, keepdims=True))
                    - jnp.exp(jnp.sum(lam[2:3, :] * lam[3:4, :], axis=1, keepdims=True)) + lam_init)
        wide = lambda x: jnp.concatenate([x] * (DIFF_VDIM // LANES), axis=1)
        for hd in range(hpb):
            o = (acc_ref[2 * hd] / wide(l_ref[2 * hd])
                 - lam_full * (acc_ref[2 * hd + 1] / wide(l_ref[2 * hd + 1])))
            ms = jnp.mean(o * o, axis=-1, keepdims=True)
            o = o * lax.rsqrt(ms + 1e-5) * sub_ref[...]
            o_ref[0, :, hd * DIFF_VDIM:(hd + 1) * DIFF_VDIM] = (o * (1.0 - lam_init)).astype(o_ref.dtype)


def _diff_flash(proj_rope, proj_plain, lam, subln, layer, tq, tk):
    b, s, _ = proj_rope.shape
    assert tq == tk
    nq = s // tq
    lam_init = 0.8 - 0.6 * math.exp(-0.3 * layer)
    hpb = DIFF_HEADS
    w = hpb * DIFF_VDIM
    assert DIFF_HEADS % hpb == 0 and ROPE_DQ % w == 0 and ROPE_DK % w == 0 and PL_DV % w == 0
    qi, kb, first = _causal_pairs(nq, nq)
    return pl.pallas_call(
        functools.partial(_diff_flash_kernel, tq=tq, tk=tk, lam_init=lam_init),
        out_shape=jax.ShapeDtypeStruct((b, s, DIFF_WIDTH), BF16),
        grid_spec=pltpu.PrefetchScalarGridSpec(
            num_scalar_prefetch=3,
            grid=(b, DIFF_HEADS // hpb, qi.shape[0]),
            in_specs=[
                pl.BlockSpec((1, tq, w), lambda bi, h, n, qi, kb, fs: (bi, qi[n], ROPE_DQ // w + h)),
                pl.BlockSpec((1, tk, w), lambda bi, h, n, qi, kb, fs: (bi, kb[n], ROPE_DK // w + h)),
                pl.BlockSpec((1, tk, w), lambda bi, h, n, qi, kb, fs: (bi, kb[n], PL_DV // w + h)),
                pl.BlockSpec(lam.shape, lambda bi, h, n, qi, kb, fs: (0, 0)),
                pl.BlockSpec((1, DIFF_VDIM), lambda bi, h, n, qi, kb, fs: (0, 0)),
            ],
            out_specs=pl.BlockSpec((1, tq, w), lambda bi, h, n, qi, kb, fs: (bi, qi[n], h)),
            scratch_shapes=[
                pltpu.VMEM((2 * hpb, tq, LANES), F32),
                pltpu.VMEM((2 * hpb, tq, LANES), F32),
                pltpu.VMEM((2 * hpb, tq, DIFF_VDIM), F32),
            ],
        ),
        compiler_params=_cparams(("parallel", "parallel", "arbitrary")),
        name="diff_flash",
    )(qi, kb, first, proj_rope, proj_rope, proj_plain, lam, subln[None, :])


def _bmm(a, b):
    return jnp.einsum("umk,ukn->umn", a.astype(BF16), b.astype(BF16), preferred_element_type=F32)


def _bmm_nt(a, b):
    return jnp.einsum("umk,unk->umn", a.astype(BF16), b.astype(BF16), preferred_element_type=F32)


def _rwkv_scan_kernel(r_ref, k_ref, v_ref, gd_ref, wa_ref, mur_ref, muk_ref, muv_ref, mugd_ref, muwa_ref,
                      wdec_ref, wrate_ref, gup_ref, w0_ref, a0_ref, kk_ref, ka_ref, rk_ref, lnw_ref, lnb_ref,
                      o_ref, st_ref, last_ref, lastgd_ref, *, ts, chunk):
    @pl.when(pl.program_id(2) == 0)
    def _():
        st_ref[...] = jnp.zeros_like(st_ref)
        last_ref[...] = jnp.zeros_like(last_ref)
        lastgd_ref[...] = jnp.zeros_like(lastgd_ref)

    def shifted(x_ref, mu_ref, carry):
        x = x_ref[0]
        row = lax.broadcasted_iota(jnp.int32, x.shape, 0)
        prev = jnp.where(row == 0, carry, pltpu.roll(x, 1, 0))
        return x + (prev - x) * mu_ref[...]

    r = shifted(r_ref, mur_ref, last_ref[0, 0:1, :])
    k = shifted(k_ref, muk_ref, last_ref[1, 0:1, :])
    v = shifted(v_ref, muv_ref, last_ref[2, 0:1, :])
    wa = shifted(wa_ref, muwa_ref, last_ref[3, 0:1, :])
    gd = shifted(gd_ref, mugd_ref, lastgd_ref[0:1, :])
    for n, ref in enumerate((r_ref, k_ref, v_ref, wa_ref)):
        last_ref[n, 0:1, :] = ref[0, ts - 1:ts, :]
    lastgd_ref[0:1, :] = gd_ref[0, ts - 1:ts, :]

    wa = jnp.where(lax.broadcasted_iota(jnp.int32, wa.shape, 1) < RWKV_W_LORA, jnp.tanh(wa), wa)
    lw = -math.exp(-0.5) * jax.nn.sigmoid(w0_ref[...] + _bdot(wa, wdec_ref[...]))
    a = jax.nn.sigmoid(a0_ref[...] + _bdot(wa, wrate_ref[...]))
    g = _bdot(jax.nn.sigmoid(gd), gup_ref[...])

    c = chunk
    nc = ts // c
    head0 =lax.broadcasted_iota(jnp.int32, (ts, LANES), 1) < RWKV_HEAD

    def seg_sum(x):
        s0 = jnp.sum(jnp.where(head0, x, 0.0), axis=1, keepdims=True)
        s1 = jnp.sum(jnp.where(head0, 0.0, x), axis=1, keepdims=True)
        return jnp.where(head0, s0, s1)

    kk = k * kk_ref[...]
    kap = kk / jnp.maximum(jnp.sqrt(seg_sum(kk * kk)), 1e-12)
    kmod = k * (1.0 + (a - 1.0) * ka_ref[...])
    bvec = kap * a

    ri = lax.broadcasted_iota(jnp.int32, (ts, ts), 0)
    ci = lax.broadcasted_iota(jnp.int32, (ts, ts), 1)
    same_chunk = jnp.right_shift(ri, c.bit_length() - 1) == jnp.right_shift(ci, c.bit_length() - 1)
    tri = jnp.where(same_chunk & (ri >= ci), 1.0, 0.0).astype(BF16)
    cum = _dot_exact_lhs(tri, lw)
    g_incl = jnp.exp(cum)
    g_inv = jnp.exp(-cum)
    split = lambda x: x.reshape(nc, c, LANES)
    g_last = [g_incl[(ch + 1) * c - 1:(ch + 1) * c, :] for ch in range(nc)]
    g_last_rows = jnp.concatenate([jnp.broadcast_to(gl, (c, LANES)) for gl in g_last], axis=0)
    kh = split(kmod * g_inv)
    bh = split(bvec * g_inv)
    kg = split(kmod * g_inv * g_last_rows)
    bg = split(bvec * g_inv * g_last_rows)
    kaph = kap * jnp.exp(cum - lw)
    rh = r * g_incl
    v3 = split(v)

    assert 2 * c == LANES
    h0 = lax.broadcasted_iota(jnp.int32, (nc, c, LANES), 2) < RWKV_HEAD
    stack = lambda y: jnp.concatenate([jnp.where(h0, y, 0.0), jnp.where(h0, 0.0, y)], axis=1)
    kap3 = split(kaph)
    r3 = split(rh)
    gram = _bmm_nt(jnp.concatenate([kap3, r3], axis=1),
                   jnp.concatenate([stack(bh), stack(kh)], axis=1))
    row = lax.broadcasted_iota(jnp.int32, (c, 4 * c), 0)
    col = lax.broadcasted_iota(jnp.int32, (c, 4 * c), 1) & (c - 1)
    top = jnp.where(row > col, gram[:, 0:c, :], 0.0)
    abk = jnp.where(row >= col, gram[:, c:2 * c, :], 0.0)
    lb = top[:, :, 0:LANES]
    lk = top[:, :, LANES:2 * LANES]
    tm = -lb
    q = _bmm(lb, stack(lb))
    n = 2
    while True:
        tm = tm + q + _bmm(tm, stack(q))
        n *= 2
        if n >= c:
            break
        q = _bmm(q, stack(q))
    kap_t = kap3 + _bmm(tm, stack(kap3))
    lkv = _bmm(lk, stack(v3))
    z0 = -(lkv + _bmm(tm, stack(lkv)))
    r_t = r3 - _bmm(abk[:, :, 0:LANES], stack(kap_t))
    y0 = _bmm(abk, jnp.concatenate([stack(z0), stack(v3)], axis=1))

    br = lax.broadcasted_iota(jnp.int32, (LANES, LANES), 0)
    bc = lax.broadcasted_iota(jnp.int32, (LANES, LANES), 1)
    blockdiag = (br < RWKV_HEAD) == (bc < RWKV_HEAD)
    state = st_ref[...]
    ys = []
    for ch in range(nc):
        trans = jnp.where(blockdiag, -_dot_tn(bg[ch], kap_t[ch]), 0.0)
        trans = trans + jnp.where(br == bc, jnp.broadcast_to(g_last[ch], (LANES, LANES)), 0.0)
        inject = jnp.where(blockdiag, _dot_tn(jnp.concatenate([bg[ch], kg[ch]], axis=0),
                                              jnp.concatenate([z0[ch], v3[ch]], axis=0)), 0.0)
        ys.append(_bdot(r_t[ch], state) + y0[ch])
        state = _dot3(trans, state) + inject
    st_ref[...] = state
    y = jnp.concatenate(ys, axis=0)

    mean = seg_sum(y) * (1.0 / RWKV_HEAD)
    yc = y - mean
    var = seg_sum(yc * yc) * (1.0 / RWKV_HEAD)
    yn = yc * lax.rsqrt(var + RWKV_LNX_EPS) * lnw_ref[...] + lnb_ref[...]
    bonus = seg_sum(r * kmod * rk_ref[...]) * v
    o_ref[0] = ((yn + bonus) * g).astype(o_ref.dtype)


def _rwkv_scan(proj3, mu_p, wwa, gup, w0, a0, k_k, k_a, r_k, lnx_w, lnx_b, ts=512, chunk=64):
    b, s, _ = proj3.shape
    w = RWKV_WIDTH
    ts = min(ts, s)
    assert s % ts == 0 and ts % chunk == 0
    npair = w // LANES
    col = lambda off: pl.BlockSpec((1, ts, LANES), lambda bi, j, t: (bi, t, off // LANES + j))
    mucol = lambda off: pl.BlockSpec((1, LANES), lambda bi, j, t: (0, off // LANES + j))
    vec = pl.BlockSpec((1, LANES), lambda bi, j, t: (0, j))
    in_specs = [
        col(0), col(w), col(2 * w),
        pl.BlockSpec((1, ts, RW_G_PAD), lambda bi, j, t: (bi, t, RW_GD // RW_G_PAD)),
        pl.BlockSpec((1, ts, RW_LORA_PAD), lambda bi, j, t: (bi, t, RW_WA // RW_LORA_PAD)),
        mucol(0), mucol(w), mucol(2 * w),
        pl.BlockSpec((1, RW_G_PAD), lambda bi, j, t: (0, RW_GD // RW_G_PAD)),
        pl.BlockSpec((1, RW_LORA_PAD), lambda bi, j, t: (0, RW_WA // RW_LORA_PAD)),
        pl.BlockSpec((RW_LORA_PAD, LANES), lambda bi, j, t: (0, j)),
        pl.BlockSpec((RW_LORA_PAD, LANES), lambda bi, j, t: (0, npair + j)),
        pl.BlockSpec((RW_G_PAD, LANES), lambda bi, j, t: (0, j)),
    ] + [vec] * 7
    row = lambda z: z[None, :]
    return pl.pallas_call(
        functools.partial(_rwkv_scan_kernel, ts=ts, chunk=chunk),
        out_shape=jax.ShapeDtypeStruct((b, s, w), BF16),
        grid=(b, npair, s // ts),
        in_specs=in_specs,
        out_specs=pl.BlockSpec((1, ts, LANES), lambda bi, j, t: (bi, t, j)),
        scratch_shapes=[
            pltpu.VMEM((LANES, LANES), F32),
            pltpu.VMEM((4, SUBLANES, LANES), F32),
            pltpu.VMEM((SUBLANES, RW_G_PAD), F32),
        ],
        compiler_params=_cparams(("parallel", "parallel", "arbitrary")),
        name="rwkv_scan",
    )(proj3, proj3, proj3, proj3, proj3, mu_p, mu_p, mu_p, mu_p, mu_p, wwa, wwa, gup,
      row(w0), row(a0), row(k_k), row(k_a), row(r_k), row(lnx_w), row(lnx_b))


def _merge_kernel(oc_ref, os_ref, ow_ref, ng_ref, yd_ref, yr_ref, wb_ref, g0_ref, g1_ref, g2_ref, bias_ref,
                  o_ref, yn_ref):
    @pl.when(pl.program_id(1) == 0)
    def _():
        gates = jax.nn.sigmoid(ng_ref[...])
        for hd in range(NSA_HEADS):
            sl = slice(hd * HEAD_DIM, (hd + 1) * HEAD_DIM)
            o = (gates[:, 3 * hd:3 * hd + 1] * oc_ref[:, sl]
                 + gates[:, 3 * hd + 1:3 * hd + 2] * os_ref[:, sl]
                 + gates[:, 3 * hd + 2:3 * hd + 3] * ow_ref[:, sl])
            yn_ref[:, sl] = o.astype(BF16)

    branches = (yn_ref, yd_ref, yr_ref)
    graw = (g0_ref, g1_ref, g2_ref)
    acc = None
    for bi in range(N_BRANCH):
        gate = jax.nn.sigmoid(graw[bi][...] + bias_ref[bi])
        term = gate * jnp.dot(branches[bi][...], wb_ref[bi], preferred_element_type=F32)
        acc = term if acc is None else acc + term
    o_ref[...] = acc.astype(o_ref.dtype)


def _merge(o_cmp, o_slc, o_win, proj_f32, proj_plain, y_diff, y_rwkv, w_branch, b_gate, tm=1024, tn=512):
    m = o_cmp.shape[0]
    n = w_branch.shape[2]
    tm = min(tm, m)
    assert m % tm == 0 and n % tn == 0 and PL_BG % tn == 0
    bw = BRANCH_WIDTH
    rowblk = lambda: pl.BlockSpec((tm, bw), lambda i, j: (i, 0))
    gate_spec = lambda bi: pl.BlockSpec((tm, tn), lambda i, j: (i, PL_BG // tn + bi * (n // tn) + j))
    return pl.pallas_call(
        _merge_kernel,
        out_shape=jax.ShapeDtypeStruct((m, n), BF16),
        grid=(m // tm, n // tn),
        in_specs=[
            rowblk(), rowblk(), rowblk(),
            pl.BlockSpec((tm, LANES), lambda i, j: (i, F_NG // LANES)),
            rowblk(), rowblk(),
            pl.BlockSpec((N_BRANCH, bw, tn), lambda i, j: (0, 0, j)),
            gate_spec(0), gate_spec(1), gate_spec(2),
            pl.BlockSpec((N_BRANCH, 1, tn), lambda i, j: (0, 0, j)),
        ],
        out_specs=pl.BlockSpec((tm, tn), lambda i, j: (i, j)),
        scratch_shapes=[pltpu.VMEM((tm, bw), BF16)],
        compiler_params=_cparams(("parallel", "arbitrary")),
        name="gated_merge",
    )(o_cmp, o_slc, o_win, proj_f32, y_diff, y_rwkv, w_branch, proj_plain, proj_plain, proj_plain,
      b_gate[:, None, :])


def _pack_layer_weights(w_in, w_gate, rwkv_mu, w_up_lora, a_up_lora, g_up_lora):
    d = w_in.shape[0]
    nsa_sizes = (NSA_WIDTH,) + (NSA_KV,) * 6 + (3 * NSA_HEADS,)
    diff_sizes = (DIFF_QK, DIFF_QK, DIFF_WIDTH)
    rw_sizes = (RWKV_WIDTH,) * 3 + (RWKV_W_LORA, RWKV_A_LORA, RWKV_G_LORA)
    offs = np.cumsum((0,) + nsa_sizes + diff_sizes + rw_sizes)
    seg = [w_in[:, offs[i]:offs[i + 1]] for i in range(len(offs) - 1)]
    q, kc, vc, ks, vs, kw, vw, ng, dq, dk, dv, rr, rk, rv, wd, ad, gd = seg
    w_rope = jnp.concatenate([q, dq, dk, ks, kw], axis=1).astype(BF16)
    w_plain = jnp.concatenate([kc, vc, vs, vw, dv] + [w_gate[bi] for bi in range(N_BRANCH)], axis=1).astype(BF16)
    zpad = lambda n: jnp.zeros((d, n), w_in.dtype)
    w_f32 = jnp.concatenate(
        [rr, rk, rv, gd, zpad(RW_G_PAD - RWKV_G_LORA), wd, ad, ng, zpad(LANES - 3 * NSA_HEADS)],
        axis=1).astype(BF16)
    mu_rkv, mu_wa, mu_gd = jnp.split(rwkv_mu, [3 * RWKV_WIDTH, 3 * RWKV_WIDTH + RW_LORA_PAD])
    mu_p = jnp.concatenate([mu_rkv, mu_gd, jnp.zeros((RW_G_PAD - RWKV_G_LORA,), rwkv_mu.dtype), mu_wa])[None, :]
    zz = jnp.zeros((RWKV_W_LORA, RWKV_WIDTH), w_up_lora.dtype)
    wwa = jnp.concatenate([jnp.concatenate([w_up_lora, zz], axis=1),
                           jnp.concatenate([zz, a_up_lora], axis=1)], axis=0).astype(BF16)
    gup = jnp.concatenate([g_up_lora, jnp.zeros((RW_G_PAD - RWKV_G_LORA, RWKV_WIDTH), g_up_lora.dtype)],
                          axis=0).astype(BF16)
    return w_rope, w_plain, w_f32, mu_p, wwa, gup


def _overlap_matrix(n_cp, n_sel):
    c_start = np.arange(n_cp) * NSA_CMP_STRIDE
    c_end = c_start + NSA_CMP_LEN - 1
    j_start = np.arange(n_sel) * NSA_SEL_BLOCK
    ov = (c_start[:, None] <= j_start[None, :] + NSA_SEL_BLOCK - 1) & (c_end[:, None] >= j_start[None, :])
    ov = np.pad(ov.astype(np.float32), ((0, 0), (0, -n_sel % LANES)))
    return jnp.asarray(ov).astype(BF16)


def _chunk_rows(z, groups):
    b, s, _ = z.shape
    z = z.reshape(b, s // NSA_CMP_STRIDE, NSA_CMP_STRIDE, groups, HEAD_DIM)
    return z.transpose(0, 3, 1, 2, 4).reshape(b, groups, s // NSA_CMP_STRIDE, NSA_CMP_STRIDE * HEAD_DIM)


def _nsa_branches(proj_rope, proj_plain, positions, cmp_pos, cmp_w1, cmp_w2, tq):
    b, s, _ = proj_rope.shape
    n_chunk = s // NSA_CMP_STRIDE
    n_sel = s // NSA_SEL_BLOCK
    kc2 = _chunk_rows(proj_plain[:, :, PL_KC:PL_KC + NSA_KV], NSA_GROUPS)
    vc2 = _chunk_rows(proj_plain[:, :, PL_VC:PL_VC + NSA_KV], NSA_GROUPS)
    cmp_end = np.minimum(np.arange(n_chunk) * NSA_CMP_STRIDE + NSA_CMP_LEN - 1, s - 1)
    pos_c = jnp.take(positions, jnp.asarray(cmp_end), axis=1)
    cosc, sinc = _rope_tables(pos_c.reshape(-1))
    cosc = cosc.reshape(b, n_chunk, LANES)
    sinc = sinc.reshape(b, n_chunk, LANES)
    w1 = cmp_w1.astype(BF16)
    w2 = cmp_w2.astype(BF16)
    pe = cmp_pos.reshape(2, 1, NSA_CMP_LEN * HEAD_DIM)
    kcmp = _nsa_compress(kc2, w1[0], w2[0], pe[0], cosc, sinc, True)
    vcmp = _nsa_compress(vc2, w1[1], w2[1], pe[1], cosc, sinc, False)
    o_cmp, sel = _nsa_cmp(proj_rope, kcmp, vcmp, _overlap_matrix(n_chunk, n_sel), n_sel, tq)
    o_slc = _nsa_flash(proj_rope, proj_plain, sel, "sel", tq, tq)
    o_win = _nsa_flash(proj_rope, proj_plain, None, "win", tq, tq)
    return o_cmp, o_slc, o_win


def _layer(x2, b, s, layer, cosf, sinf, positions, p):
    m = b * s
    w_rope, w_plain, w_f32, mu_p, wwa, gup = _pack_layer_weights(
        p["w_in"], p["w_gate"], p["rwkv_mu"], p["rwkv_w_up"], p["rwkv_a_up"], p["rwkv_g_up"])
    scale = HEAD_DIM ** -0.5 * math.log2(math.e)
    colscale = jnp.concatenate([jnp.full((ROPE_SCALED,), scale, F32),
                                jnp.ones((ROPE_COLS - ROPE_SCALED,), F32)])[None, :]
    g_pre = p["norm_pre_mix"]
    proj_rope = _norm_matmul(x2, g_pre, w_rope, "rope", BF16, rope=(cosf, sinf, colscale), tn=ROPE_COLS // 4)
    proj_plain = _norm_matmul(x2, g_pre, w_plain, "plain", BF16, tn=1024)
    proj_f32 = _norm_matmul(x2, g_pre, w_f32, "plain", F32, tn=F32_COLS // 4)
    pr3 = proj_rope.reshape(b, s, ROPE_COLS)
    pp3 = proj_plain.reshape(b, s, PLAIN_COLS)

    tq = min(512, s)
    o_cmp, o_slc, o_win = _nsa_branches(pr3, pp3, positions, p["nsa_cmp_pos"], p["nsa_cmp_w1"], p["nsa_cmp_w2"], tq)
    y_diff = _diff_flash(pr3, pp3, p["diff_lambda"], p["diff_subln"], layer, tq, tq)

    y_rwkv = _rwkv_scan(proj_f32.reshape(b, s, F32_COLS), mu_p, wwa, gup, p["rwkv_w0"], p["rwkv_a0"],
                        p["rwkv_k_k"], p["rwkv_k_a"], p["rwkv_r_k"].reshape(-1), p["rwkv_lnx_w"], p["rwkv_lnx_b"])

    merged = _merge(o_cmp.reshape(m, NSA_WIDTH), o_slc.reshape(m, NSA_WIDTH), o_win.reshape(m, NSA_WIDTH),
                    proj_f32, proj_plain, y_diff.reshape(m, DIFF_WIDTH), y_rwkv.reshape(m, RWKV_WIDTH),
                    p["w_branch"].astype(BF16), p["b_gate"])
    x2 = _matmul_norm_res(merged, p["w_out"].astype(BF16), p["norm_post_mix"], x2)
    hidden = _norm_matmul(x2, p["norm_pre_mlp"], p["w_up"].astype(BF16), "relu2", BF16, tn=1024)
    x2 = _matmul_norm_res(hidden, p["w_down"].astype(BF16), p["norm_post_mlp"], x2)
    return x2


def kernel(x, positions, norm_pre_mix, norm_post_mix, norm_pre_mlp, norm_post_mlp, w_in, nsa_cmp_pos, nsa_cmp_w1, nsa_cmp_w2, diff_lambda, diff_subln, rwkv_mu, rwkv_w0, rwkv_w_up, rwkv_a0, rwkv_a_up, rwkv_g_up, rwkv_k_k, rwkv_k_a, rwkv_r_k, rwkv_lnx_w, rwkv_lnx_b, w_gate, b_gate, w_branch, w_out, w_up, w_down):
    b, s, d = x.shape
    depth = w_in.shape[0]
    stacked = dict(
        norm_pre_mix=norm_pre_mix, norm_post_mix=norm_post_mix, norm_pre_mlp=norm_pre_mlp,
        norm_post_mlp=norm_post_mlp, w_in=w_in, nsa_cmp_pos=nsa_cmp_pos, nsa_cmp_w1=nsa_cmp_w1,
        nsa_cmp_w2=nsa_cmp_w2, diff_lambda=diff_lambda, diff_subln=diff_subln, rwkv_mu=rwkv_mu,
        rwkv_w0=rwkv_w0, rwkv_w_up=rwkv_w_up, rwkv_a0=rwkv_a0, rwkv_a_up=rwkv_a_up, rwkv_g_up=rwkv_g_up,
        rwkv_k_k=rwkv_k_k, rwkv_k_a=rwkv_k_a, rwkv_r_k=rwkv_r_k, rwkv_lnx_w=rwkv_lnx_w,
        rwkv_lnx_b=rwkv_lnx_b, w_gate=w_gate, b_gate=b_gate, w_branch=w_branch, w_out=w_out,
        w_up=w_up, w_down=w_down)
    cosf, sinf = _rope_tables(positions.reshape(-1))
    x2 = x.reshape(b * s, d)
    for layer in range(depth):
        x2 = _layer(x2, b, s, layer, cosf, sinf, positions, {n: a[layer] for n, a in stacked.items()})
    return x2.reshape(b, s, d)
```

```python
import functools
import math

import jax
import jax.numpy as jnp
import numpy as np
from jax import lax
from jax.experimental import pallas as pl
from jax.experimental.pallas import tpu as pltpu

F32 = jnp.float32
BF16 = jnp.bfloat16

D_MODEL = 2048
RMS_EPS = 1e-6
ROPE_THETA = 500000.0
HEAD_DIM = 128
ROT_HALF = HEAD_DIM // 8
NSA_HEADS = 8
NSA_GROUPS = 2
NSA_HPG = NSA_HEADS // NSA_GROUPS
NSA_CMP_LEN = 32
NSA_CMP_STRIDE = 16
NSA_CMP_HIDDEN = 256
NSA_SEL_BLOCK = 64
SEL_SHIFT = NSA_SEL_BLOCK.bit_length() - 1
NSA_SEL_TOPK = 16
NSA_WINDOW = 512
NSA_WIDTH = NSA_HEADS * HEAD_DIM
NSA_KV = NSA_GROUPS * HEAD_DIM
DIFF_HEADS = 4
DIFF_VDIM = 2 * HEAD_DIM
DIFF_WIDTH = DIFF_HEADS * DIFF_VDIM
DIFF_QK = 2 * DIFF_HEADS * HEAD_DIM
RWKV_HEAD = 64
RWKV_WIDTH = 1024
RWKV_W_LORA = 64
RWKV_A_LORA = 64
RWKV_G_LORA = 160
RWKV_LNX_EPS = 64e-5
N_BRANCH = 3
BRANCH_WIDTH = 1024

LANES = 128
SUBLANES = 8
VMEM_LIMIT_BYTES = 56 * 1024 * 1024

NEG = -1e30

ROPE_Q, ROPE_DQ, ROPE_DK, ROPE_KS, ROPE_KW = 0, 1024, 2048, 3072, 3328
ROPE_SCALED = ROPE_DK
ROPE_COLS = 3584
PL_KC, PL_VC, PL_VS, PL_VW, PL_DV, PL_BG = 0, 256, 512, 768, 1024, 2048
PLAIN_COLS = PL_BG + N_BRANCH * D_MODEL
RW_LORA_PAD = 128
RW_G_PAD = 256
RW_GD = 3 * RWKV_WIDTH
RW_WA = RW_GD + RW_G_PAD
RW_COLS = RW_WA + RW_LORA_PAD
F_NG = RW_COLS
F32_COLS = RW_COLS + LANES


def _cparams(sem):
    return pltpu.CompilerParams(dimension_semantics=sem, vmem_limit_bytes=VMEM_LIMIT_BYTES)


def _bdot(a, b):
    return jnp.dot(a.astype(BF16), b.astype(BF16), preferred_element_type=F32)


def _dot_nt(a, b):
    return lax.dot_general(a.astype(BF16), b.astype(BF16), (((1,), (1,)), ((), ())),
                           preferred_element_type=F32)


def _dot_tn(a, b):
    return lax.dot_general(a.astype(BF16), b.astype(BF16), (((0,), (0,)), ((), ())),
                           preferred_element_type=F32)


def _split2(x):
    hi = x.astype(BF16)
    lo = (x - hi.astype(F32)).astype(BF16)
    return hi, lo


def _split3(x):
    hi = x.astype(BF16)
    r1 = x - hi.astype(F32)
    mid = r1.astype(BF16)
    lo = (r1 - mid.astype(F32)).astype(BF16)
    return hi, mid, lo


def _dot_exact_lhs(a_bf, x):
    hi, mid, lo = _split3(x)
    d = lambda p: jnp.dot(a_bf, p, preferred_element_type=F32)
    return d(hi) + (d(mid) + d(lo))


def _dot_exact_rhs(x, b_bf):
    hi, mid, lo = _split3(x)
    d = lambda p: jnp.dot(p, b_bf, preferred_element_type=F32)
    return d(hi) + (d(mid) + d(lo))


def _dot3(a, b):
    ah, al = _split2(a)
    bh, bl = _split2(b)
    d = lambda p, q: jnp.dot(p, q, preferred_element_type=F32)
    return d(ah, bh) + (d(ah, bl) + d(al, bh))


def _rope_partner(z, lane):
    return jnp.where(lane < ROT_HALF, pltpu.roll(z, LANES - ROT_HALF, 1), pltpu.roll(z, ROT_HALF, 1))


def _rope_table_kernel(pos_ref, invf_ref, sign_ref, cos_ref, sin_ref):
    ang = pos_ref[...].astype(F32) * invf_ref[...]
    cos_ref[...] = jnp.cos(ang)
    sin_ref[...] = jnp.sin(ang) * sign_ref[...]


def _rope_tables(pos_flat):
    n = pos_flat.shape[0]
    half = ROT_HALF
    inv_freq = ROPE_THETA ** (-jnp.arange(half, dtype=F32) / half)
    zeros = jnp.zeros((LANES - 2 * half,), F32)
    invf = jnp.concatenate([inv_freq, inv_freq, zeros])[None, :]
    sign = jnp.concatenate([-jnp.ones((half,), F32), jnp.ones((half,), F32), zeros])[None, :]
    tm = min(n, 2048)
    assert n % tm == 0
    vec = pl.BlockSpec((1, LANES), lambda i: (0, 0))
    out = pl.BlockSpec((tm, LANES), lambda i: (i, 0))
    return pl.pallas_call(
        _rope_table_kernel,
        out_shape=(jax.ShapeDtypeStruct((n, LANES), F32),) * 2,
        grid=(n // tm,),
        in_specs=[pl.BlockSpec((tm, 1), lambda i: (i, 0)), vec, vec],
        out_specs=(out, out),
        compiler_params=_cparams(("parallel",)),
        name="rope_tables",
    )(pos_flat[:, None], invf, sign)


def _norm_matmul_kernel(x_ref, g_ref, w_ref, *rest, mode, tn):
    if mode == "rope":
        cos_ref, sin_ref, cs_ref, o_ref, u_ref = rest
    else:
        o_ref, u_ref = rest

    @pl.when(pl.program_id(1) == 0)
    def _():
        x = x_ref[...]
        ms = jnp.mean(x * x, axis=-1, keepdims=True)
        u_ref[...] = (x * lax.rsqrt(ms + RMS_EPS) * g_ref[...]).astype(BF16)

    acc = jnp.dot(u_ref[...], w_ref[...], preferred_element_type=F32)
    if mode == "rope":
        acc = acc * cs_ref[...]
        cosv = cos_ref[...]
        sinv = sin_ref[...]
        lane = lax.broadcasted_iota(jnp.int32, cosv.shape, 1)
        for h in range(tn // LANES):
            z = acc[:, h * LANES:(h + 1) * LANES]
            o_ref[:, h * LANES:(h + 1) * LANES] = (z * cosv + _rope_partner(z, lane) * sinv).astype(o_ref.dtype)
    elif mode == "relu2":
        o_ref[...] = jnp.square(jnp.maximum(acc, 0.0)).astype(o_ref.dtype)
    else:
        o_ref[...] = acc.astype(o_ref.dtype)


def _norm_matmul(x, g, w, mode, out_dtype, rope=None, tm=1024, tn=512):
    m, d = x.shape
    n = w.shape[1]
    tm = min(tm, m)
    assert m % tm == 0 and n % tn == 0
    in_specs = [
        pl.BlockSpec((tm, d), lambda i, j: (i, 0)),
        pl.BlockSpec((1, d), lambda i, j: (0, 0)),
        pl.BlockSpec((d, tn), lambda i, j: (0, j)),
    ]
    args = [x, g[None, :], w]
    if mode == "rope":
        cosf, sinf, colscale = rope
        in_specs += [
            pl.BlockSpec((tm, LANES), lambda i, j: (i, 0)),
            pl.BlockSpec((tm, LANES), lambda i, j: (i, 0)),
            pl.BlockSpec((1, tn), lambda i, j: (0, j)),
        ]
        args += [cosf, sinf, colscale]
    return pl.pallas_call(
        functools.partial(_norm_matmul_kernel, mode=mode, tn=tn),
        out_shape=jax.ShapeDtypeStruct((m, n), out_dtype),
        grid=(m // tm, n // tn),
        in_specs=in_specs,
        out_specs=pl.BlockSpec((tm, tn), lambda i, j: (i, j)),
        scratch_shapes=[pltpu.VMEM((tm, d), BF16)],
        compiler_params=_cparams(("parallel", "arbitrary")),
        name="norm_matmul_" + mode,
    )(*args)


def _matmul_norm_res_kernel(a_ref, w_ref, g_ref, res_ref, o_ref, acc_ref):
    k = pl.program_id(1)

    @pl.when(k == 0)
    def _():
        acc_ref[...] = jnp.zeros_like(acc_ref)

    acc_ref[...] += jnp.dot(a_ref[...], w_ref[...], preferred_element_type=F32)

    @pl.when(k == pl.num_programs(1) - 1)
    def _():
        y = acc_ref[...]
        ms = jnp.mean(y * y, axis=-1, keepdims=True)
        o_ref[...] = res_ref[...] + y * lax.rsqrt(ms + RMS_EPS) * g_ref[...]


def _matmul_norm_res(a, w, g, res, tm=512, tk=2048):
    m, kdim = a.shape
    n = w.shape[1]
    tm = min(tm, m)
    assert m % tm == 0 and kdim % tk == 0
    return pl.pallas_call(
        _matmul_norm_res_kernel,
        out_shape=jax.ShapeDtypeStruct((m, n), F32),
        grid=(m // tm, kdim // tk),
        in_specs=[
            pl.BlockSpec((tm, tk), lambda i, k: (i, k)),
            pl.BlockSpec((tk, n), lambda i, k: (k, 0)),
            pl.BlockSpec((1, n), lambda i, k: (0, 0)),
            pl.BlockSpec((tm, n), lambda i, k: (i, 0)),
        ],
        out_specs=pl.BlockSpec((tm, n), lambda i, k: (i, 0)),
        scratch_shapes=[pltpu.VMEM((tm, n), F32)],
        compiler_params=_cparams(("parallel", "arbitrary")),
        name="matmul_norm_res",
    )(a, w, g[None, :], res)


def _gelu_tanh(x):
    return 0.5 * x * (1.0 + jnp.tanh(math.sqrt(2.0 / math.pi) * (x + 0.044715 * (x * x * x))))


def _nsa_compress_kernel(x_ref, w1_ref, w2_ref, pe_ref, cos_ref, sin_ref, o_ref, *, use_rope):
    x = x_ref[0, 0]
    half = x.shape[1]
    n_chunk = x.shape[0]
    a = jnp.dot(x, w1_ref[0:half, :], preferred_element_type=F32)
    b = jnp.dot(x, w1_ref[half:2 * half, :], preferred_element_type=F32)
    pe = jnp.broadcast_to(pe_ref[...], (SUBLANES, pe_ref.shape[1]))
    peb = _dot_exact_rhs_general(pe, w1_ref[...])[0:1, :]
    h = a + pltpu.roll(b, n_chunk - 1, 0) + peb
    y = _bdot(_gelu_tanh(h), w2_ref[...])
    if use_rope:
        lane = lax.broadcasted_iota(jnp.int32, y.shape, 1)
        y = y * cos_ref[0] + _rope_partner(y, lane) * sin_ref[0]
    o_ref[0, 0] = y.astype(o_ref.dtype)


def _dot_exact_rhs_general(x, w_bf):
    hi, mid, lo = _split3(x)
    d = lambda p: jnp.dot(p, w_bf, preferred_element_type=F32)
    return d(hi) + (d(mid) + d(lo))


def _nsa_compress(x2, w1, w2, pe_flat, cosc, sinc, use_rope):
    b, g, n_chunk, width = x2.shape
    dh = w2.shape[1]
    return pl.pallas_call(
        functools.partial(_nsa_compress_kernel, use_rope=use_rope),
        out_shape=jax.ShapeDtypeStruct((b, g, n_chunk, dh), BF16),
        grid=(b, g),
        in_specs=[
            pl.BlockSpec((1, 1, n_chunk, width), lambda i, j: (i, j, 0, 0)),
            pl.BlockSpec(w1.shape, lambda i, j: (0, 0)),
            pl.BlockSpec(w2.shape, lambda i, j: (0, 0)),
            pl.BlockSpec(pe_flat.shape, lambda i, j: (0, 0)),
            pl.BlockSpec((1, n_chunk, dh), lambda i, j: (i, 0, 0)),
            pl.BlockSpec((1, n_chunk, dh), lambda i, j: (i, 0, 0)),
        ],
        out_specs=pl.BlockSpec((1, 1, n_chunk, dh), lambda i, j: (i, j, 0, 0)),
        compiler_params=_cparams(("parallel", "parallel")),
        name="nsa_compress",
    )(x2, w1, w2, pe_flat, cosc, sinc)


def _nsa_cmp_kernel(q_ref, kc_ref, vc_ref, ov_ref, o_ref, sel_ref, *, tq, top, n_sel):
    i = pl.program_id(2)
    kc = kc_ref[0, 0]
    vc = vc_ref[0, 0]
    ncp = kc.shape[0]
    t = i * tq + lax.broadcasted_iota(jnp.int32, (tq, ncp), 0)
    c = lax.broadcasted_iota(jnp.int32, (tq, ncp), 1)
    valid = (c * NSA_CMP_STRIDE + (NSA_CMP_LEN - 1)) <= t
    psum = jnp.zeros((tq, ncp), F32)
    for h in range(NSA_HPG):
        q = q_ref[0, :, h * HEAD_DIM:(h + 1) * HEAD_DIM]
        s = jnp.where(valid, _dot_nt(q, kc), NEG)
        m = jnp.max(s, axis=1, keepdims=True)
        e = jnp.where(valid, jnp.exp2(s - m), 0.0)
        l = jnp.sum(e, axis=1, keepdims=True)
        p = e / jnp.where(l > 0.0, l, 1.0)
        o_ref[0, :, h * HEAD_DIM:(h + 1) * HEAD_DIM] = _bdot(p, vc).astype(o_ref.dtype)
        psum = psum + p
    imp = jnp.transpose(_dot_exact_rhs(psum, ov_ref[...]))[0:n_sel]
    jj = lax.broadcasted_iota(jnp.int32, (n_sel, tq), 0)
    blk_t = jnp.right_shift(i * tq + lax.broadcasted_iota(jnp.int32, (n_sel, tq), 1), SEL_SHIFT)
    forced = (jj == 0) | (jj == blk_t) | (jj == blk_t - 1)
    imp = jnp.where(forced, 1e9, jnp.where(jj > blk_t, -1.0, imp))
    ng = n_sel // SUBLANES
    groups = [imp[g * SUBLANES:(g + 1) * SUBLANES] for g in range(ng)]
    ranks = [jnp.zeros((SUBLANES, tq), F32) for _ in range(ng)]
    sub = lax.broadcasted_iota(jnp.int32, (SUBLANES, tq), 0)
    for ii in range(n_sel):
        gi, ri = divmod(ii, SUBLANES)
        row = jnp.broadcast_to(imp[ii:ii + 1, :], (SUBLANES, tq))
        for g in range(ng):
            if g > gi:
                beats = row >= groups[g]
            elif g < gi:
                beats = row > groups[g]
            else:
                beats = (row > groups[g]) | ((row == groups[g]) & (sub > ri))
            ranks[g] = ranks[g] + jnp.where(beats, 1.0, 0.0)
    chosen = jnp.where(jnp.concatenate(ranks, axis=0) < float(top), 1.0, 0.0)
    pad = sel_ref.shape[3] - n_sel
    if pad:
        chosen = jnp.concatenate([chosen, jnp.zeros((pad, tq), F32)], axis=0)
    sel_ref[0, 0] = jnp.transpose(chosen).astype(sel_ref.dtype)


def _nsa_cmp(proj_rope, kcmp, vcmp, overlap, n_sel, tq):
    b, s, _ = proj_rope.shape
    g = kcmp.shape[1]
    ncp = kcmp.shape[2]
    n_pad = overlap.shape[1]
    top = min(NSA_SEL_TOPK, n_sel)
    qw = NSA_HPG * HEAD_DIM
    return pl.pallas_call(
        functools.partial(_nsa_cmp_kernel, tq=tq, top=top, n_sel=n_sel),
        out_shape=(jax.ShapeDtypeStruct((b, s, NSA_WIDTH), BF16),
                   jax.ShapeDtypeStruct((b, g, s, n_pad), BF16)),
        grid=(b, g, s // tq),
        in_specs=[
            pl.BlockSpec((1, tq, qw), lambda bi, gi, i: (bi, i, ROPE_Q // qw + gi)),
            pl.BlockSpec((1, 1, ncp, HEAD_DIM), lambda bi, gi, i: (bi, gi, 0, 0)),
            pl.BlockSpec((1, 1, ncp, HEAD_DIM), lambda bi, gi, i: (bi, gi, 0, 0)),
            pl.BlockSpec(overlap.shape, lambda bi, gi, i: (0, 0)),
        ],
        out_specs=(pl.BlockSpec((1, tq, qw), lambda bi, gi, i: (bi, i, gi)),
                   pl.BlockSpec((1, 1, tq, n_pad), lambda bi, gi, i: (bi, gi, i, 0))),
        compiler_params=_cparams(("parallel", "parallel", "parallel")),
        name="nsa_cmp_select",
    )(proj_rope, kcmp, vcmp, overlap)


def _nsa_flash_kernel(qi_ref, kb_ref, first_ref, *refs, mode, tq, tk):
    if mode == "sel":
        q_ref, k_ref, v_ref, sel_ref, hot_ref, o_ref, m_ref, l_ref, acc_ref = refs
    else:
        q_ref, k_ref, v_ref, o_ref, m_ref, l_ref, acc_ref = refs
    n = pl.program_id(2)
    i = qi_ref[n]
    kb = kb_ref[n]

    @pl.when(first_ref[n] == 1)
    def _():
        m_ref[...] = jnp.full_like(m_ref, NEG)
        l_ref[...] = jnp.zeros_like(l_ref)
        acc_ref[...] = jnp.zeros_like(acc_ref)

    ngrp = k_ref.shape[2] // HEAD_DIM

    def step(diag):
        valid = None
        if mode == "win":
            rows = i * tq + lax.broadcasted_iota(jnp.int32, (tq, tk), 0)
            cols = kb * tk + lax.broadcasted_iota(jnp.int32, (tq, tk), 1)
            d = rows - cols
            valid = (d >= 0) & (d < NSA_WINDOW)
        elif diag:
            valid = (lax.broadcasted_iota(jnp.int32, (tq, tk), 0) >= lax.broadcasted_iota(jnp.int32, (tq, tk), 1))
        for gi in range(ngrp):
            k = k_ref[0, :, gi * HEAD_DIM:(gi + 1) * HEAD_DIM]
            if mode == "sel":
                penalty = ((sel_ref[0, gi].astype(F32) - 1.0) * (-NEG)).astype(BF16)
                k = jnp.concatenate([k, hot_ref[...]], axis=1)
            v_ones = jnp.concatenate([v_ref[0, :, gi * HEAD_DIM:(gi + 1) * HEAD_DIM],
                                      jnp.ones((tk, LANES), BF16)], axis=1)
            for hd in range(NSA_HPG):
                slot = gi * NSA_HPG + hd
                q = q_ref[0, :, slot * HEAD_DIM:(slot + 1) * HEAD_DIM]
                if mode == "sel":
                    q = jnp.concatenate([q, penalty], axis=1)
                s = _dot_nt(q, k)
                if valid is not None:
                    s = jnp.where(valid, s, NEG)
                m_old = m_ref[slot]
                m_new = jnp.maximum(m_old, jnp.max(s, axis=1, keepdims=True))
                alpha = jnp.exp2(m_old - m_new)
                p = jnp.exp2(s - jnp.concatenate([m_new] * (tk // LANES), axis=1))
                pv = jnp.dot(p.astype(BF16), v_ones, preferred_element_type=F32)
                l_ref[slot] = alpha * l_ref[slot] + pv[:, HEAD_DIM:]
                acc_ref[slot] = alpha * acc_ref[slot] + pv[:, 0:HEAD_DIM]
                m_ref[slot] = m_new

    if mode == "sel":
        pl.when(kb < i)(functools.partial(step, False))
        pl.when(kb == i)(functools.partial(step, True))
    else:
        step(True)

    @pl.when(kb == i)
    def _():
        for slot in range(ngrp * NSA_HPG):
            o_ref[0, :, slot * HEAD_DIM:(slot + 1) * HEAD_DIM] = (acc_ref[slot] / l_ref[slot]).astype(o_ref.dtype)


def _nsa_flash(proj_rope, proj_plain, sel, mode, tq, tk):
    b, s, _ = proj_rope.shape
    g = NSA_GROUPS
    gpb = g
    qw = gpb * NSA_HPG * HEAD_DIM
    kw = gpb * HEAD_DIM
    nq = s // tq
    assert tq == tk
    if mode == "sel":
        back = nq
        koff, voff = ROPE_KS, PL_VS
    else:
        assert NSA_WINDOW % tk == 0
        back = NSA_WINDOW // tk
        koff, voff = ROPE_KW, PL_VW
    assert g % gpb == 0 and ROPE_Q % qw == 0 and koff % kw == 0 and voff % kw == 0
    qi, kb, first = _causal_pairs(nq, back)
    in_specs = [
        pl.BlockSpec((1, tq, qw), lambda bi, gi, n, qi, kb, fs: (bi, qi[n], ROPE_Q // qw + gi)),
        pl.BlockSpec((1, tk, kw), lambda bi, gi, n, qi, kb, fs: (bi, kb[n], koff // kw + gi)),
        pl.BlockSpec((1, tk, kw), lambda bi, gi, n, qi, kb, fs: (bi, kb[n], voff // kw + gi)),
    ]
    args = [proj_rope, proj_rope, proj_plain]
    if mode == "sel":
        n_pad = sel.shape[3]
        in_specs.append(pl.BlockSpec((1, gpb, tq, n_pad), lambda bi, gi, n, qi, kb, fs: (bi, gi, qi[n], 0)))
        in_specs.append(pl.BlockSpec((tk, n_pad), lambda bi, gi, n, qi, kb, fs: (kb[n], 0)))
        block_of_key = jnp.arange(s, dtype=jnp.int32)[:, None] // NSA_SEL_BLOCK
        onehot = (block_of_key == jnp.arange(n_pad, dtype=jnp.int32)[None, :]).astype(BF16)
        args += [sel, onehot]
    return pl.pallas_call(
        functools.partial(_nsa_flash_kernel, mode=mode, tq=tq, tk=tk),
        out_shape=jax.ShapeDtypeStruct((b, s, NSA_WIDTH), BF16),
        grid_spec=pltpu.PrefetchScalarGridSpec(
            num_scalar_prefetch=3,
            grid=(b, g // gpb, qi.shape[0]),
            in_specs=in_specs,
            out_specs=pl.BlockSpec((1, tq, qw), lambda bi, gi, n, qi, kb, fs: (bi, qi[n], gi)),
            scratch_shapes=[
                pltpu.VMEM((gpb * NSA_HPG, tq, LANES), F32),
                pltpu.VMEM((gpb * NSA_HPG, tq, LANES), F32),
                pltpu.VMEM((gpb * NSA_HPG, tq, HEAD_DIM), F32),
            ],
        ),
        compiler_params=_cparams(("parallel", "parallel", "arbitrary")),
        name="nsa_flash_" + mode,
    )(qi, kb, first, *args)


def _causal_pairs(nq, back):
    qi, kb, first = [], [], []
    for i in range(nq):
        lo = max(0, i - back)
        for j in range(lo, i + 1):
            qi.append(i)
            kb.append(j)
            first.append(1 if j == lo else 0)
    as_i32 = lambda z: jnp.asarray(np.asarray(z, np.int32))
    return as_i32(qi), as_i32(kb), as_i32(first)


def _diff_flash_kernel(qi_ref, kb_ref, first_ref, q_ref, k_ref, v_ref, lam_ref, sub_ref, o_ref, m_ref, l_ref,
                       acc_ref, *, tq, tk, lam_init):
    n = pl.program_id(2)
    i = qi_ref[n]
    kb = kb_ref[n]

    @pl.when(first_ref[n] == 1)
    def _():
        m_ref[...] = jnp.full_like(m_ref, NEG)
        l_ref[...] = jnp.zeros_like(l_ref)
        acc_ref[...] = jnp.zeros_like(acc_ref)

    hpb = o_ref.shape[2] // DIFF_VDIM

    def step(masked):
        if masked:
            valid = (lax.broadcasted_iota(jnp.int32, (tq, tk), 0) >= lax.broadcasted_iota(jnp.int32, (tq, tk), 1))
        for hd in range(hpb):
            v = v_ref[0, :, hd * DIFF_VDIM:(hd + 1) * DIFF_VDIM]
            for mp in range(2):
                slot = 2 * hd + mp
                q = q_ref[0, :, slot * HEAD_DIM:(slot + 1) * HEAD_DIM]
                k = k_ref[0, :, slot * HEAD_DIM:(slot + 1) * HEAD_DIM]
                s = _dot_nt(q, k)
                if masked:
                    s = jnp.where(valid, s, NEG)
                m_old = m_ref[slot]
                m_new = jnp.maximum(m_old, jnp.max(s, axis=1, keepdims=True))
                alpha = jnp.exp2(m_old - m_new)
                p = jnp.exp2(s - jnp.concatenate([m_new] * (tk // LANES), axis=1))
                l_ref[slot] = alpha * l_ref[slot] + jnp.sum(p, axis=1, keepdims=True)
                acc_ref[slot] = (jnp.concatenate([alpha] * (DIFF_VDIM // LANES), axis=1) * acc_ref[slot]
                                 + _bdot(p, v))
                m_ref[slot] = m_new

    pl.when(kb < i)(functools.partial(step, False))
    pl.when(kb == i)(functools.partial(step, True))

    @pl.when(kb == i)
    def _():
        lam = lam_ref[...]
        lam_full = (jnp.exp(jnp.sum(lam[0:1, :] * lam[1:2, :], axis=1, keepdims=True))
                    - jnp.exp(jnp.sum(lam[2:3, :] * lam[3:4, :], axis=1, keepdims=True)) + lam_init)
        wide = lambda x: jnp.concatenate([x] * (DIFF_VDIM // LANES), axis=1)
        for hd in range(hpb):
            o = (acc_ref[2 * hd] / wide(l_ref[2 * hd])
                 - lam_full * (acc_ref[2 * hd + 1] / wide(l_ref[2 * hd + 1])))
            ms = jnp.mean(o * o, axis=-1, keepdims=True)
            o = o * lax.rsqrt(ms + 1e-5) * sub_ref[...]
            o_ref[0, :, hd * DIFF_VDIM:(hd + 1) * DIFF_VDIM] = (o * (1.0 - lam_init)).astype(o_ref.dtype)


def _diff_flash(proj_rope, proj_plain, lam, subln, layer, tq, tk):
    b, s, _ = proj_rope.shape
    assert tq == tk
    nq = s // tq
    lam_init = 0.8 - 0.6 * math.exp(-0.3 * layer)
    hpb = DIFF_HEADS
    w = hpb * DIFF_VDIM
    assert DIFF_HEADS % hpb == 0 and ROPE_DQ % w == 0 and ROPE_DK % w == 0 and PL_DV % w == 0
    qi, kb, first = _causal_pairs(nq, nq)
    return pl.pallas_call(
        functools.partial(_diff_flash_kernel, tq=tq, tk=tk, lam_init=lam_init),
        out_shape=jax.ShapeDtypeStruct((b, s, DIFF_WIDTH), BF16),
        grid_spec=pltpu.PrefetchScalarGridSpec(
            num_scalar_prefetch=3,
            grid=(b, DIFF_HEADS // hpb, qi.shape[0]),
            in_specs=[
                pl.BlockSpec((1, tq, w), lambda bi, h, n, qi, kb, fs: (bi, qi[n], ROPE_DQ // w + h)),
                pl.BlockSpec((1, tk, w), lambda bi, h, n, qi, kb, fs: (bi, kb[n], ROPE_DK // w + h)),
                pl.BlockSpec((1, tk, w), lambda bi, h, n, qi, kb, fs: (bi, kb[n], PL_DV // w + h)),
                pl.BlockSpec(lam.shape, lambda bi, h, n, qi, kb, fs: (0, 0)),
                pl.BlockSpec((1, DIFF_VDIM), lambda bi, h, n, qi, kb, fs: (0, 0)),
            ],
            out_specs=pl.BlockSpec((1, tq, w), lambda bi, h, n, qi, kb, fs: (bi, qi[n], h)),
            scratch_shapes=[
                pltpu.VMEM((2 * hpb, tq, LANES), F32),
                pltpu.VMEM((2 * hpb, tq, LANES), F32),
                pltpu.VMEM((2 * hpb, tq, DIFF_VDIM), F32),
            ],
        ),
        compiler_params=_cparams(("parallel", "parallel", "arbitrary")),
        name="diff_flash",
    )(qi, kb, first, proj_rope, proj_rope, proj_plain, lam, subln[None, :])


def _bmm(a, b):
    return jnp.einsum("umk,ukn->umn", a.astype(BF16), b.astype(BF16), preferred_element_type=F32)


def _bmm_nt(a, b):
    return jnp.einsum("umk,unk->umn", a.astype(BF16), b.astype(BF16), preferred_element_type=F32)


def _rwkv_scan_kernel(r_ref, k_ref, v_ref, gd_ref, wa_ref, mur_ref, muk_ref, muv_ref, mugd_ref, muwa_ref,
                      wdec_ref, wrate_ref, gup_ref, w0_ref, a0_ref, kk_ref, ka_ref, rk_ref, lnw_ref, lnb_ref,
                      o_ref, st_ref, last_ref, lastgd_ref, *, ts, chunk):
    @pl.when(pl.program_id(2) == 0)
    def _():
        st_ref[...] = jnp.zeros_like(st_ref)
        last_ref[...] = jnp.zeros_like(last_ref)
        lastgd_ref[...] = jnp.zeros_like(lastgd_ref)

    def shifted(x_ref, mu_ref, carry):
        x = x_ref[0]
        row = lax.broadcasted_iota(jnp.int32, x.shape, 0)
        prev = jnp.where(row == 0, carry, pltpu.roll(x, 1, 0))
        return x + (prev - x) * mu_ref[...]

    r = shifted(r_ref, mur_ref, last_ref[0, 0:1, :])
    k = shifted(k_ref, muk_ref, last_ref[1, 0:1, :])
    v = shifted(v_ref, muv_ref, last_ref[2, 0:1, :])
    wa = shifted(wa_ref, muwa_ref, last_ref[3, 0:1, :])
    gd = shifted(gd_ref, mugd_ref, lastgd_ref[0:1, :])
    for n, ref in enumerate((r_ref, k_ref, v_ref, wa_ref)):
        last_ref[n, 0:1, :] = ref[0, ts - 1:ts, :]
    lastgd_ref[0:1, :] = gd_ref[0, ts - 1:ts, :]

    wa = jnp.where(lax.broadcasted_iota(jnp.int32, wa.shape, 1) < RWKV_W_LORA, jnp.tanh(wa), wa)
    lw = -math.exp(-0.5) * jax.nn.sigmoid(w0_ref[...] + _bdot(wa, wdec_ref[...]))
    a = jax.nn.sigmoid(a0_ref[...] + _bdot(wa, wrate_ref[...]))
    g = _bdot(jax.nn.sigmoid(gd), gup_ref[...])

    c = chunk
    nc = ts // c
    head0 =lax.broadcasted_iota(jnp.int32, (ts, LANES), 1) < RWKV_HEAD

    def seg_sum(x):
        s0 = jnp.sum(jnp.where(head0, x, 0.0), axis=1, keepdims=True)
        s1 = jnp.sum(jnp.where(head0, 0.0, x), axis=1, keepdims=True)
        return jnp.where(head0, s0, s1)

    kk = k * kk_ref[...]
    kap = kk / jnp.maximum(jnp.sqrt(seg_sum(kk * kk)), 1e-12)
    kmod = k * (1.0 + (a - 1.0) * ka_ref[...])
    bvec = kap * a

    ri = lax.broadcasted_iota(jnp.int32, (ts, ts), 0)
    ci = lax.broadcasted_iota(jnp.int32, (ts, ts), 1)
    same_chunk = jnp.right_shift(ri, c.bit_length() - 1) == jnp.right_shift(ci, c.bit_length() - 1)
    tri = jnp.where(same_chunk & (ri >= ci), 1.0, 0.0).astype(BF16)
    cum = _dot_exact_lhs(tri, lw)
    g_incl = jnp.exp(cum)
    g_inv = jnp.exp(-cum)
    split = lambda x: x.reshape(nc, c, LANES)
    g_last = [g_incl[(ch + 1) * c - 1:(ch + 1) * c, :] for ch in range(nc)]
    g_last_rows = jnp.concatenate([jnp.broadcast_to(gl, (c, LANES)) for gl in g_last], axis=0)
    kh = split(kmod * g_inv)
    bh = split(bvec * g_inv)
    kg = split(kmod * g_inv * g_last_rows)
    bg = split(bvec * g_inv * g_last_rows)
    kaph = kap * jnp.exp(cum - lw)
    rh = r * g_incl
    v3 = split(v)

    assert 2 * c == LANES
    h0 = lax.broadcasted_iota(jnp.int32, (nc, c, LANES), 2) < RWKV_HEAD
    stack = lambda y: jnp.concatenate([jnp.where(h0, y, 0.0), jnp.where(h0, 0.0, y)], axis=1)
    kap3 = split(kaph)
    r3 = split(rh)
    gram = _bmm_nt(jnp.concatenate([kap3, r3], axis=1),
                   jnp.concatenate([stack(bh), stack(kh)], axis=1))
    row = lax.broadcasted_iota(jnp.int32, (c, 4 * c), 0)
    col = lax.broadcasted_iota(jnp.int32, (c, 4 * c), 1) & (c - 1)
    top = jnp.where(row > col, gram[:, 0:c, :], 0.0)
    abk = jnp.where(row >= col, gram[:, c:2 * c, :], 0.0)
    lb = top[:, :, 0:LANES]
    lk = top[:, :, LANES:2 * LANES]
    tm = -lb
    q = _bmm(lb, stack(lb))
    n = 2
    while True:
        tm = tm + q + _bmm(tm, stack(q))
        n *= 2
        if n >= c:
            break
        q = _bmm(q, stack(q))
    kap_t = kap3 + _bmm(tm, stack(kap3))
    lkv = _bmm(lk, stack(v3))
    z0 = -(lkv + _bmm(tm, stack(lkv)))
    r_t = r3 - _bmm(abk[:, :, 0:LANES], stack(kap_t))
    y0 = _bmm(abk, jnp.concatenate([stack(z0), stack(v3)], axis=1))

    br = lax.broadcasted_iota(jnp.int32, (LANES, LANES), 0)
    bc = lax.broadcasted_iota(jnp.int32, (LANES, LANES), 1)
    blockdiag = (br < RWKV_HEAD) == (bc < RWKV_HEAD)
    state = st_ref[...]
    ys = []
    for ch in range(nc):
        trans = jnp.where(blockdiag, -_dot_tn(bg[ch], kap_t[ch]), 0.0)
        trans = trans + jnp.where(br == bc, jnp.broadcast_to(g_last[ch], (LANES, LANES)), 0.0)
        inject = jnp.where(blockdiag, _dot_tn(jnp.concatenate([bg[ch], kg[ch]], axis=0),
                                              jnp.concatenate([z0[ch], v3[ch]], axis=0)), 0.0)
        ys.append(_bdot(r_t[ch], state) + y0[ch])
        state = _dot3(trans, state) + inject
    st_ref[...] = state
    y = jnp.concatenate(ys, axis=0)

    mean = seg_sum(y) * (1.0 / RWKV_HEAD)
    yc = y - mean
    var = seg_sum(yc * yc) * (1.0 / RWKV_HEAD)
    yn = yc * lax.rsqrt(var + RWKV_LNX_EPS) * lnw_ref[...] + lnb_ref[...]
    bonus = seg_sum(r * kmod * rk_ref[...]) * v
    o_ref[0] = ((yn + bonus) * g).astype(o_ref.dtype)


def _rwkv_scan(proj3, mu_p, wwa, gup, w0, a0, k_k, k_a, r_k, lnx_w, lnx_b, ts=512, chunk=64):
    b, s, _ = proj3.shape
    w = RWKV_WIDTH
    ts = min(ts, s)
    assert s % ts == 0 and ts % chunk == 0
    npair = w // LANES
    col = lambda off: pl.BlockSpec((1, ts, LANES), lambda bi, j, t: (bi, t, off // LANES + j))
    mucol = lambda off: pl.BlockSpec((1, LANES), lambda bi, j, t: (0, off // LANES + j))
    vec = pl.BlockSpec((1, LANES), lambda bi, j, t: (0, j))
    in_specs = [
        col(0), col(w), col(2 * w),
        pl.BlockSpec((1, ts, RW_G_PAD), lambda bi, j, t: (bi, t, RW_GD // RW_G_PAD)),
        pl.BlockSpec((1, ts, RW_LORA_PAD), lambda bi, j, t: (bi, t, RW_WA // RW_LORA_PAD)),
        mucol(0), mucol(w), mucol(2 * w),
        pl.BlockSpec((1, RW_G_PAD), lambda bi, j, t: (0, RW_GD // RW_G_PAD)),
        pl.BlockSpec((1, RW_LORA_PAD), lambda bi, j, t: (0, RW_WA // RW_LORA_PAD)),
        pl.BlockSpec((RW_LORA_PAD, LANES), lambda bi, j, t: (0, j)),
        pl.BlockSpec((RW_LORA_PAD, LANES), lambda bi, j, t: (0, npair + j)),
        pl.BlockSpec((RW_G_PAD, LANES), lambda bi, j, t: (0, j)),
    ] + [vec] * 7
    row = lambda z: z[None, :]
    return pl.pallas_call(
        functools.partial(_rwkv_scan_kernel, ts=ts, chunk=chunk),
        out_shape=jax.ShapeDtypeStruct((b, s, w), BF16),
        grid=(b, npair, s // ts),
        in_specs=in_specs,
        out_specs=pl.BlockSpec((1, ts, LANES), lambda bi, j, t: (bi, t, j)),
        scratch_shapes=[
            pltpu.VMEM((LANES, LANES), F32),
            pltpu.VMEM((4, SUBLANES, LANES), F32),
            pltpu.VMEM((SUBLANES, RW_G_PAD), F32),
        ],
        compiler_params=_cparams(("parallel", "parallel", "arbitrary")),
        name="rwkv_scan",
    )(proj3, proj3, proj3, proj3, proj3, mu_p, mu_p, mu_p, mu_p, mu_p, wwa, wwa, gup,
      row(w0), row(a0), row(k_k), row(k_a), row(r_k), row(lnx_w), row(lnx_b))


def _merge_kernel(oc_ref, os_ref, ow_ref, ng_ref, yd_ref, yr_ref, wb_ref, g0_ref, g1_ref, g2_ref, bias_ref,
                  o_ref, yn_ref):
    @pl.when(pl.program_id(1) == 0)
    def _():
        gates = jax.nn.sigmoid(ng_ref[...])
        for hd in range(NSA_HEADS):
            sl = slice(hd * HEAD_DIM, (hd + 1) * HEAD_DIM)
            o = (gates[:, 3 * hd:3 * hd + 1] * oc_ref[:, sl]
                 + gates[:, 3 * hd + 1:3 * hd + 2] * os_ref[:, sl]
                 + gates[:, 3 * hd + 2:3 * hd + 3] * ow_ref[:, sl])
            yn_ref[:, sl] = o.astype(BF16)

    branches = (yn_ref, yd_ref, yr_ref)
    graw = (g0_ref, g1_ref, g2_ref)
    acc = None
    for bi in range(N_BRANCH):
        gate = jax.nn.sigmoid(graw[bi][...] + bias_ref[bi])
        term = gate * jnp.dot(branches[bi][...], wb_ref[bi], preferred_element_type=F32)
        acc = term if acc is None else acc + term
    o_ref[...] = acc.astype(o_ref.dtype)


def _merge(o_cmp, o_slc, o_win, proj_f32, proj_plain, y_diff, y_rwkv, w_branch, b_gate, tm=1024, tn=512):
    m = o_cmp.shape[0]
    n = w_branch.shape[2]
    tm = min(tm, m)
    assert m % tm == 0 and n % tn == 0 and PL_BG % tn == 0
    bw = BRANCH_WIDTH
    rowblk = lambda: pl.BlockSpec((tm, bw), lambda i, j: (i, 0))
    gate_spec = lambda bi: pl.BlockSpec((tm, tn), lambda i, j: (i, PL_BG // tn + bi * (n // tn) + j))
    return pl.pallas_call(
        _merge_kernel,
        out_shape=jax.ShapeDtypeStruct((m, n), BF16),
        grid=(m // tm, n // tn),
        in_specs=[
            rowblk(), rowblk(), rowblk(),
            pl.BlockSpec((tm, LANES), lambda i, j: (i, F_NG // LANES)),
            rowblk(), rowblk(),
            pl.BlockSpec((N_BRANCH, bw, tn), lambda i, j: (0, 0, j)),
            gate_spec(0), gate_spec(1), gate_spec(2),
            pl.BlockSpec((N_BRANCH, 1, tn), lambda i, j: (0, 0, j)),
        ],
        out_specs=pl.BlockSpec((tm, tn), lambda i, j: (i, j)),
        scratch_shapes=[pltpu.VMEM((tm, bw), BF16)],
        compiler_params=_cparams(("parallel", "arbitrary")),
        name="gated_merge",
    )(o_cmp, o_slc, o_win, proj_f32, y_diff, y_rwkv, w_branch, proj_plain, proj_plain, proj_plain,
      b_gate[:, None, :])


def _pack_layer_weights(w_in, w_gate, rwkv_mu, w_up_lora, a_up_lora, g_up_lora):
    d = w_in.shape[0]
    nsa_sizes = (NSA_WIDTH,) + (NSA_KV,) * 6 + (3 * NSA_HEADS,)
    diff_sizes = (DIFF_QK, DIFF_QK, DIFF_WIDTH)
    rw_sizes = (RWKV_WIDTH,) * 3 + (RWKV_W_LORA, RWKV_A_LORA, RWKV_G_LORA)
    offs = np.cumsum((0,) + nsa_sizes + diff_sizes + rw_sizes)
    seg = [w_in[:, offs[i]:offs[i + 1]] for i in range(len(offs) - 1)]
    q, kc, vc, ks, vs, kw, vw, ng, dq, dk, dv, rr, rk, rv, wd, ad, gd = seg
    w_rope = jnp.concatenate([q, dq, dk, ks, kw], axis=1).astype(BF16)
    w_plain = jnp.concatenate([kc, vc, vs, vw, dv] + [w_gate[bi] for bi in range(N_BRANCH)], axis=1).astype(BF16)
    zpad = lambda n: jnp.zeros((d, n), w_in.dtype)
    w_f32 = jnp.concatenate(
        [rr, rk, rv, gd, zpad(RW_G_PAD - RWKV_G_LORA), wd, ad, ng, zpad(LANES - 3 * NSA_HEADS)],
        axis=1).astype(BF16)
    mu_rkv, mu_wa, mu_gd = jnp.split(rwkv_mu, [3 * RWKV_WIDTH, 3 * RWKV_WIDTH + RW_LORA_PAD])
    mu_p = jnp.concatenate([mu_rkv, mu_gd, jnp.zeros((RW_G_PAD - RWKV_G_LORA,), rwkv_mu.dtype), mu_wa])[None, :]
    zz = jnp.zeros((RWKV_W_LORA, RWKV_WIDTH), w_up_lora.dtype)
    wwa = jnp.concatenate([jnp.concatenate([w_up_lora, zz], axis=1),
                           jnp.concatenate([zz, a_up_lora], axis=1)], axis=0).astype(BF16)
    gup = jnp.concatenate([g_up_lora, jnp.zeros((RW_G_PAD - RWKV_G_LORA, RWKV_WIDTH), g_up_lora.dtype)],
                          axis=0).astype(BF16)
    return w_rope, w_plain, w_f32, mu_p, wwa, gup


def _overlap_matrix(n_cp, n_sel):
    c_start = np.arange(n_cp) * NSA_CMP_STRIDE
    c_end = c_start + NSA_CMP_LEN - 1
    j_start = np.arange(n_sel) * NSA_SEL_BLOCK
    ov = (c_start[:, None] <= j_start[None, :] + NSA_SEL_BLOCK - 1) & (c_end[:, None] >= j_start[None, :])
    ov = np.pad(ov.astype(np.float32), ((0, 0), (0, -n_sel % LANES)))
    return jnp.asarray(ov).astype(BF16)


def _chunk_rows(z, groups):
    b, s, _ = z.shape
    z = z.reshape(b, s // NSA_CMP_STRIDE, NSA_CMP_STRIDE, groups, HEAD_DIM)
    return z.transpose(0, 3, 1, 2, 4).reshape(b, groups, s // NSA_CMP_STRIDE, NSA_CMP_STRIDE * HEAD_DIM)


def _nsa_branches(proj_rope, proj_plain, positions, cmp_pos, cmp_w1, cmp_w2, tq):
    b, s, _ = proj_rope.shape
    n_chunk = s // NSA_CMP_STRIDE
    n_sel = s // NSA_SEL_BLOCK
    kc2 = _chunk_rows(proj_plain[:, :, PL_KC:PL_KC + NSA_KV], NSA_GROUPS)
    vc2 = _chunk_rows(proj_plain[:, :, PL_VC:PL_VC + NSA_KV], NSA_GROUPS)
    cmp_end = np.minimum(np.arange(n_chunk) * NSA_CMP_STRIDE + NSA_CMP_LEN - 1, s - 1)
    pos_c = jnp.take(positions, jnp.asarray(cmp_end), axis=1)
    cosc, sinc = _rope_tables(pos_c.reshape(-1))
    cosc = cosc.reshape(b, n_chunk, LANES)
    sinc = sinc.reshape(b, n_chunk, LANES)
    w1 = cmp_w1.astype(BF16)
    w2 = cmp_w2.astype(BF16)
    pe = cmp_pos.reshape(2, 1, NSA_CMP_LEN * HEAD_DIM)
    kcmp = _nsa_compress(kc2, w1[0], w2[0], pe[0], cosc, sinc, True)
    vcmp = _nsa_compress(vc2, w1[1], w2[1], pe[1], cosc, sinc, False)
    o_cmp, sel = _nsa_cmp(proj_rope, kcmp, vcmp, _overlap_matrix(n_chunk, n_sel), n_sel, tq)
    o_slc = _nsa_flash(proj_rope, proj_plain, sel, "sel", tq, tq)
    o_win = _nsa_flash(proj_rope, proj_plain, None, "win", tq, tq)
    return o_cmp, o_slc, o_win


def _layer(x2, b, s, layer, cosf, sinf, positions, p):
    m = b * s
    w_rope, w_plain, w_f32, mu_p, wwa, gup = _pack_layer_weights(
        p["w_in"], p["w_gate"], p["rwkv_mu"], p["rwkv_w_up"], p["rwkv_a_up"], p["rwkv_g_up"])
    scale = HEAD_DIM ** -0.5 * math.log2(math.e)
    colscale = jnp.concatenate([jnp.full((ROPE_SCALED,), scale, F32),
                                jnp.ones((ROPE_COLS - ROPE_SCALED,), F32)])[None, :]
    g_pre = p["norm_pre_mix"]
    proj_rope = _norm_matmul(x2, g_pre, w_rope, "rope", BF16, rope=(cosf, sinf, colscale), tm=512,
                             tn=ROPE_COLS // 2)
    proj_plain = _norm_matmul(x2, g_pre, w_plain, "plain", BF16, tn=1024)
    proj_f32 = _norm_matmul(x2, g_pre, w_f32, "plain", F32, tm=512, tn=F32_COLS // 2)
    pr3 = proj_rope.reshape(b, s, ROPE_COLS)
    pp3 = proj_plain.reshape(b, s, PLAIN_COLS)

    tq = min(512, s)
    o_cmp, o_slc, o_win = _nsa_branches(pr3, pp3, positions, p["nsa_cmp_pos"], p["nsa_cmp_w1"], p["nsa_cmp_w2"], tq)
    y_diff = _diff_flash(pr3, pp3, p["diff_lambda"], p["diff_subln"], layer, tq, tq)

    y_rwkv = _rwkv_scan(proj_f32.reshape(b, s, F32_COLS), mu_p, wwa, gup, p["rwkv_w0"], p["rwkv_a0"],
                        p["rwkv_k_k"], p["rwkv_k_a"], p["rwkv_r_k"].reshape(-1), p["rwkv_lnx_w"], p["rwkv_lnx_b"])

    merged = _merge(o_cmp.reshape(m, NSA_WIDTH), o_slc.reshape(m, NSA_WIDTH), o_win.reshape(m, NSA_WIDTH),
                    proj_f32, proj_plain, y_diff.reshape(m, DIFF_WIDTH), y_rwkv.reshape(m, RWKV_WIDTH),
                    p["w_branch"].astype(BF16), p["b_gate"])
    x2 = _matmul_norm_res(merged, p["w_out"].astype(BF16), p["norm_post_mix"], x2)
    hidden = _norm_matmul(x2, p["norm_pre_mlp"], p["w_up"].astype(BF16), "relu2", BF16, tn=1024)
    x2 = _matmul_norm_res(hidden, p["w_down"].astype(BF16), p["norm_post_mlp"], x2)
    return x2


def kernel(x, positions, norm_pre_mix, norm_post_mix, norm_pre_mlp, norm_post_mlp, w_in, nsa_cmp_pos, nsa_cmp_w1, nsa_cmp_w2, diff_lambda, diff_subln, rwkv_mu, rwkv_w0, rwkv_w_up, rwkv_a0, rwkv_a_up, rwkv_g_up, rwkv_k_k, rwkv_k_a, rwkv_r_k, rwkv_lnx_w, rwkv_lnx_b, w_gate, b_gate, w_branch, w_out, w_up, w_down):
    b, s, d = x.shape
    depth = w_in.shape[0]
    stacked = dict(
        norm_pre_mix=norm_pre_mix, norm_post_mix=norm_post_mix, norm_pre_mlp=norm_pre_mlp,
        norm_post_mlp=norm_post_mlp, w_in=w_in, nsa_cmp_pos=nsa_cmp_pos, nsa_cmp_w1=nsa_cmp_w1,
        nsa_cmp_w2=nsa_cmp_w2, diff_lambda=diff_lambda, diff_subln=diff_subln, rwkv_mu=rwkv_mu,
        rwkv_w0=rwkv_w0, rwkv_w_up=rwkv_w_up, rwkv_a0=rwkv_a0, rwkv_a_up=rwkv_a_up, rwkv_g_up=rwkv_g_up,
        rwkv_k_k=rwkv_k_k, rwkv_k_a=rwkv_k_a, rwkv_r_k=rwkv_r_k, rwkv_lnx_w=rwkv_lnx_w,
        rwkv_lnx_b=rwkv_lnx_b, w_gate=w_gate, b_gate=b_gate, w_branch=w_branch, w_out=w_out,
        w_up=w_up, w_down=w_down)
    cosf, sinf = _rope_tables(positions.reshape(-1))
    x2 = x.reshape(b * s, d)
    for layer in range(depth):
        x2 = _layer(x2, b, s, layer, cosf, sinf, positions, {n: a[layer] for n, a in stacked.items()})
    return x2.reshape(b, s, d)
```

```python
import functools
import math

import jax
import jax.numpy as jnp
import numpy as np
from jax import lax
from jax.experimental import pallas as pl
from jax.experimental.pallas import tpu as pltpu

F32 = jnp.float32
BF16 = jnp.bfloat16

D_MODEL = 2048
RMS_EPS = 1e-6
ROPE_THETA = 500000.0
HEAD_DIM = 128
ROT_HALF = HEAD_DIM // 8
NSA_HEADS = 8
NSA_GROUPS = 2
NSA_HPG = NSA_HEADS // NSA_GROUPS
NSA_CMP_LEN = 32
NSA_CMP_STRIDE = 16
NSA_CMP_HIDDEN = 256
NSA_SEL_BLOCK = 64
SEL_SHIFT = NSA_SEL_BLOCK.bit_length() - 1
NSA_SEL_TOPK = 16
NSA_WINDOW = 512
NSA_WIDTH = NSA_HEADS * HEAD_DIM
NSA_KV = NSA_GROUPS * HEAD_DIM
DIFF_HEADS = 4
DIFF_VDIM = 2 * HEAD_DIM
DIFF_WIDTH = DIFF_HEADS * DIFF_VDIM
DIFF_QK = 2 * DIFF_HEADS * HEAD_DIM
RWKV_HEAD = 64
RWKV_WIDTH = 1024
RWKV_W_LORA = 64
RWKV_A_LORA = 64
RWKV_G_LORA = 160
RWKV_LNX_EPS = 64e-5
N_BRANCH = 3
BRANCH_WIDTH = 1024

LANES = 128
SUBLANES = 8
VMEM_LIMIT_BYTES = 56 * 1024 * 1024

NEG = -1e30

ROPE_Q, ROPE_DQ, ROPE_DK, ROPE_KS, ROPE_KW = 0, 1024, 2048, 3072, 3328
ROPE_SCALED = ROPE_DK
ROPE_COLS = 3584
PL_KC, PL_VC, PL_VS, PL_VW, PL_DV, PL_BG = 0, 256, 512, 768, 1024, 2048
PLAIN_COLS = PL_BG + N_BRANCH * D_MODEL
RW_LORA_PAD = 128
RW_G_PAD = 256
RW_GD = 3 * RWKV_WIDTH
RW_WA = RW_GD + RW_G_PAD
RW_COLS = RW_WA + RW_LORA_PAD
F_NG = RW_COLS
F32_COLS = RW_COLS + LANES


def _cparams(sem):
    return pltpu.CompilerParams(dimension_semantics=sem, vmem_limit_bytes=VMEM_LIMIT_BYTES)


def _bdot(a, b):
    return jnp.dot(a.astype(BF16), b.astype(BF16), preferred_element_type=F32)


def _dot_nt(a, b):
    return lax.dot_general(a.astype(BF16), b.astype(BF16), (((1,), (1,)), ((), ())),
                           preferred_element_type=F32)


def _dot_tn(a, b):
    return lax.dot_general(a.astype(BF16), b.astype(BF16), (((0,), (0,)), ((), ())),
                           preferred_element_type=F32)


def _split2(x):
    hi = x.astype(BF16)
    lo = (x - hi.astype(F32)).astype(BF16)
    return hi, lo


def _split3(x):
    hi = x.astype(BF16)
    r1 = x - hi.astype(F32)
    mid = r1.astype(BF16)
    lo = (r1 - mid.astype(F32)).astype(BF16)
    return hi, mid, lo


def _dot_exact_lhs(a_bf, x):
    hi, mid, lo = _split3(x)
    d = lambda p: jnp.dot(a_bf, p, preferred_element_type=F32)
    return d(hi) + (d(mid) + d(lo))


def _dot_exact_rhs(x, b_bf):
    hi, mid, lo = _split3(x)
    d = lambda p: jnp.dot(p, b_bf, preferred_element_type=F32)
    return d(hi) + (d(mid) + d(lo))


def _dot3(a, b):
    ah, al = _split2(a)
    bh, bl = _split2(b)
    d = lambda p, q: jnp.dot(p, q, preferred_element_type=F32)
    return d(ah, bh) + (d(ah, bl) + d(al, bh))


def _rope_partner(z, lane):
    return jnp.where(lane < ROT_HALF, pltpu.roll(z, LANES - ROT_HALF, 1), pltpu.roll(z, ROT_HALF, 1))


def _rope_table_kernel(pos_ref, invf_ref, sign_ref, cos_ref, sin_ref):
    ang = pos_ref[...].astype(F32) * invf_ref[...]
    cos_ref[...] = jnp.cos(ang)
    sin_ref[...] = jnp.sin(ang) * sign_ref[...]


def _rope_tables(pos_flat):
    n = pos_flat.shape[0]
    half = ROT_HALF
    inv_freq = ROPE_THETA ** (-jnp.arange(half, dtype=F32) / half)
    zeros = jnp.zeros((LANES - 2 * half,), F32)
    invf = jnp.concatenate([inv_freq, inv_freq, zeros])[None, :]
    sign = jnp.concatenate([-jnp.ones((half,), F32), jnp.ones((half,), F32), zeros])[None, :]
    tm = min(n, 2048)
    assert n % tm == 0
    vec = pl.BlockSpec((1, LANES), lambda i: (0, 0))
    out = pl.BlockSpec((tm, LANES), lambda i: (i, 0))
    return pl.pallas_call(
        _rope_table_kernel,
        out_shape=(jax.ShapeDtypeStruct((n, LANES), F32),) * 2,
        grid=(n // tm,),
        in_specs=[pl.BlockSpec((tm, 1), lambda i: (i, 0)), vec, vec],
        out_specs=(out, out),
        compiler_params=_cparams(("parallel",)),
        name="rope_tables",
    )(pos_flat[:, None], invf, sign)


def _norm_matmul_rope_kernel(x_ref, g_ref, w_ref, cos_ref, sin_ref, cs_ref, o_ref, u_ref, *, tn):
    @pl.when(pl.program_id(1) == 0)
    def _():
        x = x_ref[...]
        ms = jnp.mean(x * x, axis=-1, keepdims=True)
        u_ref[...] = (x * lax.rsqrt(ms + RMS_EPS) * g_ref[...]).astype(BF16)

    acc = jnp.dot(u_ref[...], w_ref[...], preferred_element_type=F32) * cs_ref[...]
    cosv = cos_ref[...]
    sinv = sin_ref[...]
    lane = lax.broadcasted_iota(jnp.int32, cosv.shape, 1)
    for h in range(tn // LANES):
        z = acc[:, h * LANES:(h + 1) * LANES]
        o_ref[:, h * LANES:(h + 1) * LANES] = (z * cosv + _rope_partner(z, lane) * sinv).astype(o_ref.dtype)


def _norm_matmul_rope(x, g, w, cosf, sinf, colscale, tm, tn):
    m, d = x.shape
    n = w.shape[1]
    tm = min(tm, m)
    assert m % tm == 0 and n % tn == 0
    rows = pl.BlockSpec((tm, LANES), lambda i, j: (i, 0))
    return pl.pallas_call(
        functools.partial(_norm_matmul_rope_kernel, tn=tn),
        out_shape=(jax.ShapeDtypeStruct((m, n), BF16), jax.ShapeDtypeStruct((m, d), BF16)),
        grid=(m // tm, n // tn),
        in_specs=[
            pl.BlockSpec((tm, d), lambda i, j: (i, 0)),
            pl.BlockSpec((1, d), lambda i, j: (0, 0)),
            pl.BlockSpec((d, tn), lambda i, j: (0, j)),
            rows, rows,
            pl.BlockSpec((1, tn), lambda i, j: (0, j)),
        ],
        out_specs=(pl.BlockSpec((tm, tn), lambda i, j: (i, j)), pl.BlockSpec((tm, d), lambda i, j: (i, 0))),
        compiler_params=_cparams(("parallel", "arbitrary")),
        name="norm_matmul_rope",
    )(x, g[None, :], w, cosf, sinf, colscale)


def _matmul_kernel(a_ref, w_ref, o_ref, *, relu2):
    acc = jnp.dot(a_ref[...], w_ref[...], preferred_element_type=F32)
    if relu2:
        acc = jnp.square(jnp.maximum(acc, 0.0))
    o_ref[...] = acc.astype(o_ref.dtype)


def _matmul(a, w, out_dtype, tm, tn, relu2=False):
    m, kdim = a.shape
    n = w.shape[1]
    tm = min(tm, m)
    assert m % tm == 0 and n % tn == 0
    return pl.pallas_call(
        functools.partial(_matmul_kernel, relu2=relu2),
        out_shape=jax.ShapeDtypeStruct((m, n), out_dtype),
        grid=(m // tm, n // tn),
        in_specs=[pl.BlockSpec((tm, kdim), lambda i, j: (i, 0)), pl.BlockSpec((kdim, tn), lambda i, j: (0, j))],
        out_specs=pl.BlockSpec((tm, tn), lambda i, j: (i, j)),
        compiler_params=_cparams(("parallel", "parallel")),
        name="matmul",
    )(a, w)


def _matmul_norm_res_kernel(a_ref, w_ref, g_ref, res_ref, *rest, with_next):
    if with_next:
        gnext_ref, o_ref, unext_ref, acc_ref = rest
    else:
        o_ref, acc_ref = rest
    k = pl.program_id(1)

    @pl.when(k == 0)
    def _():
        acc_ref[...] = jnp.zeros_like(acc_ref)

    acc_ref[...] += jnp.dot(a_ref[...], w_ref[...], preferred_element_type=F32)

    @pl.when(k == pl.num_programs(1) - 1)
    def _():
        y = acc_ref[...]
        ms = jnp.mean(y * y, axis=-1, keepdims=True)
        out = res_ref[...] + y * lax.rsqrt(ms + RMS_EPS) * g_ref[...]
        o_ref[...] = out
        if with_next:
            ms2 = jnp.mean(out * out, axis=-1, keepdims=True)
            unext_ref[...] = (out * lax.rsqrt(ms2 + RMS_EPS) * gnext_ref[...]).astype(BF16)


def _matmul_norm_res(a, w, g, res, next_gain=None, tm=512, tk=2048):
    m, kdim = a.shape
    n = w.shape[1]
    tm = min(tm, m)
    assert m % tm == 0 and kdim % tk == 0
    with_next = next_gain is not None
    vec = pl.BlockSpec((1, n), lambda i, k: (0, 0))
    rowblk = pl.BlockSpec((tm, n), lambda i, k: (i, 0))
    in_specs = [pl.BlockSpec((tm, tk), lambda i, k: (i, k)), pl.BlockSpec((tk, n), lambda i, k: (k, 0)), vec, rowblk]
    args = [a, w, g[None, :], res]
    out_shape = jax.ShapeDtypeStruct((m, n), F32)
    out_specs = rowblk
    if with_next:
        in_specs.append(vec)
        args.append(next_gain[None, :])
        out_shape = (out_shape, jax.ShapeDtypeStruct((m, n), BF16))
        out_specs = (rowblk, rowblk)
    return pl.pallas_call(
        functools.partial(_matmul_norm_res_kernel, with_next=with_next),
        out_shape=out_shape,
        grid=(m // tm, kdim // tk),
        in_specs=in_specs,
        out_specs=out_specs,
        scratch_shapes=[pltpu.VMEM((tm, n), F32)],
        compiler_params=_cparams(("parallel", "arbitrary")),
        name="matmul_norm_res",
    )(*args)


def _gelu_tanh(x):
    return 0.5 * x * (1.0 + jnp.tanh(math.sqrt(2.0 / math.pi) * (x + 0.044715 * (x * x * x))))


def _nsa_compress_kernel(x_ref, w1_ref, w2_ref, pe_ref, cos_ref, sin_ref, o_ref, *, use_rope):
    x = x_ref[0, 0]
    half = x.shape[1]
    n_chunk = x.shape[0]
    a = jnp.dot(x, w1_ref[0:half, :], preferred_element_type=F32)
    b = jnp.dot(x, w1_ref[half:2 * half, :], preferred_element_type=F32)
    pe = jnp.broadcast_to(pe_ref[...], (SUBLANES, pe_ref.shape[1]))
    peb = _dot_exact_rhs_general(pe, w1_ref[...])[0:1, :]
    h = a + pltpu.roll(b, n_chunk - 1, 0) + peb
    y = _bdot(_gelu_tanh(h), w2_ref[...])
    if use_rope:
        lane = lax.broadcasted_iota(jnp.int32, y.shape, 1)
        y = y * cos_ref[0] + _rope_partner(y, lane) * sin_ref[0]
    o_ref[0, 0] = y.astype(o_ref.dtype)


def _dot_exact_rhs_general(x, w_bf):
    hi, mid, lo = _split3(x)
    d = lambda p: jnp.dot(p, w_bf, preferred_element_type=F32)
    return d(hi) + (d(mid) + d(lo))


def _nsa_compress(x2, w1, w2, pe_flat, cosc, sinc, use_rope):
    b, g, n_chunk, width = x2.shape
    dh = w2.shape[1]
    return pl.pallas_call(
        functools.partial(_nsa_compress_kernel, use_rope=use_rope),
        out_shape=jax.ShapeDtypeStruct((b, g, n_chunk, dh), BF16),
        grid=(b, g),
        in_specs=[
            pl.BlockSpec((1, 1, n_chunk, width), lambda i, j: (i, j, 0, 0)),
            pl.BlockSpec(w1.shape, lambda i, j: (0, 0)),
            pl.BlockSpec(w2.shape, lambda i, j: (0, 0)),
            pl.BlockSpec(pe_flat.shape, lambda i, j: (0, 0)),
            pl.BlockSpec((1, n_chunk, dh), lambda i, j: (i, 0, 0)),
            pl.BlockSpec((1, n_chunk, dh), lambda i, j: (i, 0, 0)),
        ],
        out_specs=pl.BlockSpec((1, 1, n_chunk, dh), lambda i, j: (i, j, 0, 0)),
        compiler_params=_cparams(("parallel", "parallel")),
        name="nsa_compress",
    )(x2, w1, w2, pe_flat, cosc, sinc)


def _nsa_cmp_kernel(q_ref, kc_ref, vc_ref, ov_ref, o_ref, sel_ref, *, tq, top, n_sel):
    i = pl.program_id(2)
    kc = kc_ref[0, 0]
    vc = vc_ref[0, 0]
    ncp = kc.shape[0]
    t = i * tq + lax.broadcasted_iota(jnp.int32, (tq, ncp), 0)
    c = lax.broadcasted_iota(jnp.int32, (tq, ncp), 1)
    valid = (c * NSA_CMP_STRIDE + (NSA_CMP_LEN - 1)) <= t
    psum = jnp.zeros((tq, ncp), F32)
    for h in range(NSA_HPG):
        q = q_ref[0, :, h * HEAD_DIM:(h + 1) * HEAD_DIM]
        s = jnp.where(valid, _dot_nt(q, kc), NEG)
        m = jnp.max(s, axis=1, keepdims=True)
        e = jnp.where(valid, jnp.exp2(s - m), 0.0)
        l = jnp.sum(e, axis=1, keepdims=True)
        p = e / jnp.where(l > 0.0, l, 1.0)
        o_ref[0, :, h * HEAD_DIM:(h + 1) * HEAD_DIM] = _bdot(p, vc).astype(o_ref.dtype)
        psum = psum + p
    imp = jnp.transpose(_dot_exact_rhs(psum, ov_ref[...]))[0:n_sel]
    jj = lax.broadcasted_iota(jnp.int32, (n_sel, tq), 0)
    blk_t = jnp.right_shift(i * tq + lax.broadcasted_iota(jnp.int32, (n_sel, tq), 1), SEL_SHIFT)
    forced = (jj == 0) | (jj == blk_t) | (jj == blk_t - 1)
    imp = jnp.where(forced, 1e9, jnp.where(jj > blk_t, -1.0, imp))
    ng = n_sel // SUBLANES
    groups = [imp[g * SUBLANES:(g + 1) * SUBLANES] for g in range(ng)]
    ranks = [jnp.zeros((SUBLANES, tq), F32) for _ in range(ng)]
    sub = lax.broadcasted_iota(jnp.int32, (SUBLANES, tq), 0)
    for ii in range(n_sel):
        gi, ri = divmod(ii, SUBLANES)
        row = jnp.broadcast_to(imp[ii:ii + 1, :], (SUBLANES, tq))
        for g in range(ng):
            if g > gi:
                beats = row >= groups[g]
            elif g < gi:
                beats = row > groups[g]
            else:
                beats = (row > groups[g]) | ((row == groups[g]) & (sub > ri))
            ranks[g] = ranks[g] + jnp.where(beats, 1.0, 0.0)
    chosen = jnp.where(jnp.concatenate(ranks, axis=0) < float(top), 1.0, 0.0)
    pad = sel_ref.shape[3] - n_sel
    if pad:
        chosen = jnp.concatenate([chosen, jnp.zeros((pad, tq), F32)], axis=0)
    sel_ref[0, 0] = jnp.transpose(chosen).astype(sel_ref.dtype)


def _nsa_cmp(proj_rope, kcmp, vcmp, overlap, n_sel, tq):
    b, s, _ = proj_rope.shape
    g = kcmp.shape[1]
    ncp = kcmp.shape[2]
    n_pad = overlap.shape[1]
    top = min(NSA_SEL_TOPK, n_sel)
    qw = NSA_HPG * HEAD_DIM
    return pl.pallas_call(
        functools.partial(_nsa_cmp_kernel, tq=tq, top=top, n_sel=n_sel),
        out_shape=(jax.ShapeDtypeStruct((b, s, NSA_WIDTH), BF16),
                   jax.ShapeDtypeStruct((b, g, s, n_pad), BF16)),
        grid=(b, g, s // tq),
        in_specs=[
            pl.BlockSpec((1, tq, qw), lambda bi, gi, i: (bi, i, ROPE_Q // qw + gi)),
            pl.BlockSpec((1, 1, ncp, HEAD_DIM), lambda bi, gi, i: (bi, gi, 0, 0)),
            pl.BlockSpec((1, 1, ncp, HEAD_DIM), lambda bi, gi, i: (bi, gi, 0, 0)),
            pl.BlockSpec(overlap.shape, lambda bi, gi, i: (0, 0)),
        ],
        out_specs=(pl.BlockSpec((1, tq, qw), lambda bi, gi, i: (bi, i, gi)),
                   pl.BlockSpec((1, 1, tq, n_pad), lambda bi, gi, i: (bi, gi, i, 0))),
        compiler_params=_cparams(("parallel", "parallel", "parallel")),
        name="nsa_cmp_select",
    )(proj_rope, kcmp, vcmp, overlap)


def _nsa_flash_kernel(qi_ref, kb_ref, first_ref, *refs, mode, tq, tk):
    if mode == "sel":
        q_ref, k_ref, v_ref, sel_ref, hot_ref, o_ref, m_ref, l_ref, acc_ref = refs
    else:
        q_ref, k_ref, v_ref, o_ref, m_ref, l_ref, acc_ref = refs
    n = pl.program_id(2)
    i = qi_ref[n]
    kb = kb_ref[n]

    @pl.when(first_ref[n] == 1)
    def _():
        m_ref[...] = jnp.full_like(m_ref, NEG)
        l_ref[...] = jnp.zeros_like(l_ref)
        acc_ref[...] = jnp.zeros_like(acc_ref)

    ngrp = k_ref.shape[2] // HEAD_DIM

    def step(diag):
        valid = None
        if mode == "win":
            rows = i * tq + lax.broadcasted_iota(jnp.int32, (tq, tk), 0)
            cols = kb * tk + lax.broadcasted_iota(jnp.int32, (tq, tk), 1)
            d = rows - cols
            valid = (d >= 0) & (d < NSA_WINDOW)
        elif diag:
            valid = (lax.broadcasted_iota(jnp.int32, (tq, tk), 0) >= lax.broadcasted_iota(jnp.int32, (tq, tk), 1))
        for gi in range(ngrp):
            k = k_ref[0, :, gi * HEAD_DIM:(gi + 1) * HEAD_DIM]
            if mode == "sel":
                penalty = ((sel_ref[0, gi].astype(F32) - 1.0) * (-NEG)).astype(BF16)
                k = jnp.concatenate([k, hot_ref[...]], axis=1)
            v_ones = jnp.concatenate([v_ref[0, :, gi * HEAD_DIM:(gi + 1) * HEAD_DIM],
                                      jnp.ones((tk, LANES), BF16)], axis=1)
            for hd in range(NSA_HPG):
                slot = gi * NSA_HPG + hd
                q = q_ref[0, :, slot * HEAD_DIM:(slot + 1) * HEAD_DIM]
                if mode == "sel":
                    q = jnp.concatenate([q, penalty], axis=1)
                s = _dot_nt(q, k)
                if valid is not None:
                    s = jnp.where(valid, s, NEG)
                m_old = m_ref[slot]
                m_new = jnp.maximum(m_old, jnp.max(s, axis=1, keepdims=True))
                alpha = jnp.exp2(m_old - m_new)
                p = jnp.exp2(s - jnp.concatenate([m_new] * (tk // LANES), axis=1))
                pv = jnp.dot(p.astype(BF16), v_ones, preferred_element_type=F32)
                l_ref[slot] = alpha * l_ref[slot] + pv[:, HEAD_DIM:]
                acc_ref[slot] = alpha * acc_ref[slot] + pv[:, 0:HEAD_DIM]
                m_ref[slot] = m_new

    if mode == "sel":
        pl.when(kb < i)(functools.partial(step, False))
        pl.when(kb == i)(functools.partial(step, True))
    else:
        step(True)

    @pl.when(kb == i)
    def _():
        for slot in range(ngrp * NSA_HPG):
            o_ref[0, :, slot * HEAD_DIM:(slot + 1) * HEAD_DIM] = (acc_ref[slot] / l_ref[slot]).astype(o_ref.dtype)


def _nsa_flash(proj_rope, proj_plain, sel, mode, tq, tk):
    b, s, _ = proj_rope.shape
    g = NSA_GROUPS
    gpb = g
    qw = gpb * NSA_HPG * HEAD_DIM
    kw = gpb * HEAD_DIM
    nq = s // tq
    assert tq == tk
    if mode == "sel":
        back = nq
        koff, voff = ROPE_KS, PL_VS
    else:
        assert NSA_WINDOW % tk == 0
        back = NSA_WINDOW // tk
        koff, voff = ROPE_KW, PL_VW
    assert g % gpb == 0 and ROPE_Q % qw == 0 and koff % kw == 0 and voff % kw == 0
    qi, kb, first = _causal_pairs(nq, back)
    in_specs = [
        pl.BlockSpec((1, tq, qw), lambda bi, gi, n, qi, kb, fs: (bi, qi[n], ROPE_Q // qw + gi)),
        pl.BlockSpec((1, tk, kw), lambda bi, gi, n, qi, kb, fs: (bi, kb[n], koff // kw + gi)),
        pl.BlockSpec((1, tk, kw), lambda bi, gi, n, qi, kb, fs: (bi, kb[n], voff // kw + gi)),
    ]
    args = [proj_rope, proj_rope, proj_plain]
    if mode == "sel":
        n_pad = sel.shape[3]
        in_specs.append(pl.BlockSpec((1, gpb, tq, n_pad), lambda bi, gi, n, qi, kb, fs: (bi, gi, qi[n], 0)))
        in_specs.append(pl.BlockSpec((tk, n_pad), lambda bi, gi, n, qi, kb, fs: (kb[n], 0)))
        block_of_key = jnp.arange(s, dtype=jnp.int32)[:, None] // NSA_SEL_BLOCK
        onehot = (block_of_key == jnp.arange(n_pad, dtype=jnp.int32)[None, :]).astype(BF16)
        args += [sel, onehot]
    return pl.pallas_call(
        functools.partial(_nsa_flash_kernel, mode=mode, tq=tq, tk=tk),
        out_shape=jax.ShapeDtypeStruct((b, s, NSA_WIDTH), BF16),
        grid_spec=pltpu.PrefetchScalarGridSpec(
            num_scalar_prefetch=3,
            grid=(b, g // gpb, qi.shape[0]),
            in_specs=in_specs,
            out_specs=pl.BlockSpec((1, tq, qw), lambda bi, gi, n, qi, kb, fs: (bi, qi[n], gi)),
            scratch_shapes=[
                pltpu.VMEM((gpb * NSA_HPG, tq, LANES), F32),
                pltpu.VMEM((gpb * NSA_HPG, tq, LANES), F32),
                pltpu.VMEM((gpb * NSA_HPG, tq, HEAD_DIM), F32),
            ],
        ),
        compiler_params=_cparams(("parallel", "parallel", "arbitrary")),
        name="nsa_flash_" + mode,
    )(qi, kb, first, *args)


def _causal_pairs(nq, back):
    qi, kb, first = [], [], []
    for i in range(nq):
        lo = max(0, i - back)
        for j in range(lo, i + 1):
            qi.append(i)
            kb.append(j)
            first.append(1 if j == lo else 0)
    as_i32 = lambda z: jnp.asarray(np.asarray(z, np.int32))
    return as_i32(qi), as_i32(kb), as_i32(first)


def _diff_flash_kernel(qi_ref, kb_ref, first_ref, q_ref, k_ref, v_ref, lam_ref, sub_ref, o_ref, m_ref, l_ref,
                       acc_ref, *, tq, tk, lam_init):
    n = pl.program_id(2)
    i = qi_ref[n]
    kb = kb_ref[n]

    @pl.when(first_ref[n] == 1)
    def _():
        m_ref[...] = jnp.full_like(m_ref, NEG)
        l_ref[...] = jnp.zeros_like(l_ref)
        acc_ref[...] = jnp.zeros_like(acc_ref)

    hpb = o_ref.shape[2] // DIFF_VDIM

    def step(masked):
        if masked:
            valid = (lax.broadcasted_iota(jnp.int32, (tq, tk), 0) >= lax.broadcasted_iota(jnp.int32, (tq, tk), 1))
        for hd in range(hpb):
            v = v_ref[0, :, hd * DIFF_VDIM:(hd + 1) * DIFF_VDIM]
            for mp in range(2):
                slot = 2 * hd + mp
                q = q_ref[0, :, slot * HEAD_DIM:(slot + 1) * HEAD_DIM]
                k = k_ref[0, :, slot * HEAD_DIM:(slot + 1) * HEAD_DIM]
                s = _dot_nt(q, k)
                if masked:
                    s = jnp.where(valid, s, NEG)
                m_old = m_ref[slot]
                m_new = jnp.maximum(m_old, jnp.max(s, axis=1, keepdims=True))
                alpha = jnp.exp2(m_old - m_new)
                p = jnp.exp2(s - jnp.concatenate([m_new] * (tk // LANES), axis=1))
                l_ref[slot] = alpha * l_ref[slot] + jnp.sum(p, axis=1, keepdims=True)
                acc_ref[slot] = (jnp.concatenate([alpha] * (DIFF_VDIM // LANES), axis=1) * acc_ref[slot]
                                 + _bdot(p, v))
                m_ref[slot] = m_new

    pl.when(kb < i)(functools.partial(step, False))
    pl.when(kb == i)(functools.partial(step, True))

    @pl.when(kb == i)
    def _():
        lam = lam_ref[...]
        lam_full = (jnp.exp(jnp.sum(lam[0:1, :] * lam[1:2, :], axis=1, keepdims=True))
                    - jnp.exp(jnp.sum(lam[2:3, :] * lam[3:4, :], axis=1, keepdims=True)) + lam_init)
        wide = lambda x: jnp.concatenate([x] * (DIFF_VDIM // LANES), axis=1)
        for hd in range(hpb):
            o = (acc_ref[2 * hd] / wide(l_ref[2 * hd])
                 - lam_full * (acc_ref[2 * hd + 1] / wide(l_ref[2 * hd + 1])))
            ms = jnp.mean(o * o, axis=-1, keepdims=True)
            o = o * lax.rsqrt(ms + 1e-5) * sub_ref[...]
            o_ref[0, :, hd * DIFF_VDIM:(hd + 1) * DIFF_VDIM] = (o * (1.0 - lam_init)).astype(o_ref.dtype)


def _diff_flash(proj_rope, proj_plain, lam, subln, layer, tq, tk):
    b, s, _ = proj_rope.shape
    assert tq == tk
    nq = s // tq
    lam_init = 0.8 - 0.6 * math.exp(-0.3 * layer)
    hpb = DIFF_HEADS
    w = hpb * DIFF_VDIM
    assert DIFF_HEADS % hpb == 0 and ROPE_DQ % w == 0 and ROPE_DK % w == 0 and PL_DV % w == 0
    qi, kb, first = _causal_pairs(nq, nq)
    return pl.pallas_call(
        functools.partial(_diff_flash_kernel, tq=tq, tk=tk, lam_init=lam_init),
        out_shape=jax.ShapeDtypeStruct((b, s, DIFF_WIDTH), BF16),
        grid_spec=pltpu.PrefetchScalarGridSpec(
            num_scalar_prefetch=3,
            grid=(b, DIFF_HEADS // hpb, qi.shape[0]),
            in_specs=[
                pl.BlockSpec((1, tq, w), lambda bi, h, n, qi, kb, fs: (bi, qi[n], ROPE_DQ // w + h)),
                pl.BlockSpec((1, tk, w), lambda bi, h, n, qi, kb, fs: (bi, kb[n], ROPE_DK // w + h)),
                pl.BlockSpec((1, tk, w), lambda bi, h, n, qi, kb, fs: (bi, kb[n], PL_DV // w + h)),
                pl.BlockSpec(lam.shape, lambda bi, h, n, qi, kb, fs: (0, 0)),
                pl.BlockSpec((1, DIFF_VDIM), lambda bi, h, n, qi, kb, fs: (0, 0)),
            ],
            out_specs=pl.BlockSpec((1, tq, w), lambda bi, h, n, qi, kb, fs: (bi, qi[n], h)),
            scratch_shapes=[
                pltpu.VMEM((2 * hpb, tq, LANES), F32),
                pltpu.VMEM((2 * hpb, tq, LANES), F32),
                pltpu.VMEM((2 * hpb, tq, DIFF_VDIM), F32),
            ],
        ),
        compiler_params=_cparams(("parallel", "parallel", "arbitrary")),
        name="diff_flash",
    )(qi, kb, first, proj_rope, proj_rope, proj_plain, lam, subln[None, :])


def _bmm(a, b):
    return jnp.einsum("umk,ukn->umn", a.astype(BF16), b.astype(BF16), preferred_element_type=F32)


def _bmm_nt(a, b):
    return jnp.einsum("umk,unk->umn", a.astype(BF16), b.astype(BF16), preferred_element_type=F32)


def _rwkv_scan_kernel(r_ref, k_ref, v_ref, gd_ref, wa_ref, mur_ref, muk_ref, muv_ref, mugd_ref, muwa_ref,
                      wdec_ref, wrate_ref, gup_ref, w0_ref, a0_ref, kk_ref, ka_ref, rk_ref, lnw_ref, lnb_ref,
                      o_ref, st_ref, last_ref, lastgd_ref, *, ts, chunk):
    @pl.when(pl.program_id(2) == 0)
    def _():
        st_ref[...] = jnp.zeros_like(st_ref)
        last_ref[...] = jnp.zeros_like(last_ref)
        lastgd_ref[...] = jnp.zeros_like(lastgd_ref)

    def shifted(x_ref, mu_ref, carry):
        x = x_ref[0]
        row = lax.broadcasted_iota(jnp.int32, x.shape, 0)
        prev = jnp.where(row == 0, carry, pltpu.roll(x, 1, 0))
        return x + (prev - x) * mu_ref[...]

    r = shifted(r_ref, mur_ref, last_ref[0, 0:1, :])
    k = shifted(k_ref, muk_ref, last_ref[1, 0:1, :])
    v = shifted(v_ref, muv_ref, last_ref[2, 0:1, :])
    wa = shifted(wa_ref, muwa_ref, last_ref[3, 0:1, :])
    gd = shifted(gd_ref, mugd_ref, lastgd_ref[0:1, :])
    for n, ref in enumerate((r_ref, k_ref, v_ref, wa_ref)):
        last_ref[n, 0:1, :] = ref[0, ts - 1:ts, :]
    lastgd_ref[0:1, :] = gd_ref[0, ts - 1:ts, :]

    wa = jnp.where(lax.broadcasted_iota(jnp.int32, wa.shape, 1) < RWKV_W_LORA, jnp.tanh(wa), wa)
    lw = -math.exp(-0.5) * jax.nn.sigmoid(w0_ref[...] + _bdot(wa, wdec_ref[...]))
    a = jax.nn.sigmoid(a0_ref[...] + _bdot(wa, wrate_ref[...]))
    g = _bdot(jax.nn.sigmoid(gd), gup_ref[...])

    c = chunk
    nc = ts // c
    head0 =lax.broadcasted_iota(jnp.int32, (ts, LANES), 1) < RWKV_HEAD

    def seg_sum(x):
        s0 = jnp.sum(jnp.where(head0, x, 0.0), axis=1, keepdims=True)
        s1 = jnp.sum(jnp.where(head0, 0.0, x), axis=1, keepdims=True)
        return jnp.where(head0, s0, s1)

    kk = k * kk_ref[...]
    kap = kk / jnp.maximum(jnp.sqrt(seg_sum(kk * kk)), 1e-12)
    kmod = k * (1.0 + (a - 1.0) * ka_ref[...])
    bvec = kap * a

    ri = lax.broadcasted_iota(jnp.int32, (ts, ts), 0)
    ci = lax.broadcasted_iota(jnp.int32, (ts, ts), 1)
    same_chunk = jnp.right_shift(ri, c.bit_length() - 1) == jnp.right_shift(ci, c.bit_length() - 1)
    tri = jnp.where(same_chunk & (ri >= ci), 1.0, 0.0).astype(BF16)
    cum = _dot_exact_lhs(tri, lw)
    g_incl = jnp.exp(cum)
    g_inv = jnp.exp(-cum)
    split = lambda x: x.reshape(nc, c, LANES)
    g_last = [g_incl[(ch + 1) * c - 1:(ch + 1) * c, :] for ch in range(nc)]
    g_last_rows = jnp.concatenate([jnp.broadcast_to(gl, (c, LANES)) for gl in g_last], axis=0)
    kh = split(kmod * g_inv)
    bh = split(bvec * g_inv)
    kg = split(kmod * g_inv * g_last_rows)
    bg = split(bvec * g_inv * g_last_rows)
    kaph = kap * jnp.exp(cum - lw)
    rh = r * g_incl
    v3 = split(v)

    assert 2 * c == LANES
    h0 = lax.broadcasted_iota(jnp.int32, (nc, c, LANES), 2) < RWKV_HEAD
    stack = lambda y: jnp.concatenate([jnp.where(h0, y, 0.0), jnp.where(h0, 0.0, y)], axis=1)
    kap3 = split(kaph)
    r3 = split(rh)
    gram = _bmm_nt(jnp.concatenate([kap3, r3], axis=1),
                   jnp.concatenate([stack(bh), stack(kh)], axis=1))
    row = lax.broadcasted_iota(jnp.int32, (c, 4 * c), 0)
    col = lax.broadcasted_iota(jnp.int32, (c, 4 * c), 1) & (c - 1)
    top = jnp.where(row > col, gram[:, 0:c, :], 0.0)
    abk = jnp.where(row >= col, gram[:, c:2 * c, :], 0.0)
    lb = top[:, :, 0:LANES]
    lk = top[:, :, LANES:2 * LANES]
    tm = -lb
    q = _bmm(lb, stack(lb))
    n = 2
    while True:
        tm = tm + q + _bmm(tm, stack(q))
        n *= 2
        if n >= c:
            break
        q = _bmm(q, stack(q))
    kap_t = kap3 + _bmm(tm, stack(kap3))
    lkv = _bmm(lk, stack(v3))
    z0 = -(lkv + _bmm(tm, stack(lkv)))
    r_t = r3 - _bmm(abk[:, :, 0:LANES], stack(kap_t))
    y0 = _bmm(abk, jnp.concatenate([stack(z0), stack(v3)], axis=1))

    br = lax.broadcasted_iota(jnp.int32, (LANES, LANES), 0)
    bc = lax.broadcasted_iota(jnp.int32, (LANES, LANES), 1)
    blockdiag = (br < RWKV_HEAD) == (bc < RWKV_HEAD)
    state = st_ref[...]
    ys = []
    for ch in range(nc):
        trans = jnp.where(blockdiag, -_dot_tn(bg[ch], kap_t[ch]), 0.0)
        trans = trans + jnp.where(br == bc, jnp.broadcast_to(g_last[ch], (LANES, LANES)), 0.0)
        inject = jnp.where(blockdiag, _dot_tn(jnp.concatenate([bg[ch], kg[ch]], axis=0),
                                              jnp.concatenate([z0[ch], v3[ch]], axis=0)), 0.0)
        ys.append(_bdot(r_t[ch], state) + y0[ch])
        state = _dot3(trans, state) + inject
    st_ref[...] = state
    y = jnp.concatenate(ys, axis=0)

    mean = seg_sum(y) * (1.0 / RWKV_HEAD)
    yc = y - mean
    var = seg_sum(yc * yc) * (1.0 / RWKV_HEAD)
    yn = yc * lax.rsqrt(var + RWKV_LNX_EPS) * lnw_ref[...] + lnb_ref[...]
    bonus = seg_sum(r * kmod * rk_ref[...]) * v
    o_ref[0] = ((yn + bonus) * g).astype(o_ref.dtype)


def _rwkv_scan(proj3, mu_p, wwa, gup, w0, a0, k_k, k_a, r_k, lnx_w, lnx_b, ts=512, chunk=64):
    b, s, _ = proj3.shape
    w = RWKV_WIDTH
    ts = min(ts, s)
    assert s % ts == 0 and ts % chunk == 0
    npair = w // LANES
    col = lambda off: pl.BlockSpec((1, ts, LANES), lambda bi, j, t: (bi, t, off // LANES + j))
    mucol = lambda off: pl.BlockSpec((1, LANES), lambda bi, j, t: (0, off // LANES + j))
    vec = pl.BlockSpec((1, LANES), lambda bi, j, t: (0, j))
    in_specs = [
        col(0), col(w), col(2 * w),
        pl.BlockSpec((1, ts, RW_G_PAD), lambda bi, j, t: (bi, t, RW_GD // RW_G_PAD)),
        pl.BlockSpec((1, ts, RW_LORA_PAD), lambda bi, j, t: (bi, t, RW_WA // RW_LORA_PAD)),
        mucol(0), mucol(w), mucol(2 * w),
        pl.BlockSpec((1, RW_G_PAD), lambda bi, j, t: (0, RW_GD // RW_G_PAD)),
        pl.BlockSpec((1, RW_LORA_PAD), lambda bi, j, t: (0, RW_WA // RW_LORA_PAD)),
        pl.BlockSpec((RW_LORA_PAD, LANES), lambda bi, j, t: (0, j)),
        pl.BlockSpec((RW_LORA_PAD, LANES), lambda bi, j, t: (0, npair + j)),
        pl.BlockSpec((RW_G_PAD, LANES), lambda bi, j, t: (0, j)),
    ] + [vec] * 7
    row = lambda z: z[None, :]
    return pl.pallas_call(
        functools.partial(_rwkv_scan_kernel, ts=ts, chunk=chunk),
        out_shape=jax.ShapeDtypeStruct((b, s, w), BF16),
        grid=(b, npair, s // ts),
        in_specs=in_specs,
        out_specs=pl.BlockSpec((1, ts, LANES), lambda bi, j, t: (bi, t, j)),
        scratch_shapes=[
            pltpu.VMEM((LANES, LANES), F32),
            pltpu.VMEM((4, SUBLANES, LANES), F32),
            pltpu.VMEM((SUBLANES, RW_G_PAD), F32),
        ],
        compiler_params=_cparams(("parallel", "parallel", "arbitrary")),
        name="rwkv_scan",
    )(proj3, proj3, proj3, proj3, proj3, mu_p, mu_p, mu_p, mu_p, mu_p, wwa, wwa, gup,
      row(w0), row(a0), row(k_k), row(k_a), row(r_k), row(lnx_w), row(lnx_b))


def _merge_kernel(oc_ref, os_ref, ow_ref, ng_ref, yd_ref, yr_ref, wb_ref, g0_ref, g1_ref, g2_ref, bias_ref,
                  o_ref, yn_ref):
    @pl.when(pl.program_id(1) == 0)
    def _():
        gates = jax.nn.sigmoid(ng_ref[...])
        for hd in range(NSA_HEADS):
            sl = slice(hd * HEAD_DIM, (hd + 1) * HEAD_DIM)
            o = (gates[:, 3 * hd:3 * hd + 1] * oc_ref[:, sl]
                 + gates[:, 3 * hd + 1:3 * hd + 2] * os_ref[:, sl]
                 + gates[:, 3 * hd + 2:3 * hd + 3] * ow_ref[:, sl])
            yn_ref[:, sl] = o.astype(BF16)

    branches = (yn_ref, yd_ref, yr_ref)
    graw = (g0_ref, g1_ref, g2_ref)
    acc = None
    for bi in range(N_BRANCH):
        gate = jax.nn.sigmoid(graw[bi][...] + bias_ref[bi])
        term = gate * jnp.dot(branches[bi][...], wb_ref[bi], preferred_element_type=F32)
        acc = term if acc is None else acc + term
    o_ref[...] = acc.astype(o_ref.dtype)


def _merge(o_cmp, o_slc, o_win, proj_f32, proj_plain, y_diff, y_rwkv, w_branch, b_gate, tm=1024, tn=512):
    m = o_cmp.shape[0]
    n = w_branch.shape[2]
    tm = min(tm, m)
    assert m % tm == 0 and n % tn == 0 and PL_BG % tn == 0
    bw = BRANCH_WIDTH
    rowblk = lambda: pl.BlockSpec((tm, bw), lambda i, j: (i, 0))
    gate_spec = lambda bi: pl.BlockSpec((tm, tn), lambda i, j: (i, PL_BG // tn + bi * (n // tn) + j))
    return pl.pallas_call(
        _merge_kernel,
        out_shape=jax.ShapeDtypeStruct((m, n), BF16),
        grid=(m // tm, n // tn),
        in_specs=[
            rowblk(), rowblk(), rowblk(),
            pl.BlockSpec((tm, LANES), lambda i, j: (i, F_NG // LANES)),
            rowblk(), rowblk(),
            pl.BlockSpec((N_BRANCH, bw, tn), lambda i, j: (0, 0, j)),
            gate_spec(0), gate_spec(1), gate_spec(2),
            pl.BlockSpec((N_BRANCH, 1, tn), lambda i, j: (0, 0, j)),
        ],
        out_specs=pl.BlockSpec((tm, tn), lambda i, j: (i, j)),
        scratch_shapes=[pltpu.VMEM((tm, bw), BF16)],
        compiler_params=_cparams(("parallel", "arbitrary")),
        name="gated_merge",
    )(o_cmp, o_slc, o_win, proj_f32, y_diff, y_rwkv, w_branch, proj_plain, proj_plain, proj_plain,
      b_gate[:, None, :])


def _pack_layer_weights(w_in, w_gate, rwkv_mu, w_up_lora, a_up_lora, g_up_lora):
    d = w_in.shape[0]
    nsa_sizes = (NSA_WIDTH,) + (NSA_KV,) * 6 + (3 * NSA_HEADS,)
    diff_sizes = (DIFF_QK, DIFF_QK, DIFF_WIDTH)
    rw_sizes = (RWKV_WIDTH,) * 3 + (RWKV_W_LORA, RWKV_A_LORA, RWKV_G_LORA)
    offs = np.cumsum((0,) + nsa_sizes + diff_sizes + rw_sizes)
    seg = [w_in[:, offs[i]:offs[i + 1]] for i in range(len(offs) - 1)]
    q, kc, vc, ks, vs, kw, vw, ng, dq, dk, dv, rr, rk, rv, wd, ad, gd = seg
    w_rope = jnp.concatenate([q, dq, dk, ks, kw], axis=1).astype(BF16)
    w_plain = jnp.concatenate([kc, vc, vs, vw, dv] + [w_gate[bi] for bi in range(N_BRANCH)], axis=1).astype(BF16)
    zpad = lambda n: jnp.zeros((d, n), w_in.dtype)
    w_f32 = jnp.concatenate(
        [rr, rk, rv, gd, zpad(RW_G_PAD - RWKV_G_LORA), wd, ad, ng, zpad(LANES - 3 * NSA_HEADS)],
        axis=1).astype(BF16)
    mu_rkv, mu_wa, mu_gd = jnp.split(rwkv_mu, [3 * RWKV_WIDTH, 3 * RWKV_WIDTH + RW_LORA_PAD])
    mu_p = jnp.concatenate([mu_rkv, mu_gd, jnp.zeros((RW_G_PAD - RWKV_G_LORA,), rwkv_mu.dtype), mu_wa])[None, :]
    zz = jnp.zeros((RWKV_W_LORA, RWKV_WIDTH), w_up_lora.dtype)
    wwa = jnp.concatenate([jnp.concatenate([w_up_lora, zz], axis=1),
                           jnp.concatenate([zz, a_up_lora], axis=1)], axis=0).astype(BF16)
    gup = jnp.concatenate([g_up_lora, jnp.zeros((RW_G_PAD - RWKV_G_LORA, RWKV_WIDTH), g_up_lora.dtype)],
                          axis=0).astype(BF16)
    return w_rope, w_plain, w_f32, mu_p, wwa, gup


def _overlap_matrix(n_cp, n_sel):
    c_start = np.arange(n_cp) * NSA_CMP_STRIDE
    c_end = c_start + NSA_CMP_LEN - 1
    j_start = np.arange(n_sel) * NSA_SEL_BLOCK
    ov = (c_start[:, None] <= j_start[None, :] + NSA_SEL_BLOCK - 1) & (c_end[:, None] >= j_start[None, :])
    ov = np.pad(ov.astype(np.float32), ((0, 0), (0, -n_sel % LANES)))
    return jnp.asarray(ov).astype(BF16)


def _chunk_rows(z, groups):
    b, s, _ = z.shape
    z = z.reshape(b, s // NSA_CMP_STRIDE, NSA_CMP_STRIDE, groups, HEAD_DIM)
    return z.transpose(0, 3, 1, 2, 4).reshape(b, groups, s // NSA_CMP_STRIDE, NSA_CMP_STRIDE * HEAD_DIM)


def _nsa_branches(proj_rope, proj_plain, positions, cmp_pos, cmp_w1, cmp_w2, tq):
    b, s, _ = proj_rope.shape
    n_chunk = s // NSA_CMP_STRIDE
    n_sel = s // NSA_SEL_BLOCK
    kc2 = _chunk_rows(proj_plain[:, :, PL_KC:PL_KC + NSA_KV], NSA_GROUPS)
    vc2 = _chunk_rows(proj_plain[:, :, PL_VC:PL_VC + NSA_KV], NSA_GROUPS)
    cmp_end = np.minimum(np.arange(n_chunk) * NSA_CMP_STRIDE + NSA_CMP_LEN - 1, s - 1)
    pos_c = jnp.take(positions, jnp.asarray(cmp_end), axis=1)
    cosc, sinc = _rope_tables(pos_c.reshape(-1))
    cosc = cosc.reshape(b, n_chunk, LANES)
    sinc = sinc.reshape(b, n_chunk, LANES)
    w1 = cmp_w1.astype(BF16)
    w2 = cmp_w2.astype(BF16)
    pe = cmp_pos.reshape(2, 1, NSA_CMP_LEN * HEAD_DIM)
    kcmp = _nsa_compress(kc2, w1[0], w2[0], pe[0], cosc, sinc, True)
    vcmp = _nsa_compress(vc2, w1[1], w2[1], pe[1], cosc, sinc, False)
    o_cmp, sel = _nsa_cmp(proj_rope, kcmp, vcmp, _overlap_matrix(n_chunk, n_sel), n_sel, tq)
    o_slc = _nsa_flash(proj_rope, proj_plain, sel, "sel", tq, tq)
    o_win = _nsa_flash(proj_rope, proj_plain, None, "win", tq, tq)
    return o_cmp, o_slc, o_win


def _layer(x2, b, s, layer, cosf, sinf, positions, p):
    m = b * s
    w_rope, w_plain, w_f32, mu_p, wwa, gup = _pack_layer_weights(
        p["w_in"], p["w_gate"], p["rwkv_mu"], p["rwkv_w_up"], p["rwkv_a_up"], p["rwkv_g_up"])
    scale = HEAD_DIM ** -0.5 * math.log2(math.e)
    colscale = jnp.concatenate([jnp.full((ROPE_SCALED,), scale, F32),
                                jnp.ones((ROPE_COLS - ROPE_SCALED,), F32)])[None, :]
    g_pre = p["norm_pre_mix"]
    proj_rope, u = _norm_matmul_rope(x2, g_pre, w_rope, cosf, sinf, colscale, tm=512, tn=ROPE_COLS // 2)
    proj_plain = _matmul(u, w_plain, BF16, tm=2048, tn=1024)
    proj_f32 = _matmul(u, w_f32, F32, tm=1024, tn=F32_COLS // 2)
    pr3 = proj_rope.reshape(b, s, ROPE_COLS)
    pp3 = proj_plain.reshape(b, s, PLAIN_COLS)

    tq = min(512, s)
    o_cmp, o_slc, o_win = _nsa_branches(pr3, pp3, positions, p["nsa_cmp_pos"], p["nsa_cmp_w1"], p["nsa_cmp_w2"], tq)
    y_diff = _diff_flash(pr3, pp3, p["diff_lambda"], p["diff_subln"], layer, tq, tq)

    y_rwkv = _rwkv_scan(proj_f32.reshape(b, s, F32_COLS), mu_p, wwa, gup, p["rwkv_w0"], p["rwkv_a0"],
                        p["rwkv_k_k"], p["rwkv_k_a"], p["rwkv_r_k"].reshape(-1), p["rwkv_lnx_w"], p["rwkv_lnx_b"])

    merged = _merge(o_cmp.reshape(m, NSA_WIDTH), o_slc.reshape(m, NSA_WIDTH), o_win.reshape(m, NSA_WIDTH),
                    proj_f32, proj_plain, y_diff.reshape(m, DIFF_WIDTH), y_rwkv.reshape(m, RWKV_WIDTH),
                    p["w_branch"].astype(BF16), p["b_gate"])
    x2, u_mlp = _matmul_norm_res(merged, p["w_out"].astype(BF16), p["norm_post_mix"], x2,
                                 next_gain=p["norm_pre_mlp"])
    hidden = _matmul(u_mlp, p["w_up"].astype(BF16), BF16, tm=2048, tn=1024, relu2=True)
    x2 = _matmul_norm_res(hidden, p["w_down"].astype(BF16), p["norm_post_mlp"], x2)
    return x2


def kernel(x, positions, norm_pre_mix, norm_post_mix, norm_pre_mlp, norm_post_mlp, w_in, nsa_cmp_pos, nsa_cmp_w1, nsa_cmp_w2, diff_lambda, diff_subln, rwkv_mu, rwkv_w0, rwkv_w_up, rwkv_a0, rwkv_a_up, rwkv_g_up, rwkv_k_k, rwkv_k_a, rwkv_r_k, rwkv_lnx_w, rwkv_lnx_b, w_gate, b_gate, w_branch, w_out, w_up, w_down):
    b, s, d = x.shape
    depth = w_in.shape[0]
    stacked = dict(
        norm_pre_mix=norm_pre_mix, norm_post_mix=norm_post_mix, norm_pre_mlp=norm_pre_mlp,
        norm_post_mlp=norm_post_mlp, w_in=w_in, nsa_cmp_pos=nsa_cmp_pos, nsa_cmp_w1=nsa_cmp_w1,
        nsa_cmp_w2=nsa_cmp_w2, diff_lambda=diff_lambda, diff_subln=diff_subln, rwkv_mu=rwkv_mu,
        rwkv_w0=rwkv_w0, rwkv_w_up=rwkv_w_up, rwkv_a0=rwkv_a0, rwkv_a_up=rwkv_a_up, rwkv_g_up=rwkv_g_up,
        rwkv_k_k=rwkv_k_k, rwkv_k_a=rwkv_k_a, rwkv_r_k=rwkv_r_k, rwkv_lnx_w=rwkv_lnx_w,
        rwkv_lnx_b=rwkv_lnx_b, w_gate=w_gate, b_gate=b_gate, w_branch=w_branch, w_out=w_out,
        w_up=w_up, w_down=w_down)
    cosf, sinf = _rope_tables(positions.reshape(-1))
    x2 = x.reshape(b * s, d)
    for layer in range(depth):
        x2 = _layer(x2, b, s, layer, cosf, sinf, positions, {n: a[layer] for n, a in stacked.items()})
    return x2.reshape(b, s, d)
```

```python
import functools
import math

import jax
import jax.numpy as jnp
import numpy as np
from jax import lax
from jax.experimental import pallas as pl
from jax.experimental.pallas import tpu as pltpu

F32 = jnp.float32
BF16 = jnp.bfloat16

D_MODEL = 2048
RMS_EPS = 1e-6
ROPE_THETA = 500000.0
HEAD_DIM = 128
ROT_HALF = HEAD_DIM // 8
NSA_HEADS = 8
NSA_GROUPS = 2
NSA_HPG = NSA_HEADS // NSA_GROUPS
NSA_CMP_LEN = 32
NSA_CMP_STRIDE = 16
NSA_CMP_HIDDEN = 256
NSA_SEL_BLOCK = 64
SEL_SHIFT = NSA_SEL_BLOCK.bit_length() - 1
NSA_SEL_TOPK = 16
NSA_WINDOW = 512
NSA_WIDTH = NSA_HEADS * HEAD_DIM
NSA_KV = NSA_GROUPS * HEAD_DIM
DIFF_HEADS = 4
DIFF_VDIM = 2 * HEAD_DIM
DIFF_WIDTH = DIFF_HEADS * DIFF_VDIM
DIFF_QK = 2 * DIFF_HEADS * HEAD_DIM
RWKV_HEAD = 64
RWKV_WIDTH = 1024
RWKV_W_LORA = 64
RWKV_A_LORA = 64
RWKV_G_LORA = 160
RWKV_LNX_EPS = 64e-5
N_BRANCH = 3
BRANCH_WIDTH = 1024

LANES = 128
SUBLANES = 8
VMEM_LIMIT_BYTES = 56 * 1024 * 1024

NEG = -1e30

ROPE_Q, ROPE_DQ, ROPE_DK, ROPE_KS, ROPE_KW = 0, 1024, 2048, 3072, 3328
ROPE_SCALED = ROPE_DK
ROPE_COLS = 3584
PL_KC, PL_VC, PL_VS, PL_VW, PL_DV, PL_BG = 0, 256, 512, 768, 1024, 2048
PLAIN_COLS = PL_BG + N_BRANCH * D_MODEL
RW_LORA_PAD = 128
RW_G_PAD = 256
RW_GD = 3 * RWKV_WIDTH
RW_WA = RW_GD + RW_G_PAD
RW_COLS = RW_WA + RW_LORA_PAD
F_NG = RW_COLS
F32_COLS = RW_COLS + LANES


def _cparams(sem):
    return pltpu.CompilerParams(dimension_semantics=sem, vmem_limit_bytes=VMEM_LIMIT_BYTES)


def _bdot(a, b):
    return jnp.dot(a.astype(BF16), b.astype(BF16), preferred_element_type=F32)


def _dot_nt(a, b):
    return lax.dot_general(a.astype(BF16), b.astype(BF16), (((1,), (1,)), ((), ())),
                           preferred_element_type=F32)


def _dot_tn(a, b):
    return lax.dot_general(a.astype(BF16), b.astype(BF16), (((0,), (0,)), ((), ())),
                           preferred_element_type=F32)


def _split2(x):
    hi = x.astype(BF16)
    lo = (x - hi.astype(F32)).astype(BF16)
    return hi, lo


def _split3(x):
    hi = x.astype(BF16)
    r1 = x - hi.astype(F32)
    mid = r1.astype(BF16)
    lo = (r1 - mid.astype(F32)).astype(BF16)
    return hi, mid, lo


def _dot_exact_lhs(a_bf, x):
    hi, mid, lo = _split3(x)
    d = lambda p: jnp.dot(a_bf, p, preferred_element_type=F32)
    return d(hi) + (d(mid) + d(lo))


def _dot_exact_rhs(x, b_bf):
    hi, mid, lo = _split3(x)
    d = lambda p: jnp.dot(p, b_bf, preferred_element_type=F32)
    return d(hi) + (d(mid) + d(lo))


def _dot3(a, b):
    ah, al = _split2(a)
    bh, bl = _split2(b)
    d = lambda p, q: jnp.dot(p, q, preferred_element_type=F32)
    return d(ah, bh) + (d(ah, bl) + d(al, bh))


def _rope_partner(z, lane):
    return jnp.where(lane < ROT_HALF, pltpu.roll(z, LANES - ROT_HALF, 1), pltpu.roll(z, ROT_HALF, 1))


def _rope_table_kernel(pos_ref, invf_ref, sign_ref, cos_ref, sin_ref):
    ang = pos_ref[...].astype(F32) * invf_ref[...]
    cos_ref[...] = jnp.cos(ang)
    sin_ref[...] = jnp.sin(ang) * sign_ref[...]


def _rope_tables(pos_flat):
    n = pos_flat.shape[0]
    half = ROT_HALF
    inv_freq = ROPE_THETA ** (-jnp.arange(half, dtype=F32) / half)
    zeros = jnp.zeros((LANES - 2 * half,), F32)
    invf = jnp.concatenate([inv_freq, inv_freq, zeros])[None, :]
    sign = jnp.concatenate([-jnp.ones((half,), F32), jnp.ones((half,), F32), zeros])[None, :]
    tm = min(n, 2048)
    assert n % tm == 0
    vec = pl.BlockSpec((1, LANES), lambda i: (0, 0))
    out = pl.BlockSpec((tm, LANES), lambda i: (i, 0))
    return pl.pallas_call(
        _rope_table_kernel,
        out_shape=(jax.ShapeDtypeStruct((n, LANES), F32),) * 2,
        grid=(n // tm,),
        in_specs=[pl.BlockSpec((tm, 1), lambda i: (i, 0)), vec, vec],
        out_specs=(out, out),
        compiler_params=_cparams(("parallel",)),
        name="rope_tables",
    )(pos_flat[:, None], invf, sign)


def _norm_matmul_rope_kernel(x_ref, g_ref, w_ref, cos_ref, sin_ref, cs_ref, o_ref, u_ref, *, tn):
    @pl.when(pl.program_id(1) == 0)
    def _():
        x = x_ref[...]
        ms = jnp.mean(x * x, axis=-1, keepdims=True)
        u_ref[...] = (x * lax.rsqrt(ms + RMS_EPS) * g_ref[...]).astype(BF16)

    acc = jnp.dot(u_ref[...], w_ref[...], preferred_element_type=F32) * cs_ref[...]
    cosv = cos_ref[...]
    sinv = sin_ref[...]
    lane = lax.broadcasted_iota(jnp.int32, cosv.shape, 1)
    for h in range(tn // LANES):
        z = acc[:, h * LANES:(h + 1) * LANES]
        o_ref[:, h * LANES:(h + 1) * LANES] = (z * cosv + _rope_partner(z, lane) * sinv).astype(o_ref.dtype)


def _norm_matmul_rope(x, g, w, cosf, sinf, colscale, tm, tn):
    m, d = x.shape
    n = w.shape[1]
    tm = min(tm, m)
    assert m % tm == 0 and n % tn == 0
    rows = pl.BlockSpec((tm, LANES), lambda i, j: (i, 0))
    return pl.pallas_call(
        functools.partial(_norm_matmul_rope_kernel, tn=tn),
        out_shape=(jax.ShapeDtypeStruct((m, n), BF16), jax.ShapeDtypeStruct((m, d), BF16)),
        grid=(m // tm, n // tn),
        in_specs=[
            pl.BlockSpec((tm, d), lambda i, j: (i, 0)),
            pl.BlockSpec((1, d), lambda i, j: (0, 0)),
            pl.BlockSpec((d, tn), lambda i, j: (0, j)),
            rows, rows,
            pl.BlockSpec((1, tn), lambda i, j: (0, j)),
        ],
        out_specs=(pl.BlockSpec((tm, tn), lambda i, j: (i, j)), pl.BlockSpec((tm, d), lambda i, j: (i, 0))),
        compiler_params=_cparams(("parallel", "arbitrary")),
        name="norm_matmul_rope",
    )(x, g[None, :], w, cosf, sinf, colscale)


def _matmul_kernel(a_ref, w_ref, o_ref, *, relu2):
    acc = jnp.dot(a_ref[...], w_ref[...], preferred_element_type=F32)
    if relu2:
        acc = jnp.square(jnp.maximum(acc, 0.0))
    o_ref[...] = acc.astype(o_ref.dtype)


def _matmul(a, w, out_dtype, tm, tn, relu2=False):
    m, kdim = a.shape
    n = w.shape[1]
    tm = min(tm, m)
    assert m % tm == 0 and n % tn == 0
    return pl.pallas_call(
        functools.partial(_matmul_kernel, relu2=relu2),
        out_shape=jax.ShapeDtypeStruct((m, n), out_dtype),
        grid=(m // tm, n // tn),
        in_specs=[pl.BlockSpec((tm, kdim), lambda i, j: (i, 0)), pl.BlockSpec((kdim, tn), lambda i, j: (0, j))],
        out_specs=pl.BlockSpec((tm, tn), lambda i, j: (i, j)),
        compiler_params=_cparams(("parallel", "parallel")),
        name="matmul",
    )(a, w)


def _matmul_norm_res_kernel(a_ref, w_ref, g_ref, res_ref, *rest, with_next):
    if with_next:
        gnext_ref, o_ref, unext_ref, acc_ref = rest
    else:
        o_ref, acc_ref = rest
    k = pl.program_id(1)

    @pl.when(k == 0)
    def _():
        acc_ref[...] = jnp.zeros_like(acc_ref)

    acc_ref[...] += jnp.dot(a_ref[...], w_ref[...], preferred_element_type=F32)

    @pl.when(k == pl.num_programs(1) - 1)
    def _():
        y = acc_ref[...]
        ms = jnp.mean(y * y, axis=-1, keepdims=True)
        out = res_ref[...] + y * lax.rsqrt(ms + RMS_EPS) * g_ref[...]
        o_ref[...] = out
        if with_next:
            ms2 = jnp.mean(out * out, axis=-1, keepdims=True)
            unext_ref[...] = (out * lax.rsqrt(ms2 + RMS_EPS) * gnext_ref[...]).astype(BF16)


def _matmul_norm_res(a, w, g, res, next_gain=None, tm=512, tk=2048):
    m, kdim = a.shape
    n = w.shape[1]
    tm = min(tm, m)
    assert m % tm == 0 and kdim % tk == 0
    with_next = next_gain is not None
    vec = pl.BlockSpec((1, n), lambda i, k: (0, 0))
    rowblk = pl.BlockSpec((tm, n), lambda i, k: (i, 0))
    in_specs = [pl.BlockSpec((tm, tk), lambda i, k: (i, k)), pl.BlockSpec((tk, n), lambda i, k: (k, 0)), vec, rowblk]
    args = [a, w, g[None, :], res]
    out_shape = jax.ShapeDtypeStruct((m, n), F32)
    out_specs = rowblk
    if with_next:
        in_specs.append(vec)
        args.append(next_gain[None, :])
        out_shape = (out_shape, jax.ShapeDtypeStruct((m, n), BF16))
        out_specs = (rowblk, rowblk)
    return pl.pallas_call(
        functools.partial(_matmul_norm_res_kernel, with_next=with_next),
        out_shape=out_shape,
        grid=(m // tm, kdim // tk),
        in_specs=in_specs,
        out_specs=out_specs,
        scratch_shapes=[pltpu.VMEM((tm, n), F32)],
        compiler_params=_cparams(("parallel", "arbitrary")),
        name="matmul_norm_res",
    )(*args)


def _gelu_tanh(x):
    return 0.5 * x * (1.0 + jnp.tanh(math.sqrt(2.0 / math.pi) * (x + 0.044715 * (x * x * x))))


def _nsa_compress_kernel(x_ref, w1_ref, w2_ref, pe_ref, cos_ref, sin_ref, o_ref, *, use_rope):
    x = x_ref[0, 0]
    half = x.shape[1]
    n_chunk = x.shape[0]
    a = jnp.dot(x, w1_ref[0:half, :], preferred_element_type=F32)
    b = jnp.dot(x, w1_ref[half:2 * half, :], preferred_element_type=F32)
    pe = jnp.broadcast_to(pe_ref[...], (SUBLANES, pe_ref.shape[1]))
    peb = _dot_exact_rhs_general(pe, w1_ref[...])[0:1, :]
    h = a + pltpu.roll(b, n_chunk - 1, 0) + peb
    y = _bdot(_gelu_tanh(h), w2_ref[...])
    if use_rope:
        lane = lax.broadcasted_iota(jnp.int32, y.shape, 1)
        y = y * cos_ref[0] + _rope_partner(y, lane) * sin_ref[0]
    o_ref[0, 0] = y.astype(o_ref.dtype)


def _dot_exact_rhs_general(x, w_bf):
    hi, mid, lo = _split3(x)
    d = lambda p: jnp.dot(p, w_bf, preferred_element_type=F32)
    return d(hi) + (d(mid) + d(lo))


def _nsa_compress(x2, w1, w2, pe_flat, cosc, sinc, use_rope):
    b, g, n_chunk, width = x2.shape
    dh = w2.shape[1]
    return pl.pallas_call(
        functools.partial(_nsa_compress_kernel, use_rope=use_rope),
        out_shape=jax.ShapeDtypeStruct((b, g, n_chunk, dh), BF16),
        grid=(b, g),
        in_specs=[
            pl.BlockSpec((1, 1, n_chunk, width), lambda i, j: (i, j, 0, 0)),
            pl.BlockSpec(w1.shape, lambda i, j: (0, 0)),
            pl.BlockSpec(w2.shape, lambda i, j: (0, 0)),
            pl.BlockSpec(pe_flat.shape, lambda i, j: (0, 0)),
            pl.BlockSpec((1, n_chunk, dh), lambda i, j: (i, 0, 0)),
            pl.BlockSpec((1, n_chunk, dh), lambda i, j: (i, 0, 0)),
        ],
        out_specs=pl.BlockSpec((1, 1, n_chunk, dh), lambda i, j: (i, j, 0, 0)),
        compiler_params=_cparams(("parallel", "parallel")),
        name="nsa_compress",
    )(x2, w1, w2, pe_flat, cosc, sinc)


def _nsa_cmp_kernel(q_ref, kc_ref, vc_ref, ov_ref, o_ref, sel_ref, *, tq, top, n_sel):
    i = pl.program_id(2)
    kc = kc_ref[0, 0]
    vc = vc_ref[0, 0]
    ncp = kc.shape[0]
    t = i * tq + lax.broadcasted_iota(jnp.int32, (tq, ncp), 0)
    c = lax.broadcasted_iota(jnp.int32, (tq, ncp), 1)
    valid = (c * NSA_CMP_STRIDE + (NSA_CMP_LEN - 1)) <= t
    psum = jnp.zeros((tq, ncp), F32)
    for h in range(NSA_HPG):
        q = q_ref[0, :, h * HEAD_DIM:(h + 1) * HEAD_DIM]
        s = jnp.where(valid, _dot_nt(q, kc), NEG)
        m = jnp.max(s, axis=1, keepdims=True)
        e = jnp.where(valid, jnp.exp2(s - m), 0.0)
        l = jnp.sum(e, axis=1, keepdims=True)
        p = e / jnp.where(l > 0.0, l, 1.0)
        o_ref[0, :, h * HEAD_DIM:(h + 1) * HEAD_DIM] = _bdot(p, vc).astype(o_ref.dtype)
        psum = psum + p
    imp = jnp.transpose(_dot_exact_rhs(psum, ov_ref[...]))[0:n_sel]
    jj = lax.broadcasted_iota(jnp.int32, (n_sel, tq), 0)
    blk_t = jnp.right_shift(i * tq + lax.broadcasted_iota(jnp.int32, (n_sel, tq), 1), SEL_SHIFT)
    forced = (jj == 0) | (jj == blk_t) | (jj == blk_t - 1)
    imp = jnp.where(forced, 1e9, jnp.where(jj > blk_t, -1.0, imp))
    ng = n_sel // SUBLANES
    groups = [imp[g * SUBLANES:(g + 1) * SUBLANES] for g in range(ng)]
    ranks = [jnp.zeros((SUBLANES, tq), F32) for _ in range(ng)]
    sub = lax.broadcasted_iota(jnp.int32, (SUBLANES, tq), 0)
    for ii in range(n_sel):
        gi, ri = divmod(ii, SUBLANES)
        row = jnp.broadcast_to(imp[ii:ii + 1, :], (SUBLANES, tq))
        for g in range(ng):
            if g > gi:
                beats = row >= groups[g]
            elif g < gi:
                beats = row > groups[g]
            else:
                beats = (row > groups[g]) | ((row == groups[g]) & (sub > ri))
            ranks[g] = ranks[g] + jnp.where(beats, 1.0, 0.0)
    chosen = jnp.where(jnp.concatenate(ranks, axis=0) < float(top), 1.0, 0.0)
    pad = sel_ref.shape[3] - n_sel
    if pad:
        chosen = jnp.concatenate([chosen, jnp.zeros((pad, tq), F32)], axis=0)
    sel_ref[0, 0] = jnp.transpose(chosen).astype(sel_ref.dtype)


def _nsa_cmp(proj_rope, kcmp, vcmp, overlap, n_sel, tq):
    b, s, _ = proj_rope.shape
    g = kcmp.shape[1]
    ncp = kcmp.shape[2]
    n_pad = overlap.shape[1]
    top = min(NSA_SEL_TOPK, n_sel)
    qw = NSA_HPG * HEAD_DIM
    return pl.pallas_call(
        functools.partial(_nsa_cmp_kernel, tq=tq, top=top, n_sel=n_sel),
        out_shape=(jax.ShapeDtypeStruct((b, s, NSA_WIDTH), BF16),
                   jax.ShapeDtypeStruct((b, g, s, n_pad), BF16)),
        grid=(b, g, s // tq),
        in_specs=[
            pl.BlockSpec((1, tq, qw), lambda bi, gi, i: (bi, i, ROPE_Q // qw + gi)),
            pl.BlockSpec((1, 1, ncp, HEAD_DIM), lambda bi, gi, i: (bi, gi, 0, 0)),
            pl.BlockSpec((1, 1, ncp, HEAD_DIM), lambda bi, gi, i: (bi, gi, 0, 0)),
            pl.BlockSpec(overlap.shape, lambda bi, gi, i: (0, 0)),
        ],
        out_specs=(pl.BlockSpec((1, tq, qw), lambda bi, gi, i: (bi, i, gi)),
                   pl.BlockSpec((1, 1, tq, n_pad), lambda bi, gi, i: (bi, gi, i, 0))),
        compiler_params=_cparams(("parallel", "parallel", "parallel")),
        name="nsa_cmp_select",
    )(proj_rope, kcmp, vcmp, overlap)


def _nsa_flash_kernel(qi_ref, kb_ref, first_ref, *refs, mode, tq, tk):
    if mode == "sel":
        q_ref, k_ref, v_ref, sel_ref, hot_ref, o_ref, m_ref, l_ref, acc_ref = refs
    else:
        q_ref, k_ref, v_ref, o_ref, m_ref, l_ref, acc_ref = refs
    n = pl.program_id(2)
    i = qi_ref[n]
    kb = kb_ref[n]

    @pl.when(first_ref[n] == 1)
    def _():
        m_ref[...] = jnp.full_like(m_ref, NEG)
        l_ref[...] = jnp.zeros_like(l_ref)
        acc_ref[...] = jnp.zeros_like(acc_ref)

    ngrp = k_ref.shape[2] // HEAD_DIM

    def step(diag):
        valid = None
        if mode == "win":
            rows = i * tq + lax.broadcasted_iota(jnp.int32, (tq, tk), 0)
            cols = kb * tk + lax.broadcasted_iota(jnp.int32, (tq, tk), 1)
            d = rows - cols
            valid = (d >= 0) & (d < NSA_WINDOW)
        elif diag:
            valid = (lax.broadcasted_iota(jnp.int32, (tq, tk), 0) >= lax.broadcasted_iota(jnp.int32, (tq, tk), 1))
        for gi in range(ngrp):
            k = k_ref[0, :, gi * HEAD_DIM:(gi + 1) * HEAD_DIM]
            if mode == "sel":
                penalty = ((sel_ref[0, gi].astype(F32) - 1.0) * (-NEG)).astype(BF16)
                k = jnp.concatenate([k, hot_ref[...]], axis=1)
            v_ones = jnp.concatenate([v_ref[0, :, gi * HEAD_DIM:(gi + 1) * HEAD_DIM],
                                      jnp.ones((tk, LANES), BF16)], axis=1)
            for hd in range(NSA_HPG):
                slot = gi * NSA_HPG + hd
                q = q_ref[0, :, slot * HEAD_DIM:(slot + 1) * HEAD_DIM]
                if mode == "sel":
                    q = jnp.concatenate([q, penalty], axis=1)
                s = _dot_nt(q, k)
                if valid is not None:
                    s = jnp.where(valid, s, NEG)
                m_old = m_ref[slot]
                m_new = jnp.maximum(m_old, jnp.max(s, axis=1, keepdims=True))
                alpha = jnp.exp2(m_old - m_new)
                p = jnp.exp2(s - jnp.concatenate([m_new] * (tk // LANES), axis=1))
                pv = jnp.dot(p.astype(BF16), v_ones, preferred_element_type=F32)
                l_ref[slot] = alpha * l_ref[slot] + pv[:, HEAD_DIM:]
                acc_ref[slot] = alpha * acc_ref[slot] + pv[:, 0:HEAD_DIM]
                m_ref[slot] = m_new

    if mode == "sel":
        pl.when(kb < i)(functools.partial(step, False))
        pl.when(kb == i)(functools.partial(step, True))
    else:
        step(True)

    @pl.when(kb == i)
    def _():
        for slot in range(ngrp * NSA_HPG):
            o_ref[0, :, slot * HEAD_DIM:(slot + 1) * HEAD_DIM] = (acc_ref[slot] / l_ref[slot]).astype(o_ref.dtype)


def _nsa_flash(proj_rope, proj_plain, sel, mode, tq, tk):
    b, s, _ = proj_rope.shape
    g = NSA_GROUPS
    gpb = g
    qw = gpb * NSA_HPG * HEAD_DIM
    kw = gpb * HEAD_DIM
    nq = s // tq
    assert tq == tk
    if mode == "sel":
        back = nq
        koff, voff = ROPE_KS, PL_VS
    else:
        assert NSA_WINDOW % tk == 0
        back = NSA_WINDOW // tk
        koff, voff = ROPE_KW, PL_VW
    assert g % gpb == 0 and ROPE_Q % qw == 0 and koff % kw == 0 and voff % kw == 0
    qi, kb, first = _causal_pairs(nq, back)
    in_specs = [
        pl.BlockSpec((1, tq, qw), lambda bi, gi, n, qi, kb, fs: (bi, qi[n], ROPE_Q // qw + gi)),
        pl.BlockSpec((1, tk, kw), lambda bi, gi, n, qi, kb, fs: (bi, kb[n], koff // kw + gi)),
        pl.BlockSpec((1, tk, kw), lambda bi, gi, n, qi, kb, fs: (bi, kb[n], voff // kw + gi)),
    ]
    args = [proj_rope, proj_rope, proj_plain]
    if mode == "sel":
        n_pad = sel.shape[3]
        in_specs.append(pl.BlockSpec((1, gpb, tq, n_pad), lambda bi, gi, n, qi, kb, fs: (bi, gi, qi[n], 0)))
        in_specs.append(pl.BlockSpec((tk, n_pad), lambda bi, gi, n, qi, kb, fs: (kb[n], 0)))
        block_of_key = jnp.arange(s, dtype=jnp.int32)[:, None] // NSA_SEL_BLOCK
        onehot = (block_of_key == jnp.arange(n_pad, dtype=jnp.int32)[None, :]).astype(BF16)
        args += [sel, onehot]
    return pl.pallas_call(
        functools.partial(_nsa_flash_kernel, mode=mode, tq=tq, tk=tk),
        out_shape=jax.ShapeDtypeStruct((b, s, NSA_WIDTH), BF16),
        grid_spec=pltpu.PrefetchScalarGridSpec(
            num_scalar_prefetch=3,
            grid=(b, g // gpb, qi.shape[0]),
            in_specs=in_specs,
            out_specs=pl.BlockSpec((1, tq, qw), lambda bi, gi, n, qi, kb, fs: (bi, qi[n], gi)),
            scratch_shapes=[
                pltpu.VMEM((gpb * NSA_HPG, tq, LANES), F32),
                pltpu.VMEM((gpb * NSA_HPG, tq, LANES), F32),
                pltpu.VMEM((gpb * NSA_HPG, tq, HEAD_DIM), F32),
            ],
        ),
        compiler_params=_cparams(("parallel", "parallel", "arbitrary")),
        name="nsa_flash_" + mode,
    )(qi, kb, first, *args)


def _causal_pairs(nq, back):
    qi, kb, first = [], [], []
    for i in range(nq):
        lo = max(0, i - back)
        for j in range(lo, i + 1):
            qi.append(i)
            kb.append(j)
            first.append(1 if j == lo else 0)
    as_i32 = lambda z: jnp.asarray(np.asarray(z, np.int32))
    return as_i32(qi), as_i32(kb), as_i32(first)


def _diff_flash_kernel(qi_ref, kb_ref, first_ref, q_ref, k_ref, v_ref, lam_ref, sub_ref, o_ref, m_ref, l_ref,
                       acc_ref, *, tq, tk, lam_init):
    n = pl.program_id(2)
    i = qi_ref[n]
    kb = kb_ref[n]

    @pl.when(first_ref[n] == 1)
    def _():
        m_ref[...] = jnp.full_like(m_ref, NEG)
        l_ref[...] = jnp.zeros_like(l_ref)
        acc_ref[...] = jnp.zeros_like(acc_ref)

    hpb = o_ref.shape[2] // DIFF_VDIM

    def step(masked):
        if masked:
            valid = (lax.broadcasted_iota(jnp.int32, (tq, tk), 0) >= lax.broadcasted_iota(jnp.int32, (tq, tk), 1))
        for hd in range(hpb):
            v = v_ref[0, :, hd * DIFF_VDIM:(hd + 1) * DIFF_VDIM]
            for mp in range(2):
                slot = 2 * hd + mp
                q = q_ref[0, :, slot * HEAD_DIM:(slot + 1) * HEAD_DIM]
                k = k_ref[0, :, slot * HEAD_DIM:(slot + 1) * HEAD_DIM]
                s = _dot_nt(q, k)
                if masked:
                    s = jnp.where(valid, s, NEG)
                m_old = m_ref[slot]
                m_new = jnp.maximum(m_old, jnp.max(s, axis=1, keepdims=True))
                alpha = jnp.exp2(m_old - m_new)
                p = jnp.exp2(s - jnp.concatenate([m_new] * (tk // LANES), axis=1))
                l_ref[slot] = alpha * l_ref[slot] + jnp.sum(p, axis=1, keepdims=True)
                acc_ref[slot] = (jnp.concatenate([alpha] * (DIFF_VDIM // LANES), axis=1) * acc_ref[slot]
                                 + _bdot(p, v))
                m_ref[slot] = m_new

    pl.when(kb < i)(functools.partial(step, False))
    pl.when(kb == i)(functools.partial(step, True))

    @pl.when(kb == i)
    def _():
        lam = lam_ref[...]
        lam_full = (jnp.exp(jnp.sum(lam[0:1, :] * lam[1:2, :], axis=1, keepdims=True))
                    - jnp.exp(jnp.sum(lam[2:3, :] * lam[3:4, :], axis=1, keepdims=True)) + lam_init)
        wide = lambda x: jnp.concatenate([x] * (DIFF_VDIM // LANES), axis=1)
        for hd in range(hpb):
            o = (acc_ref[2 * hd] / wide(l_ref[2 * hd])
                 - lam_full * (acc_ref[2 * hd + 1] / wide(l_ref[2 * hd + 1])))
            ms = jnp.mean(o * o, axis=-1, keepdims=True)
            o = o * lax.rsqrt(ms + 1e-5) * sub_ref[...]
            o_ref[0, :, hd * DIFF_VDIM:(hd + 1) * DIFF_VDIM] = (o * (1.0 - lam_init)).astype(o_ref.dtype)


def _diff_flash(proj_rope, proj_plain, lam, subln, layer, tq, tk):
    b, s, _ = proj_rope.shape
    assert tq == tk
    nq = s // tq
    lam_init = 0.8 - 0.6 * math.exp(-0.3 * layer)
    hpb = DIFF_HEADS
    w = hpb * DIFF_VDIM
    assert DIFF_HEADS % hpb == 0 and ROPE_DQ % w == 0 and ROPE_DK % w == 0 and PL_DV % w == 0
    qi, kb, first = _causal_pairs(nq, nq)
    return pl.pallas_call(
        functools.partial(_diff_flash_kernel, tq=tq, tk=tk, lam_init=lam_init),
        out_shape=jax.ShapeDtypeStruct((b, s, DIFF_WIDTH), BF16),
        grid_spec=pltpu.PrefetchScalarGridSpec(
            num_scalar_prefetch=3,
            grid=(b, DIFF_HEADS // hpb, qi.shape[0]),
            in_specs=[
                pl.BlockSpec((1, tq, w), lambda bi, h, n, qi, kb, fs: (bi, qi[n], ROPE_DQ // w + h)),
                pl.BlockSpec((1, tk, w), lambda bi, h, n, qi, kb, fs: (bi, kb[n], ROPE_DK // w + h)),
                pl.BlockSpec((1, tk, w), lambda bi, h, n, qi, kb, fs: (bi, kb[n], PL_DV // w + h)),
                pl.BlockSpec(lam.shape, lambda bi, h, n, qi, kb, fs: (0, 0)),
                pl.BlockSpec((1, DIFF_VDIM), lambda bi, h, n, qi, kb, fs: (0, 0)),
            ],
            out_specs=pl.BlockSpec((1, tq, w), lambda bi, h, n, qi, kb, fs: (bi, qi[n], h)),
            scratch_shapes=[
                pltpu.VMEM((2 * hpb, tq, LANES), F32),
                pltpu.VMEM((2 * hpb, tq, LANES), F32),
                pltpu.VMEM((2 * hpb, tq, DIFF_VDIM), F32),
            ],
        ),
        compiler_params=_cparams(("parallel", "parallel", "arbitrary")),
        name="diff_flash",
    )(qi, kb, first, proj_rope, proj_rope, proj_plain, lam, subln[None, :])


def _bmm(a, b):
    return jnp.einsum("umk,ukn->umn", a.astype(BF16), b.astype(BF16), preferred_element_type=F32)


def _bmm_nt(a, b):
    return jnp.einsum("umk,unk->umn", a.astype(BF16), b.astype(BF16), preferred_element_type=F32)


def _rwkv_scan_kernel(r_ref, k_ref, v_ref, gd_ref, wa_ref, mur_ref, muk_ref, muv_ref, mugd_ref, muwa_ref,
                      wdec_ref, wrate_ref, gup_ref, w0_ref, a0_ref, kk_ref, ka_ref, rk_ref, lnw_ref, lnb_ref,
                      o_ref, st_ref, last_ref, lastgd_ref, *, ts, chunk):
    @pl.when(pl.program_id(2) == 0)
    def _():
        st_ref[...] = jnp.zeros_like(st_ref)
        last_ref[...] = jnp.zeros_like(last_ref)
        lastgd_ref[...] = jnp.zeros_like(lastgd_ref)

    def shifted(x_ref, mu_ref, carry):
        x = x_ref[0]
        row = lax.broadcasted_iota(jnp.int32, x.shape, 0)
        prev = jnp.where(row == 0, carry, pltpu.roll(x, 1, 0))
        return x + (prev - x) * mu_ref[...]

    r = shifted(r_ref, mur_ref, last_ref[0, 0:1, :])
    k = shifted(k_ref, muk_ref, last_ref[1, 0:1, :])
    v = shifted(v_ref, muv_ref, last_ref[2, 0:1, :])
    wa = shifted(wa_ref, muwa_ref, last_ref[3, 0:1, :])
    gd = shifted(gd_ref, mugd_ref, lastgd_ref[0:1, :])
    for n, ref in enumerate((r_ref, k_ref, v_ref, wa_ref)):
        last_ref[n, 0:1, :] = ref[0, ts - 1:ts, :]
    lastgd_ref[0:1, :] = gd_ref[0, ts - 1:ts, :]

    wa = jnp.where(lax.broadcasted_iota(jnp.int32, wa.shape, 1) < RWKV_W_LORA, jnp.tanh(wa), wa)
    lw = -math.exp(-0.5) * jax.nn.sigmoid(w0_ref[...] + _bdot(wa, wdec_ref[...]))
    a = jax.nn.sigmoid(a0_ref[...] + _bdot(wa, wrate_ref[...]))
    g = _bdot(jax.nn.sigmoid(gd), gup_ref[...])

    c = chunk
    nc = ts // c
    head0 =lax.broadcasted_iota(jnp.int32, (ts, LANES), 1) < RWKV_HEAD

    def seg_sum(x):
        s0 = jnp.sum(jnp.where(head0, x, 0.0), axis=1, keepdims=True)
        s1 = jnp.sum(jnp.where(head0, 0.0, x), axis=1, keepdims=True)
        return jnp.where(head0, s0, s1)

    kk = k * kk_ref[...]
    kap = kk / jnp.maximum(jnp.sqrt(seg_sum(kk * kk)), 1e-12)
    kmod = k * (1.0 + (a - 1.0) * ka_ref[...])
    bvec = kap * a

    row_in_chunk = lax.broadcasted_iota(jnp.int32, (ts, LANES), 0) & (c - 1)
    cum = lw
    step = 1
    while step < c:
        cum = cum + jnp.where(row_in_chunk >= step, pltpu.roll(cum, step, 0), 0.0)
        step *= 2
    g_incl = jnp.exp(cum)
    g_inv = jnp.exp(-cum)
    split = lambda x: x.reshape(nc, c, LANES)
    g_last = [g_incl[(ch + 1) * c - 1:(ch + 1) * c, :] for ch in range(nc)]
    g_last_rows = jnp.concatenate([jnp.broadcast_to(gl, (c, LANES)) for gl in g_last], axis=0)
    kh = split(kmod * g_inv)
    bh = split(bvec * g_inv)
    kg = split(kmod * g_inv * g_last_rows)
    bg = split(bvec * g_inv * g_last_rows)
    kaph = kap * jnp.exp(cum - lw)
    rh = r * g_incl
    v3 = split(v)

    assert 2 * c == LANES
    h0 = lax.broadcasted_iota(jnp.int32, (nc, c, LANES), 2) < RWKV_HEAD
    stack = lambda y: jnp.concatenate([jnp.where(h0, y, 0.0), jnp.where(h0, 0.0, y)], axis=1)
    kap3 = split(kaph)
    r3 = split(rh)
    gram = _bmm_nt(jnp.concatenate([kap3, r3], axis=1),
                   jnp.concatenate([stack(bh), stack(kh)], axis=1))
    row = lax.broadcasted_iota(jnp.int32, (c, 4 * c), 0)
    col = lax.broadcasted_iota(jnp.int32, (c, 4 * c), 1) & (c - 1)
    top = jnp.where(row > col, gram[:, 0:c, :], 0.0)
    abk = jnp.where(row >= col, gram[:, c:2 * c, :], 0.0)
    lb = top[:, :, 0:LANES]
    lk = top[:, :, LANES:2 * LANES]
    tm = -lb
    q = _bmm(lb, stack(lb))
    n = 2
    while True:
        tm = tm + q + _bmm(tm, stack(q))
        n *= 2
        if n >= c:
            break
        q = _bmm(q, stack(q))
    kap_t = kap3 + _bmm(tm, stack(kap3))
    lkv = _bmm(lk, stack(v3))
    z0 = -(lkv + _bmm(tm, stack(lkv)))
    r_t = r3 - _bmm(abk[:, :, 0:LANES], stack(kap_t))
    y0 = _bmm(abk, jnp.concatenate([stack(z0), stack(v3)], axis=1))

    br = lax.broadcasted_iota(jnp.int32, (LANES, LANES), 0)
    bc = lax.broadcasted_iota(jnp.int32, (LANES, LANES), 1)
    blockdiag = (br < RWKV_HEAD) == (bc < RWKV_HEAD)
    state = st_ref[...]
    ys = []
    for ch in range(nc):
        trans = jnp.where(blockdiag, -_dot_tn(bg[ch], kap_t[ch]), 0.0)
        trans = trans + jnp.where(br == bc, jnp.broadcast_to(g_last[ch], (LANES, LANES)), 0.0)
        inject = jnp.where(blockdiag, _dot_tn(jnp.concatenate([bg[ch], kg[ch]], axis=0),
                                              jnp.concatenate([z0[ch], v3[ch]], axis=0)), 0.0)
        ys.append(_bdot(r_t[ch], state) + y0[ch])
        state = _dot3(trans, state) + inject
    st_ref[...] = state
    y = jnp.concatenate(ys, axis=0)

    mean = seg_sum(y) * (1.0 / RWKV_HEAD)
    yc = y - mean
    var = seg_sum(yc * yc) * (1.0 / RWKV_HEAD)
    yn = yc * lax.rsqrt(var + RWKV_LNX_EPS) * lnw_ref[...] + lnb_ref[...]
    bonus = seg_sum(r * kmod * rk_ref[...]) * v
    o_ref[0] = ((yn + bonus) * g).astype(o_ref.dtype)


def _rwkv_scan(proj3, mu_p, wwa, gup, w0, a0, k_k, k_a, r_k, lnx_w, lnx_b, ts=1024, chunk=64):
    b, s, _ = proj3.shape
    w = RWKV_WIDTH
    ts = min(ts, s)
    assert s % ts == 0 and ts % chunk == 0
    npair = w // LANES
    col = lambda off: pl.BlockSpec((1, ts, LANES), lambda bi, j, t: (bi, t, off // LANES + j))
    mucol = lambda off: pl.BlockSpec((1, LANES), lambda bi, j, t: (0, off // LANES + j))
    vec = pl.BlockSpec((1, LANES), lambda bi, j, t: (0, j))
    in_specs = [
        col(0), col(w), col(2 * w),
        pl.BlockSpec((1, ts, RW_G_PAD), lambda bi, j, t: (bi, t, RW_GD // RW_G_PAD)),
        pl.BlockSpec((1, ts, RW_LORA_PAD), lambda bi, j, t: (bi, t, RW_WA // RW_LORA_PAD)),
        mucol(0), mucol(w), mucol(2 * w),
        pl.BlockSpec((1, RW_G_PAD), lambda bi, j, t: (0, RW_GD // RW_G_PAD)),
        pl.BlockSpec((1, RW_LORA_PAD), lambda bi, j, t: (0, RW_WA // RW_LORA_PAD)),
        pl.BlockSpec((RW_LORA_PAD, LANES), lambda bi, j, t: (0, j)),
        pl.BlockSpec((RW_LORA_PAD, LANES), lambda bi, j, t: (0, npair + j)),
        pl.BlockSpec((RW_G_PAD, LANES), lambda bi, j, t: (0, j)),
    ] + [vec] * 7
    row = lambda z: z[None, :]
    return pl.pallas_call(
        functools.partial(_rwkv_scan_kernel, ts=ts, chunk=chunk),
        out_shape=jax.ShapeDtypeStruct((b, s, w), BF16),
        grid=(b, npair, s // ts),
        in_specs=in_specs,
        out_specs=pl.BlockSpec((1, ts, LANES), lambda bi, j, t: (bi, t, j)),
        scratch_shapes=[
            pltpu.VMEM((LANES, LANES), F32),
            pltpu.VMEM((4, SUBLANES, LANES), F32),
            pltpu.VMEM((SUBLANES, RW_G_PAD), F32),
        ],
        compiler_params=_cparams(("parallel", "parallel", "arbitrary")),
        name="rwkv_scan",
    )(proj3, proj3, proj3, proj3, proj3, mu_p, mu_p, mu_p, mu_p, mu_p, wwa, wwa, gup,
      row(w0), row(a0), row(k_k), row(k_a), row(r_k), row(lnx_w), row(lnx_b))


def _merge_kernel(oc_ref, os_ref, ow_ref, ng_ref, yd_ref, yr_ref, wb_ref, g0_ref, g1_ref, g2_ref, bias_ref,
                  o_ref, yn_ref):
    @pl.when(pl.program_id(1) == 0)
    def _():
        gates = jax.nn.sigmoid(ng_ref[...])
        for hd in range(NSA_HEADS):
            sl = slice(hd * HEAD_DIM, (hd + 1) * HEAD_DIM)
            o = (gates[:, 3 * hd:3 * hd + 1] * oc_ref[:, sl]
                 + gates[:, 3 * hd + 1:3 * hd + 2] * os_ref[:, sl]
                 + gates[:, 3 * hd + 2:3 * hd + 3] * ow_ref[:, sl])
            yn_ref[:, sl] = o.astype(BF16)

    branches = (yn_ref, yd_ref, yr_ref)
    graw = (g0_ref, g1_ref, g2_ref)
    acc = None
    for bi in range(N_BRANCH):
        gate = jax.nn.sigmoid(graw[bi][...] + bias_ref[bi])
        term = gate * jnp.dot(branches[bi][...], wb_ref[bi], preferred_element_type=F32)
        acc = term if acc is None else acc + term
    o_ref[...] = acc.astype(o_ref.dtype)


def _merge(o_cmp, o_slc, o_win, proj_f32, proj_plain, y_diff, y_rwkv, w_branch, b_gate, tm=1024, tn=512):
    m = o_cmp.shape[0]
    n = w_branch.shape[2]
    tm = min(tm, m)
    assert m % tm == 0 and n % tn == 0 and PL_BG % tn == 0
    bw = BRANCH_WIDTH
    rowblk = lambda: pl.BlockSpec((tm, bw), lambda i, j: (i, 0))
    gate_spec = lambda bi: pl.BlockSpec((tm, tn), lambda i, j: (i, PL_BG // tn + bi * (n // tn) + j))
    return pl.pallas_call(
        _merge_kernel,
        out_shape=jax.ShapeDtypeStruct((m, n), BF16),
        grid=(m // tm, n // tn),
        in_specs=[
            rowblk(), rowblk(), rowblk(),
            pl.BlockSpec((tm, LANES), lambda i, j: (i, F_NG // LANES)),
            rowblk(), rowblk(),
            pl.BlockSpec((N_BRANCH, bw, tn), lambda i, j: (0, 0, j)),
            gate_spec(0), gate_spec(1), gate_spec(2),
            pl.BlockSpec((N_BRANCH, 1, tn), lambda i, j: (0, 0, j)),
        ],
        out_specs=pl.BlockSpec((tm, tn), lambda i, j: (i, j)),
        scratch_shapes=[pltpu.VMEM((tm, bw), BF16)],
        compiler_params=_cparams(("parallel", "arbitrary")),
        name="gated_merge",
    )(o_cmp, o_slc, o_win, proj_f32, y_diff, y_rwkv, w_branch, proj_plain, proj_plain, proj_plain,
      b_gate[:, None, :])


def _pack_layer_weights(w_in, w_gate, rwkv_mu, w_up_lora, a_up_lora, g_up_lora):
    d = w_in.shape[0]
    nsa_sizes = (NSA_WIDTH,) + (NSA_KV,) * 6 + (3 * NSA_HEADS,)
    diff_sizes = (DIFF_QK, DIFF_QK, DIFF_WIDTH)
    rw_sizes = (RWKV_WIDTH,) * 3 + (RWKV_W_LORA, RWKV_A_LORA, RWKV_G_LORA)
    offs = np.cumsum((0,) + nsa_sizes + diff_sizes + rw_sizes)
    seg = [w_in[:, offs[i]:offs[i + 1]] for i in range(len(offs) - 1)]
    q, kc, vc, ks, vs, kw, vw, ng, dq, dk, dv, rr, rk, rv, wd, ad, gd = seg
    w_rope = jnp.concatenate([q, dq, dk, ks, kw], axis=1).astype(BF16)
    w_plain = jnp.concatenate([kc, vc, vs, vw, dv] + [w_gate[bi] for bi in range(N_BRANCH)], axis=1).astype(BF16)
    zpad = lambda n: jnp.zeros((d, n), w_in.dtype)
    w_f32 = jnp.concatenate(
        [rr, rk, rv, gd, zpad(RW_G_PAD - RWKV_G_LORA), wd, ad, ng, zpad(LANES - 3 * NSA_HEADS)],
        axis=1).astype(BF16)
    mu_rkv, mu_wa, mu_gd = jnp.split(rwkv_mu, [3 * RWKV_WIDTH, 3 * RWKV_WIDTH + RW_LORA_PAD])
    mu_p = jnp.concatenate([mu_rkv, mu_gd, jnp.zeros((RW_G_PAD - RWKV_G_LORA,), rwkv_mu.dtype), mu_wa])[None, :]
    zz = jnp.zeros((RWKV_W_LORA, RWKV_WIDTH), w_up_lora.dtype)
    wwa = jnp.concatenate([jnp.concatenate([w_up_lora, zz], axis=1),
                           jnp.concatenate([zz, a_up_lora], axis=1)], axis=0).astype(BF16)
    gup = jnp.concatenate([g_up_lora, jnp.zeros((RW_G_PAD - RWKV_G_LORA, RWKV_WIDTH), g_up_lora.dtype)],
                          axis=0).astype(BF16)
    return w_rope, w_plain, w_f32, mu_p, wwa, gup


def _overlap_matrix(n_cp, n_sel):
    c_start = np.arange(n_cp) * NSA_CMP_STRIDE
    c_end = c_start + NSA_CMP_LEN - 1
    j_start = np.arange(n_sel) * NSA_SEL_BLOCK
    ov = (c_start[:, None] <= j_start[None, :] + NSA_SEL_BLOCK - 1) & (c_end[:, None] >= j_start[None, :])
    ov = np.pad(ov.astype(np.float32), ((0, 0), (0, -n_sel % LANES)))
    return jnp.asarray(ov).astype(BF16)


def _chunk_rows(z, groups):
    b, s, _ = z.shape
    z = z.reshape(b, s // NSA_CMP_STRIDE, NSA_CMP_STRIDE, groups, HEAD_DIM)
    return z.transpose(0, 3, 1, 2, 4).reshape(b, groups, s // NSA_CMP_STRIDE, NSA_CMP_STRIDE * HEAD_DIM)


def _nsa_branches(proj_rope, proj_plain, positions, cmp_pos, cmp_w1, cmp_w2, tq):
    b, s, _ = proj_rope.shape
    n_chunk = s // NSA_CMP_STRIDE
    n_sel = s // NSA_SEL_BLOCK
    kc2 = _chunk_rows(proj_plain[:, :, PL_KC:PL_KC + NSA_KV], NSA_GROUPS)
    vc2 = _chunk_rows(proj_plain[:, :, PL_VC:PL_VC + NSA_KV], NSA_GROUPS)
    cmp_end = np.minimum(np.arange(n_chunk) * NSA_CMP_STRIDE + NSA_CMP_LEN - 1, s - 1)
    pos_c = jnp.take(positions, jnp.asarray(cmp_end), axis=1)
    cosc, sinc = _rope_tables(pos_c.reshape(-1))
    cosc = cosc.reshape(b, n_chunk, LANES)
    sinc = sinc.reshape(b, n_chunk, LANES)
    w1 = cmp_w1.astype(BF16)
    w2 = cmp_w2.astype(BF16)
    pe = cmp_pos.reshape(2, 1, NSA_CMP_LEN * HEAD_DIM)
    kcmp = _nsa_compress(kc2, w1[0], w2[0], pe[0], cosc, sinc, True)
    vcmp = _nsa_compress(vc2, w1[1], w2[1], pe[1], cosc, sinc, False)
    o_cmp, sel = _nsa_cmp(proj_rope, kcmp, vcmp, _overlap_matrix(n_chunk, n_sel), n_sel, tq)
    o_slc = _nsa_flash(proj_rope, proj_plain, sel, "sel", tq, tq)
    o_win = _nsa_flash(proj_rope, proj_plain, None, "win", tq, tq)
    return o_cmp, o_slc, o_win


def _layer(x2, b, s, layer, cosf, sinf, positions, p):
    m = b * s
    w_rope, w_plain, w_f32, mu_p, wwa, gup = _pack_layer_weights(
        p["w_in"], p["w_gate"], p["rwkv_mu"], p["rwkv_w_up"], p["rwkv_a_up"], p["rwkv_g_up"])
    scale = HEAD_DIM ** -0.5 * math.log2(math.e)
    colscale = jnp.concatenate([jnp.full((ROPE_SCALED,), scale, F32),
                                jnp.ones((ROPE_COLS - ROPE_SCALED,), F32)])[None, :]
    g_pre = p["norm_pre_mix"]
    proj_rope, u = _norm_matmul_rope(x2, g_pre, w_rope, cosf, sinf, colscale, tm=512, tn=ROPE_COLS // 2)
    proj_plain = _matmul(u, w_plain, BF16, tm=2048, tn=1024)
    proj_f32 = _matmul(u, w_f32, F32, tm=1024, tn=F32_COLS // 2)
    pr3 = proj_rope.reshape(b, s, ROPE_COLS)
    pp3 = proj_plain.reshape(b, s, PLAIN_COLS)

    tq = min(512, s)
    o_cmp, o_slc, o_win = _nsa_branches(pr3, pp3, positions, p["nsa_cmp_pos"], p["nsa_cmp_w1"], p["nsa_cmp_w2"], tq)
    y_diff = _diff_flash(pr3, pp3, p["diff_lambda"], p["diff_subln"], layer, tq, tq)

    y_rwkv = _rwkv_scan(proj_f32.reshape(b, s, F32_COLS), mu_p, wwa, gup, p["rwkv_w0"], p["rwkv_a0"],
                        p["rwkv_k_k"], p["rwkv_k_a"], p["rwkv_r_k"].reshape(-1), p["rwkv_lnx_w"], p["rwkv_lnx_b"])

    merged = _merge(o_cmp.reshape(m, NSA_WIDTH), o_slc.reshape(m, NSA_WIDTH), o_win.reshape(m, NSA_WIDTH),
                    proj_f32, proj_plain, y_diff.reshape(m, DIFF_WIDTH), y_rwkv.reshape(m, RWKV_WIDTH),
                    p["w_branch"].astype(BF16), p["b_gate"])
    x2, u_mlp = _matmul_norm_res(merged, p["w_out"].astype(BF16), p["norm_post_mix"], x2,
                                 next_gain=p["norm_pre_mlp"])
    hidden = _matmul(u_mlp, p["w_up"].astype(BF16), BF16, tm=2048, tn=1024, relu2=True)
    x2 = _matmul_norm_res(hidden, p["w_down"].astype(BF16), p["norm_post_mlp"], x2)
    return x2


def kernel(x, positions, norm_pre_mix, norm_post_mix, norm_pre_mlp, norm_post_mlp, w_in, nsa_cmp_pos, nsa_cmp_w1, nsa_cmp_w2, diff_lambda, diff_subln, rwkv_mu, rwkv_w0, rwkv_w_up, rwkv_a0, rwkv_a_up, rwkv_g_up, rwkv_k_k, rwkv_k_a, rwkv_r_k, rwkv_lnx_w, rwkv_lnx_b, w_gate, b_gate, w_branch, w_out, w_up, w_down):
    b, s, d = x.shape
    depth = w_in.shape[0]
    stacked = dict(
        norm_pre_mix=norm_pre_mix, norm_post_mix=norm_post_mix, norm_pre_mlp=norm_pre_mlp,
        norm_post_mlp=norm_post_mlp, w_in=w_in, nsa_cmp_pos=nsa_cmp_pos, nsa_cmp_w1=nsa_cmp_w1,
        nsa_cmp_w2=nsa_cmp_w2, diff_lambda=diff_lambda, diff_subln=diff_subln, rwkv_mu=rwkv_mu,
        rwkv_w0=rwkv_w0, rwkv_w_up=rwkv_w_up, rwkv_a0=rwkv_a0, rwkv_a_up=rwkv_a_up, rwkv_g_up=rwkv_g_up,
        rwkv_k_k=rwkv_k_k, rwkv_k_a=rwkv_k_a, rwkv_r_k=rwkv_r_k, rwkv_lnx_w=rwkv_lnx_w,
        rwkv_lnx_b=rwkv_lnx_b, w_gate=w_gate, b_gate=b_gate, w_branch=w_branch, w_out=w_out,
        w_up=w_up, w_down=w_down)
    cosf, sinf = _rope_tables(positions.reshape(-1))
    x2 = x.reshape(b * s, d)
    for layer in range(depth):
        x2 = _layer(x2, b, s, layer, cosf, sinf, positions, {n: a[layer] for n, a in stacked.items()})
    return x2.reshape(b, s, d)
```

```python
import functools
import math

import jax
import jax.numpy as jnp
import numpy as np
from jax import lax
from jax.experimental import pallas as pl
from jax.experimental.pallas import tpu as pltpu

F32 = jnp.float32
BF16 = jnp.bfloat16

D_MODEL = 2048
RMS_EPS = 1e-6
ROPE_THETA = 500000.0
HEAD_DIM = 128
ROT_HALF = HEAD_DIM // 8
NSA_HEADS = 8
NSA_GROUPS = 2
NSA_HPG = NSA_HEADS // NSA_GROUPS
NSA_CMP_LEN = 32
NSA_CMP_STRIDE = 16
NSA_CMP_HIDDEN = 256
NSA_SEL_BLOCK = 64
SEL_SHIFT = NSA_SEL_BLOCK.bit_length() - 1
NSA_SEL_TOPK = 16
NSA_WINDOW = 512
NSA_WIDTH = NSA_HEADS * HEAD_DIM
NSA_KV = NSA_GROUPS * HEAD_DIM
DIFF_HEADS = 4
DIFF_VDIM = 2 * HEAD_DIM
DIFF_WIDTH = DIFF_HEADS * DIFF_VDIM
DIFF_QK = 2 * DIFF_HEADS * HEAD_DIM
RWKV_HEAD = 64
RWKV_WIDTH = 1024
RWKV_W_LORA = 64
RWKV_A_LORA = 64
RWKV_G_LORA = 160
RWKV_LNX_EPS = 64e-5
N_BRANCH = 3
BRANCH_WIDTH = 1024

LANES = 128
SUBLANES = 8
VMEM_LIMIT_BYTES = 56 * 1024 * 1024

NEG = -1e30

ROPE_Q, ROPE_DQ, ROPE_DK, ROPE_KS, ROPE_KW = 0, 1024, 2048, 3072, 3328
ROPE_SCALED = ROPE_DK
ROPE_COLS = 3584
PL_KC, PL_VC, PL_VS, PL_VW, PL_DV, PL_BG = 0, 256, 512, 768, 1024, 2048
PLAIN_COLS = PL_BG + N_BRANCH * D_MODEL
RW_LORA_PAD = 128
RW_G_PAD = 256
RW_GD = 3 * RWKV_WIDTH
RW_WA = RW_GD + RW_G_PAD
RW_COLS = RW_WA + RW_LORA_PAD
F_NG = RW_COLS
F32_COLS = RW_COLS + LANES


def _cparams(sem):
    return pltpu.CompilerParams(dimension_semantics=sem, vmem_limit_bytes=VMEM_LIMIT_BYTES)


def _bdot(a, b):
    return jnp.dot(a.astype(BF16), b.astype(BF16), preferred_element_type=F32)


def _dot_nt(a, b):
    return lax.dot_general(a.astype(BF16), b.astype(BF16), (((1,), (1,)), ((), ())),
                           preferred_element_type=F32)


def _dot_tn(a, b):
    return lax.dot_general(a.astype(BF16), b.astype(BF16), (((0,), (0,)), ((), ())),
                           preferred_element_type=F32)


def _split2(x):
    hi = x.astype(BF16)
    lo = (x - hi.astype(F32)).astype(BF16)
    return hi, lo


def _split3(x):
    hi = x.astype(BF16)
    r1 = x - hi.astype(F32)
    mid = r1.astype(BF16)
    lo = (r1 - mid.astype(F32)).astype(BF16)
    return hi, mid, lo


def _dot_exact_lhs(a_bf, x):
    hi, mid, lo = _split3(x)
    d = lambda p: jnp.dot(a_bf, p, preferred_element_type=F32)
    return d(hi) + (d(mid) + d(lo))


def _dot_exact_rhs(x, b_bf):
    hi, mid, lo = _split3(x)
    d = lambda p: jnp.dot(p, b_bf, preferred_element_type=F32)
    return d(hi) + (d(mid) + d(lo))


def _dot3(a, b):
    ah, al = _split2(a)
    bh, bl = _split2(b)
    d = lambda p, q: jnp.dot(p, q, preferred_element_type=F32)
    return d(ah, bh) + (d(ah, bl) + d(al, bh))


def _rope_partner(z, lane):
    return jnp.where(lane < ROT_HALF, pltpu.roll(z, LANES - ROT_HALF, 1), pltpu.roll(z, ROT_HALF, 1))


def _rope_table_kernel(pos_ref, invf_ref, sign_ref, cos_ref, sin_ref):
    ang = pos_ref[...].astype(F32) * invf_ref[...]
    cos_ref[...] = jnp.cos(ang)
    sin_ref[...] = jnp.sin(ang) * sign_ref[...]


def _rope_tables(pos_flat):
    n = pos_flat.shape[0]
    half = ROT_HALF
    inv_freq = ROPE_THETA ** (-jnp.arange(half, dtype=F32) / half)
    zeros = jnp.zeros((LANES - 2 * half,), F32)
    invf = jnp.concatenate([inv_freq, inv_freq, zeros])[None, :]
    sign = jnp.concatenate([-jnp.ones((half,), F32), jnp.ones((half,), F32), zeros])[None, :]
    tm = min(n, 2048)
    assert n % tm == 0
    vec = pl.BlockSpec((1, LANES), lambda i: (0, 0))
    out = pl.BlockSpec((tm, LANES), lambda i: (i, 0))
    return pl.pallas_call(
        _rope_table_kernel,
        out_shape=(jax.ShapeDtypeStruct((n, LANES), F32),) * 2,
        grid=(n // tm,),
        in_specs=[pl.BlockSpec((tm, 1), lambda i: (i, 0)), vec, vec],
        out_specs=(out, out),
        compiler_params=_cparams(("parallel",)),
        name="rope_tables",
    )(pos_flat[:, None], invf, sign)


def _norm_matmul_rope_kernel(x_ref, g_ref, w_ref, cos_ref, sin_ref, cs_ref, o_ref, u_ref, *, tn):
    @pl.when(pl.program_id(1) == 0)
    def _():
        x = x_ref[...]
        ms = jnp.mean(x * x, axis=-1, keepdims=True)
        u_ref[...] = (x * lax.rsqrt(ms + RMS_EPS) * g_ref[...]).astype(BF16)

    acc = jnp.dot(u_ref[...], w_ref[...], preferred_element_type=F32) * cs_ref[...]
    cosv = cos_ref[...]
    sinv = sin_ref[...]
    lane = lax.broadcasted_iota(jnp.int32, cosv.shape, 1)
    for h in range(tn // LANES):
        z = acc[:, h * LANES:(h + 1) * LANES]
        o_ref[:, h * LANES:(h + 1) * LANES] = (z * cosv + _rope_partner(z, lane) * sinv).astype(o_ref.dtype)


def _norm_matmul_rope(x, g, w, cosf, sinf, colscale, tm, tn):
    m, d = x.shape
    n = w.shape[1]
    tm = min(tm, m)
    assert m % tm == 0 and n % tn == 0
    rows = pl.BlockSpec((tm, LANES), lambda i, j: (i, 0))
    return pl.pallas_call(
        functools.partial(_norm_matmul_rope_kernel, tn=tn),
        out_shape=(jax.ShapeDtypeStruct((m, n), BF16), jax.ShapeDtypeStruct((m, d), BF16)),
        grid=(m // tm, n // tn),
        in_specs=[
            pl.BlockSpec((tm, d), lambda i, j: (i, 0)),
            pl.BlockSpec((1, d), lambda i, j: (0, 0)),
            pl.BlockSpec((d, tn), lambda i, j: (0, j)),
            rows, rows,
            pl.BlockSpec((1, tn), lambda i, j: (0, j)),
        ],
        out_specs=(pl.BlockSpec((tm, tn), lambda i, j: (i, j)), pl.BlockSpec((tm, d), lambda i, j: (i, 0))),
        compiler_params=_cparams(("parallel", "arbitrary")),
        name="norm_matmul_rope",
    )(x, g[None, :], w, cosf, sinf, colscale)


def _matmul_kernel(a_ref, w_ref, o_ref, *, relu2):
    acc = jnp.dot(a_ref[...], w_ref[...].astype(BF16), preferred_element_type=F32)
    if relu2:
        acc = jnp.square(jnp.maximum(acc, 0.0))
    o_ref[...] = acc.astype(o_ref.dtype)


def _matmul(a, w, out_dtype, tm, tn, relu2=False):
    m, kdim = a.shape
    n = w.shape[1]
    tm = min(tm, m)
    assert m % tm == 0 and n % tn == 0
    return pl.pallas_call(
        functools.partial(_matmul_kernel, relu2=relu2),
        out_shape=jax.ShapeDtypeStruct((m, n), out_dtype),
        grid=(m // tm, n // tn),
        in_specs=[pl.BlockSpec((tm, kdim), lambda i, j: (i, 0)), pl.BlockSpec((kdim, tn), lambda i, j: (0, j))],
        out_specs=pl.BlockSpec((tm, tn), lambda i, j: (i, j)),
        compiler_params=_cparams(("parallel", "parallel")),
        name="matmul",
    )(a, w)


def _matmul_norm_res_kernel(a_ref, w_ref, g_ref, res_ref, *rest, with_next):
    if with_next:
        gnext_ref, o_ref, unext_ref, acc_ref = rest
    else:
        o_ref, acc_ref = rest
    k = pl.program_id(1)

    @pl.when(k == 0)
    def _():
        acc_ref[...] = jnp.zeros_like(acc_ref)

    acc_ref[...] += jnp.dot(a_ref[...], w_ref[...], preferred_element_type=F32)

    @pl.when(k == pl.num_programs(1) - 1)
    def _():
        y = acc_ref[...]
        ms = jnp.mean(y * y, axis=-1, keepdims=True)
        out = res_ref[...] + y * lax.rsqrt(ms + RMS_EPS) * g_ref[...]
        o_ref[...] = out
        if with_next:
            ms2 = jnp.mean(out * out, axis=-1, keepdims=True)
            unext_ref[...] = (out * lax.rsqrt(ms2 + RMS_EPS) * gnext_ref[...]).astype(BF16)


def _matmul_norm_res(a, w, g, res, next_gain=None, tm=512, tk=2048):
    m, kdim = a.shape
    n = w.shape[1]
    tm = min(tm, m)
    assert m % tm == 0 and kdim % tk == 0
    with_next = next_gain is not None
    vec = pl.BlockSpec((1, n), lambda i, k: (0, 0))
    rowblk = pl.BlockSpec((tm, n), lambda i, k: (i, 0))
    in_specs = [pl.BlockSpec((tm, tk), lambda i, k: (i, k)), pl.BlockSpec((tk, n), lambda i, k: (k, 0)), vec, rowblk]
    args = [a, w, g[None, :], res]
    out_shape = jax.ShapeDtypeStruct((m, n), F32)
    out_specs = rowblk
    if with_next:
        in_specs.append(vec)
        args.append(next_gain[None, :])
        out_shape = (out_shape, jax.ShapeDtypeStruct((m, n), BF16))
        out_specs = (rowblk, rowblk)
    return pl.pallas_call(
        functools.partial(_matmul_norm_res_kernel, with_next=with_next),
        out_shape=out_shape,
        grid=(m // tm, kdim // tk),
        in_specs=in_specs,
        out_specs=out_specs,
        scratch_shapes=[pltpu.VMEM((tm, n), F32)],
        compiler_params=_cparams(("parallel", "arbitrary")),
        name="matmul_norm_res",
    )(*args)


def _gelu_tanh(x):
    return 0.5 * x * (1.0 + jnp.tanh(math.sqrt(2.0 / math.pi) * (x + 0.044715 * (x * x * x))))


def _nsa_compress_kernel(x_ref, w1_ref, w2_ref, pe_ref, cos_ref, sin_ref, o_ref, *, use_rope):
    x = x_ref[0, 0]
    half = x.shape[1]
    n_chunk = x.shape[0]
    a = jnp.dot(x, w1_ref[0:half, :], preferred_element_type=F32)
    b = jnp.dot(x, w1_ref[half:2 * half, :], preferred_element_type=F32)
    pe = jnp.broadcast_to(pe_ref[...], (SUBLANES, pe_ref.shape[1]))
    peb = _dot_exact_rhs_general(pe, w1_ref[...])[0:1, :]
    h = a + pltpu.roll(b, n_chunk - 1, 0) + peb
    y = _bdot(_gelu_tanh(h), w2_ref[...])
    if use_rope:
        lane = lax.broadcasted_iota(jnp.int32, y.shape, 1)
        y = y * cos_ref[0] + _rope_partner(y, lane) * sin_ref[0]
    o_ref[0, 0] = y.astype(o_ref.dtype)


def _dot_exact_rhs_general(x, w_bf):
    hi, mid, lo = _split3(x)
    d = lambda p: jnp.dot(p, w_bf, preferred_element_type=F32)
    return d(hi) + (d(mid) + d(lo))


def _nsa_compress(x2, w1, w2, pe_flat, cosc, sinc, use_rope):
    b, g, n_chunk, width = x2.shape
    dh = w2.shape[1]
    return pl.pallas_call(
        functools.partial(_nsa_compress_kernel, use_rope=use_rope),
        out_shape=jax.ShapeDtypeStruct((b, g, n_chunk, dh), BF16),
        grid=(b, g),
        in_specs=[
            pl.BlockSpec((1, 1, n_chunk, width), lambda i, j: (i, j, 0, 0)),
            pl.BlockSpec(w1.shape, lambda i, j: (0, 0)),
            pl.BlockSpec(w2.shape, lambda i, j: (0, 0)),
            pl.BlockSpec(pe_flat.shape, lambda i, j: (0, 0)),
            pl.BlockSpec((1, n_chunk, dh), lambda i, j: (i, 0, 0)),
            pl.BlockSpec((1, n_chunk, dh), lambda i, j: (i, 0, 0)),
        ],
        out_specs=pl.BlockSpec((1, 1, n_chunk, dh), lambda i, j: (i, j, 0, 0)),
        compiler_params=_cparams(("parallel", "parallel")),
        name="nsa_compress",
    )(x2, w1, w2, pe_flat, cosc, sinc)


def _nsa_cmp_kernel(q_ref, kc_ref, vc_ref, ov_ref, o_ref, sel_ref, *, tq, top, n_sel):
    i = pl.program_id(2)
    kc = kc_ref[0, 0]
    vc = vc_ref[0, 0]
    ncp = kc.shape[0]
    t = i * tq + lax.broadcasted_iota(jnp.int32, (tq, ncp), 0)
    c = lax.broadcasted_iota(jnp.int32, (tq, ncp), 1)
    valid = (c * NSA_CMP_STRIDE + (NSA_CMP_LEN - 1)) <= t
    psum = jnp.zeros((tq, ncp), F32)
    for h in range(NSA_HPG):
        q = q_ref[0, :, h * HEAD_DIM:(h + 1) * HEAD_DIM]
        s = jnp.where(valid, _dot_nt(q, kc), NEG)
        m = jnp.max(s, axis=1, keepdims=True)
        e = jnp.where(valid, jnp.exp2(s - m), 0.0)
        l = jnp.sum(e, axis=1, keepdims=True)
        p = e / jnp.where(l > 0.0, l, 1.0)
        o_ref[0, :, h * HEAD_DIM:(h + 1) * HEAD_DIM] = _bdot(p, vc).astype(o_ref.dtype)
        psum = psum + p
    imp = jnp.transpose(_dot_exact_rhs(psum, ov_ref[...]))[0:n_sel]
    jj = lax.broadcasted_iota(jnp.int32, (n_sel, tq), 0)
    blk_t = jnp.right_shift(i * tq + lax.broadcasted_iota(jnp.int32, (n_sel, tq), 1), SEL_SHIFT)
    forced = (jj == 0) | (jj == blk_t) | (jj == blk_t - 1)
    imp = jnp.where(forced, 1e9, jnp.where(jj > blk_t, -1.0, imp))
    ng = n_sel // SUBLANES
    groups = [imp[g * SUBLANES:(g + 1) * SUBLANES] for g in range(ng)]
    ranks = [jnp.zeros((SUBLANES, tq), F32) for _ in range(ng)]
    sub = lax.broadcasted_iota(jnp.int32, (SUBLANES, tq), 0)
    for ii in range(n_sel):
        gi, ri = divmod(ii, SUBLANES)
        row = jnp.broadcast_to(imp[ii:ii + 1, :], (SUBLANES, tq))
        for g in range(ng):
            if g > gi:
                beats = row >= groups[g]
            elif g < gi:
                beats = row > groups[g]
            else:
                beats = (row > groups[g]) | ((row == groups[g]) & (sub > ri))
            ranks[g] = ranks[g] + jnp.where(beats, 1.0, 0.0)
    chosen = jnp.where(jnp.concatenate(ranks, axis=0) < float(top), 1.0, 0.0)
    pad = sel_ref.shape[3] - n_sel
    if pad:
        chosen = jnp.concatenate([chosen, jnp.zeros((pad, tq), F32)], axis=0)
    sel_ref[0, 0] = jnp.transpose(chosen).astype(sel_ref.dtype)


def _nsa_cmp(proj_rope, kcmp, vcmp, overlap, n_sel, tq):
    b, s, _ = proj_rope.shape
    g = kcmp.shape[1]
    ncp = kcmp.shape[2]
    n_pad = overlap.shape[1]
    top = min(NSA_SEL_TOPK, n_sel)
    qw = NSA_HPG * HEAD_DIM
    return pl.pallas_call(
        functools.partial(_nsa_cmp_kernel, tq=tq, top=top, n_sel=n_sel),
        out_shape=(jax.ShapeDtypeStruct((b, s, NSA_WIDTH), BF16),
                   jax.ShapeDtypeStruct((b, g, s, n_pad), BF16)),
        grid=(b, g, s // tq),
        in_specs=[
            pl.BlockSpec((1, tq, qw), lambda bi, gi, i: (bi, i, ROPE_Q // qw + gi)),
            pl.BlockSpec((1, 1, ncp, HEAD_DIM), lambda bi, gi, i: (bi, gi, 0, 0)),
            pl.BlockSpec((1, 1, ncp, HEAD_DIM), lambda bi, gi, i: (bi, gi, 0, 0)),
            pl.BlockSpec(overlap.shape, lambda bi, gi, i: (0, 0)),
        ],
        out_specs=(pl.BlockSpec((1, tq, qw), lambda bi, gi, i: (bi, i, gi)),
                   pl.BlockSpec((1, 1, tq, n_pad), lambda bi, gi, i: (bi, gi, i, 0))),
        compiler_params=_cparams(("parallel", "parallel", "parallel")),
        name="nsa_cmp_select",
    )(proj_rope, kcmp, vcmp, overlap)


def _nsa_flash_kernel(qi_ref, kb_ref, first_ref, *refs, mode, tq, tk):
    if mode == "sel":
        q_ref, k_ref, v_ref, sel_ref, hot_ref, o_ref, m_ref, l_ref, acc_ref = refs
    else:
        q_ref, k_ref, v_ref, o_ref, m_ref, l_ref, acc_ref = refs
    n = pl.program_id(2)
    i = qi_ref[n]
    kb = kb_ref[n]

    @pl.when(first_ref[n] == 1)
    def _():
        m_ref[...] = jnp.full_like(m_ref, NEG)
        l_ref[...] = jnp.zeros_like(l_ref)
        acc_ref[...] = jnp.zeros_like(acc_ref)

    ngrp = k_ref.shape[2] // HEAD_DIM

    def step(diag):
        valid = None
        if mode == "win":
            rows = i * tq + lax.broadcasted_iota(jnp.int32, (tq, tk), 0)
            cols = kb * tk + lax.broadcasted_iota(jnp.int32, (tq, tk), 1)
            d = rows - cols
            valid = (d >= 0) & (d < NSA_WINDOW)
        elif diag:
            valid = (lax.broadcasted_iota(jnp.int32, (tq, tk), 0) >= lax.broadcasted_iota(jnp.int32, (tq, tk), 1))
        for gi in range(ngrp):
            k = k_ref[0, :, gi * HEAD_DIM:(gi + 1) * HEAD_DIM]
            if mode == "sel":
                penalty = ((sel_ref[0, gi].astype(F32) - 1.0) * (-NEG)).astype(BF16)
                k = jnp.concatenate([k, hot_ref[...]], axis=1)
            v_ones = jnp.concatenate([v_ref[0, :, gi * HEAD_DIM:(gi + 1) * HEAD_DIM],
                                      jnp.ones((tk, LANES), BF16)], axis=1)
            for hd in range(NSA_HPG):
                slot = gi * NSA_HPG + hd
                q = q_ref[0, :, slot * HEAD_DIM:(slot + 1) * HEAD_DIM]
                if mode == "sel":
                    q = jnp.concatenate([q, penalty], axis=1)
                s = _dot_nt(q, k)
                if valid is not None:
                    s = jnp.where(valid, s, NEG)
                m_old = m_ref[slot]
                m_new = jnp.maximum(m_old, jnp.max(s, axis=1, keepdims=True))
                alpha = jnp.exp2(m_old - m_new)
                p = jnp.exp2(s - jnp.concatenate([m_new] * (tk // LANES), axis=1))
                pv = jnp.dot(p.astype(BF16), v_ones, preferred_element_type=F32)
                l_ref[slot] = alpha * l_ref[slot] + pv[:, HEAD_DIM:]
                acc_ref[slot] = alpha * acc_ref[slot] + pv[:, 0:HEAD_DIM]
                m_ref[slot] = m_new

    if mode == "sel":
        pl.when(kb < i)(functools.partial(step, False))
        pl.when(kb == i)(functools.partial(step, True))
    else:
        step(True)

    @pl.when(kb == i)
    def _():
        for slot in range(ngrp * NSA_HPG):
            o_ref[0, :, slot * HEAD_DIM:(slot + 1) * HEAD_DIM] = (acc_ref[slot] / l_ref[slot]).astype(o_ref.dtype)


def _nsa_flash(proj_rope, proj_plain, sel, mode, tq, tk):
    b, s, _ = proj_rope.shape
    g = NSA_GROUPS
    gpb = g
    qw = gpb * NSA_HPG * HEAD_DIM
    kw = gpb * HEAD_DIM
    nq = s // tq
    assert tq == tk
    if mode == "sel":
        back = nq
        koff, voff = ROPE_KS, PL_VS
    else:
        assert NSA_WINDOW % tk == 0
        back = NSA_WINDOW // tk
        koff, voff = ROPE_KW, PL_VW
    assert g % gpb == 0 and ROPE_Q % qw == 0 and koff % kw == 0 and voff % kw == 0
    qi, kb, first = _causal_pairs(nq, back)
    in_specs = [
        pl.BlockSpec((1, tq, qw), lambda bi, gi, n, qi, kb, fs: (bi, qi[n], ROPE_Q // qw + gi)),
        pl.BlockSpec((1, tk, kw), lambda bi, gi, n, qi, kb, fs: (bi, kb[n], koff // kw + gi)),
        pl.BlockSpec((1, tk, kw), lambda bi, gi, n, qi, kb, fs: (bi, kb[n], voff // kw + gi)),
    ]
    args = [proj_rope, proj_rope, proj_plain]
    if mode == "sel":
        n_pad = sel.shape[3]
        in_specs.append(pl.BlockSpec((1, gpb, tq, n_pad), lambda bi, gi, n, qi, kb, fs: (bi, gi, qi[n], 0)))
        in_specs.append(pl.BlockSpec((tk, n_pad), lambda bi, gi, n, qi, kb, fs: (kb[n], 0)))
        block_of_key = jnp.arange(s, dtype=jnp.int32)[:, None] // NSA_SEL_BLOCK
        onehot = (block_of_key == jnp.arange(n_pad, dtype=jnp.int32)[None, :]).astype(BF16)
        args += [sel, onehot]
    return pl.pallas_call(
        functools.partial(_nsa_flash_kernel, mode=mode, tq=tq, tk=tk),
        out_shape=jax.ShapeDtypeStruct((b, s, NSA_WIDTH), BF16),
        grid_spec=pltpu.PrefetchScalarGridSpec(
            num_scalar_prefetch=3,
            grid=(b, g // gpb, qi.shape[0]),
            in_specs=in_specs,
            out_specs=pl.BlockSpec((1, tq, qw), lambda bi, gi, n, qi, kb, fs: (bi, qi[n], gi)),
            scratch_shapes=[
                pltpu.VMEM((gpb * NSA_HPG, tq, LANES), F32),
                pltpu.VMEM((gpb * NSA_HPG, tq, LANES), F32),
                pltpu.VMEM((gpb * NSA_HPG, tq, HEAD_DIM), F32),
            ],
        ),
        compiler_params=_cparams(("parallel", "parallel", "arbitrary")),
        name="nsa_flash_" + mode,
    )(qi, kb, first, *args)


def _causal_pairs(nq, back):
    qi, kb, first = [], [], []
    for i in range(nq):
        lo = max(0, i - back)
        for j in range(lo, i + 1):
            qi.append(i)
            kb.append(j)
            first.append(1 if j == lo else 0)
    as_i32 = lambda z: jnp.asarray(np.asarray(z, np.int32))
    return as_i32(qi), as_i32(kb), as_i32(first)


def _diff_flash_kernel(qi_ref, kb_ref, first_ref, q_ref, k_ref, v_ref, lam_ref, sub_ref, o_ref, m_ref, l_ref,
                       acc_ref, *, tq, tk, lam_init):
    n = pl.program_id(2)
    i = qi_ref[n]
    kb = kb_ref[n]

    @pl.when(first_ref[n] == 1)
    def _():
        m_ref[...] = jnp.full_like(m_ref, NEG)
        l_ref[...] = jnp.zeros_like(l_ref)
        acc_ref[...] = jnp.zeros_like(acc_ref)

    hpb = o_ref.shape[2] // DIFF_VDIM

    def step(masked):
        if masked:
            valid = (lax.broadcasted_iota(jnp.int32, (tq, tk), 0) >= lax.broadcasted_iota(jnp.int32, (tq, tk), 1))
        for hd in range(hpb):
            v = v_ref[0, :, hd * DIFF_VDIM:(hd + 1) * DIFF_VDIM]
            for mp in range(2):
                slot = 2 * hd + mp
                q = q_ref[0, :, slot * HEAD_DIM:(slot + 1) * HEAD_DIM]
                k = k_ref[0, :, slot * HEAD_DIM:(slot + 1) * HEAD_DIM]
                s = _dot_nt(q, k)
                if masked:
                    s = jnp.where(valid, s, NEG)
                m_old = m_ref[slot]
                m_new = jnp.maximum(m_old, jnp.max(s, axis=1, keepdims=True))
                alpha = jnp.exp2(m_old - m_new)
                p = jnp.exp2(s - jnp.concatenate([m_new] * (tk // LANES), axis=1))
                l_ref[slot] = alpha * l_ref[slot] + jnp.sum(p, axis=1, keepdims=True)
                acc_ref[slot] = (jnp.concatenate([alpha] * (DIFF_VDIM // LANES), axis=1) * acc_ref[slot]
                                 + _bdot(p, v))
                m_ref[slot] = m_new

    pl.when(kb < i)(functools.partial(step, False))
    pl.when(kb == i)(functools.partial(step, True))

    @pl.when(kb == i)
    def _():
        lam = lam_ref[...]
        lam_full = (jnp.exp(jnp.sum(lam[0:1, :] * lam[1:2, :], axis=1, keepdims=True))
                    - jnp.exp(jnp.sum(lam[2:3, :] * lam[3:4, :], axis=1, keepdims=True)) + lam_init)
        wide = lambda x: jnp.concatenate([x] * (DIFF_VDIM // LANES), axis=1)
        for hd in range(hpb):
            o = (acc_ref[2 * hd] / wide(l_ref[2 * hd])
                 - lam_full * (acc_ref[2 * hd + 1] / wide(l_ref[2 * hd + 1])))
            ms = jnp.mean(o * o, axis=-1, keepdims=True)
            o = o * lax.rsqrt(ms + 1e-5) * sub_ref[...]
            o_ref[0, :, hd * DIFF_VDIM:(hd + 1) * DIFF_VDIM] = (o * (1.0 - lam_init)).astype(o_ref.dtype)


def _diff_flash(proj_rope, proj_plain, lam, subln, layer, tq, tk):
    b, s, _ = proj_rope.shape
    assert tq == tk
    nq = s // tq
    lam_init = 0.8 - 0.6 * math.exp(-0.3 * layer)
    hpb = DIFF_HEADS
    w = hpb * DIFF_VDIM
    assert DIFF_HEADS % hpb == 0 and ROPE_DQ % w == 0 and ROPE_DK % w == 0 and PL_DV % w == 0
    qi, kb, first = _causal_pairs(nq, nq)
    return pl.pallas_call(
        functools.partial(_diff_flash_kernel, tq=tq, tk=tk, lam_init=lam_init),
        out_shape=jax.ShapeDtypeStruct((b, s, DIFF_WIDTH), BF16),
        grid_spec=pltpu.PrefetchScalarGridSpec(
            num_scalar_prefetch=3,
            grid=(b, DIFF_HEADS // hpb, qi.shape[0]),
            in_specs=[
                pl.BlockSpec((1, tq, w), lambda bi, h, n, qi, kb, fs: (bi, qi[n], ROPE_DQ // w + h)),
                pl.BlockSpec((1, tk, w), lambda bi, h, n, qi, kb, fs: (bi, kb[n], ROPE_DK // w + h)),
                pl.BlockSpec((1, tk, w), lambda bi, h, n, qi, kb, fs: (bi, kb[n], PL_DV // w + h)),
                pl.BlockSpec(lam.shape, lambda bi, h, n, qi, kb, fs: (0, 0)),
                pl.BlockSpec((1, DIFF_VDIM), lambda bi, h, n, qi, kb, fs: (0, 0)),
            ],
            out_specs=pl.BlockSpec((1, tq, w), lambda bi, h, n, qi, kb, fs: (bi, qi[n], h)),
            scratch_shapes=[
                pltpu.VMEM((2 * hpb, tq, LANES), F32),
                pltpu.VMEM((2 * hpb, tq, LANES), F32),
                pltpu.VMEM((2 * hpb, tq, DIFF_VDIM), F32),
            ],
        ),
        compiler_params=_cparams(("parallel", "parallel", "arbitrary")),
        name="diff_flash",
    )(qi, kb, first, proj_rope, proj_rope, proj_plain, lam, subln[None, :])


def _bmm(a, b):
    return jnp.einsum("umk,ukn->umn", a.astype(BF16), b.astype(BF16), preferred_element_type=F32)


def _bmm_nt(a, b):
    return jnp.einsum("umk,unk->umn", a.astype(BF16), b.astype(BF16), preferred_element_type=F32)


def _rwkv_scan_kernel(r_ref, k_ref, v_ref, gd_ref, wa_ref, mur_ref, muk_ref, muv_ref, mugd_ref, muwa_ref,
                      wdec_ref, wrate_ref, gup_ref, w0_ref, a0_ref, kk_ref, ka_ref, rk_ref, lnw_ref, lnb_ref,
                      o_ref, st_ref, last_ref, lastgd_ref, *, ts, chunk):
    @pl.when(pl.program_id(2) == 0)
    def _():
        st_ref[...] = jnp.zeros_like(st_ref)
        last_ref[...] = jnp.zeros_like(last_ref)
        lastgd_ref[...] = jnp.zeros_like(lastgd_ref)

    def shifted(x_ref, mu_ref, carry):
        x = x_ref[0]
        row = lax.broadcasted_iota(jnp.int32, x.shape, 0)
        prev = jnp.where(row == 0, carry, pltpu.roll(x, 1, 0))
        return x + (prev - x) * mu_ref[...]

    r = shifted(r_ref, mur_ref, last_ref[0, 0:1, :])
    k = shifted(k_ref, muk_ref, last_ref[1, 0:1, :])
    v = shifted(v_ref, muv_ref, last_ref[2, 0:1, :])
    wa = shifted(wa_ref, muwa_ref, last_ref[3, 0:1, :])
    gd = shifted(gd_ref, mugd_ref, lastgd_ref[0:1, :])
    for n, ref in enumerate((r_ref, k_ref, v_ref, wa_ref)):
        last_ref[n, 0:1, :] = ref[0, ts - 1:ts, :]
    lastgd_ref[0:1, :] = gd_ref[0, ts - 1:ts, :]

    wa = jnp.where(lax.broadcasted_iota(jnp.int32, wa.shape, 1) < RWKV_W_LORA, jnp.tanh(wa), wa)
    lw = -math.exp(-0.5) * jax.nn.sigmoid(w0_ref[...] + _bdot(wa, wdec_ref[...]))
    a = jax.nn.sigmoid(a0_ref[...] + _bdot(wa, wrate_ref[...]))
    g = _bdot(jax.nn.sigmoid(gd), gup_ref[...])

    c = chunk
    nc = ts // c
    head0 =lax.broadcasted_iota(jnp.int32, (ts, LANES), 1) < RWKV_HEAD

    def seg_sum(x):
        s0 = jnp.sum(jnp.where(head0, x, 0.0), axis=1, keepdims=True)
        s1 = jnp.sum(jnp.where(head0, 0.0, x), axis=1, keepdims=True)
        return jnp.where(head0, s0, s1)

    kk = k * kk_ref[...]
    kap = kk / jnp.maximum(jnp.sqrt(seg_sum(kk * kk)), 1e-12)
    kmod = k * (1.0 + (a - 1.0) * ka_ref[...])
    bvec = kap * a

    row_in_chunk = lax.broadcasted_iota(jnp.int32, (ts, LANES), 0) & (c - 1)
    cum = lw
    step = 1
    while step < c:
        cum = cum + jnp.where(row_in_chunk >= step, pltpu.roll(cum, step, 0), 0.0)
        step *= 2
    g_incl = jnp.exp(cum)
    g_inv = jnp.exp(-cum)
    split = lambda x: x.reshape(nc, c, LANES)
    g_last = [g_incl[(ch + 1) * c - 1:(ch + 1) * c, :] for ch in range(nc)]
    g_last_rows = jnp.concatenate([jnp.broadcast_to(gl, (c, LANES)) for gl in g_last], axis=0)
    kh = split(kmod * g_inv)
    bh = split(bvec * g_inv)
    kg = split(kmod * g_inv * g_last_rows)
    bg = split(bvec * g_inv * g_last_rows)
    kaph = kap * jnp.exp(cum - lw)
    rh = r * g_incl
    v3 = split(v)

    assert 2 * c == LANES
    h0 = lax.broadcasted_iota(jnp.int32, (nc, c, LANES), 2) < RWKV_HEAD
    stack = lambda y: jnp.concatenate([jnp.where(h0, y, 0.0), jnp.where(h0, 0.0, y)], axis=1)
    kap3 = split(kaph)
    r3 = split(rh)
    gram = _bmm_nt(jnp.concatenate([kap3, r3], axis=1),
                   jnp.concatenate([stack(bh), stack(kh)], axis=1))
    row = lax.broadcasted_iota(jnp.int32, (c, 4 * c), 0)
    col = lax.broadcasted_iota(jnp.int32, (c, 4 * c), 1) & (c - 1)
    top = jnp.where(row > col, gram[:, 0:c, :], 0.0)
    abk = jnp.where(row >= col, gram[:, c:2 * c, :], 0.0)
    lb = top[:, :, 0:LANES]
    lk = top[:, :, LANES:2 * LANES]
    tm = -lb
    q = _bmm(lb, stack(lb))
    n = 2
    while True:
        tm = tm + q + _bmm(tm, stack(q))
        n *= 2
        if n >= c:
            break
        q = _bmm(q, stack(q))
    kap_t = kap3 + _bmm(tm, stack(kap3))
    lkv = _bmm(lk, stack(v3))
    z0 = -(lkv + _bmm(tm, stack(lkv)))
    r_t = r3 - _bmm(abk[:, :, 0:LANES], stack(kap_t))
    y0 = _bmm(abk, jnp.concatenate([stack(z0), stack(v3)], axis=1))

    br = lax.broadcasted_iota(jnp.int32, (LANES, LANES), 0)
    bc = lax.broadcasted_iota(jnp.int32, (LANES, LANES), 1)
    blockdiag = (br < RWKV_HEAD) == (bc < RWKV_HEAD)
    state = st_ref[...]
    ys = []
    for ch in range(nc):
        trans = jnp.where(blockdiag, -_dot_tn(bg[ch], kap_t[ch]), 0.0)
        trans = trans + jnp.where(br == bc, jnp.broadcast_to(g_last[ch], (LANES, LANES)), 0.0)
        inject = jnp.where(blockdiag, _dot_tn(jnp.concatenate([bg[ch], kg[ch]], axis=0),
                                              jnp.concatenate([z0[ch], v3[ch]], axis=0)), 0.0)
        ys.append(_bdot(r_t[ch], state) + y0[ch])
        state = _dot3(trans, state) + inject
    st_ref[...] = state
    y = jnp.concatenate(ys, axis=0)

    mean = seg_sum(y) * (1.0 / RWKV_HEAD)
    yc = y - mean
    var = seg_sum(yc * yc) * (1.0 / RWKV_HEAD)
    yn = yc * lax.rsqrt(var + RWKV_LNX_EPS) * lnw_ref[...] + lnb_ref[...]
    bonus = seg_sum(r * kmod * rk_ref[...]) * v
    o_ref[0] = ((yn + bonus) * g).astype(o_ref.dtype)


def _rwkv_scan(proj3, mu_p, wwa, gup, w0, a0, k_k, k_a, r_k, lnx_w, lnx_b, ts=1024, chunk=64):
    b, s, _ = proj3.shape
    w = RWKV_WIDTH
    ts = min(ts, s)
    assert s % ts == 0 and ts % chunk == 0
    npair = w // LANES
    col = lambda off: pl.BlockSpec((1, ts, LANES), lambda bi, j, t: (bi, t, off // LANES + j))
    mucol = lambda off: pl.BlockSpec((1, LANES), lambda bi, j, t: (0, off // LANES + j))
    vec = pl.BlockSpec((1, LANES), lambda bi, j, t: (0, j))
    in_specs = [
        col(0), col(w), col(2 * w),
        pl.BlockSpec((1, ts, RW_G_PAD), lambda bi, j, t: (bi, t, RW_GD // RW_G_PAD)),
        pl.BlockSpec((1, ts, RW_LORA_PAD), lambda bi, j, t: (bi, t, RW_WA // RW_LORA_PAD)),
        mucol(0), mucol(w), mucol(2 * w),
        pl.BlockSpec((1, RW_G_PAD), lambda bi, j, t: (0, RW_GD // RW_G_PAD)),
        pl.BlockSpec((1, RW_LORA_PAD), lambda bi, j, t: (0, RW_WA // RW_LORA_PAD)),
        pl.BlockSpec((RW_LORA_PAD, LANES), lambda bi, j, t: (0, j)),
        pl.BlockSpec((RW_LORA_PAD, LANES), lambda bi, j, t: (0, npair + j)),
        pl.BlockSpec((RW_G_PAD, LANES), lambda bi, j, t: (0, j)),
    ] + [vec] * 7
    row = lambda z: z[None, :]
    return pl.pallas_call(
        functools.partial(_rwkv_scan_kernel, ts=ts, chunk=chunk),
        out_shape=jax.ShapeDtypeStruct((b, s, w), BF16),
        grid=(b, npair, s // ts),
        in_specs=in_specs,
        out_specs=pl.BlockSpec((1, ts, LANES), lambda bi, j, t: (bi, t, j)),
        scratch_shapes=[
            pltpu.VMEM((LANES, LANES), F32),
            pltpu.VMEM((4, SUBLANES, LANES), F32),
            pltpu.VMEM((SUBLANES, RW_G_PAD), F32),
        ],
        compiler_params=_cparams(("parallel", "parallel", "arbitrary")),
        name="rwkv_scan",
    )(proj3, proj3, proj3, proj3, proj3, mu_p, mu_p, mu_p, mu_p, mu_p, wwa, wwa, gup,
      row(w0), row(a0), row(k_k), row(k_a), row(r_k), row(lnx_w), row(lnx_b))


def _merge_kernel(oc_ref, os_ref, ow_ref, ng_ref, yd_ref, yr_ref, wb_ref, g0_ref, g1_ref, g2_ref, bias_ref,
                  o_ref, yn_ref):
    @pl.when(pl.program_id(1) == 0)
    def _():
        gates = jax.nn.sigmoid(ng_ref[...])
        for hd in range(NSA_HEADS):
            sl = slice(hd * HEAD_DIM, (hd + 1) * HEAD_DIM)
            o = (gates[:, 3 * hd:3 * hd + 1] * oc_ref[:, sl]
                 + gates[:, 3 * hd + 1:3 * hd + 2] * os_ref[:, sl]
                 + gates[:, 3 * hd + 2:3 * hd + 3] * ow_ref[:, sl])
            yn_ref[:, sl] = o.astype(BF16)

    branches = (yn_ref, yd_ref, yr_ref)
    graw = (g0_ref, g1_ref, g2_ref)
    acc = None
    for bi in range(N_BRANCH):
        gate = jax.nn.sigmoid(graw[bi][...] + bias_ref[bi])
        term = gate * jnp.dot(branches[bi][...], wb_ref[bi], preferred_element_type=F32)
        acc = term if acc is None else acc + term
    o_ref[...] = acc.astype(o_ref.dtype)


def _merge(o_cmp, o_slc, o_win, proj_f32, proj_plain, y_diff, y_rwkv, w_branch, b_gate, tm=1024, tn=512):
    m = o_cmp.shape[0]
    n = w_branch.shape[2]
    tm = min(tm, m)
    assert m % tm == 0 and n % tn == 0 and PL_BG % tn == 0
    bw = BRANCH_WIDTH
    rowblk = lambda: pl.BlockSpec((tm, bw), lambda i, j: (i, 0))
    gate_spec = lambda bi: pl.BlockSpec((tm, tn), lambda i, j: (i, PL_BG // tn + bi * (n // tn) + j))
    return pl.pallas_call(
        _merge_kernel,
        out_shape=jax.ShapeDtypeStruct((m, n), BF16),
        grid=(m // tm, n // tn),
        in_specs=[
            rowblk(), rowblk(), rowblk(),
            pl.BlockSpec((tm, LANES), lambda i, j: (i, F_NG // LANES)),
            rowblk(), rowblk(),
            pl.BlockSpec((N_BRANCH, bw, tn), lambda i, j: (0, 0, j)),
            gate_spec(0), gate_spec(1), gate_spec(2),
            pl.BlockSpec((N_BRANCH, 1, tn), lambda i, j: (0, 0, j)),
        ],
        out_specs=pl.BlockSpec((tm, tn), lambda i, j: (i, j)),
        scratch_shapes=[pltpu.VMEM((tm, bw), BF16)],
        compiler_params=_cparams(("parallel", "arbitrary")),
        name="gated_merge",
    )(o_cmp, o_slc, o_win, proj_f32, y_diff, y_rwkv, w_branch, proj_plain, proj_plain, proj_plain,
      b_gate[:, None, :])


def _pack_layer_weights(w_in, w_gate, rwkv_mu, w_up_lora, a_up_lora, g_up_lora):
    d = w_in.shape[0]
    nsa_sizes = (NSA_WIDTH,) + (NSA_KV,) * 6 + (3 * NSA_HEADS,)
    diff_sizes = (DIFF_QK, DIFF_QK, DIFF_WIDTH)
    rw_sizes = (RWKV_WIDTH,) * 3 + (RWKV_W_LORA, RWKV_A_LORA, RWKV_G_LORA)
    offs = np.cumsum((0,) + nsa_sizes + diff_sizes + rw_sizes)
    seg = [w_in[:, offs[i]:offs[i + 1]] for i in range(len(offs) - 1)]
    q, kc, vc, ks, vs, kw, vw, ng, dq, dk, dv, rr, rk, rv, wd, ad, gd = seg
    w_rope = jnp.concatenate([q, dq, dk, ks, kw], axis=1).astype(BF16)
    w_plain = jnp.concatenate([kc, vc, vs, vw, dv] + [w_gate[bi] for bi in range(N_BRANCH)], axis=1).astype(BF16)
    zpad = lambda n: jnp.zeros((d, n), w_in.dtype)
    w_f32 = jnp.concatenate(
        [rr, rk, rv, gd, zpad(RW_G_PAD - RWKV_G_LORA), wd, ad, ng, zpad(LANES - 3 * NSA_HEADS)],
        axis=1).astype(BF16)
    mu_rkv, mu_wa, mu_gd = jnp.split(rwkv_mu, [3 * RWKV_WIDTH, 3 * RWKV_WIDTH + RW_LORA_PAD])
    mu_p = jnp.concatenate([mu_rkv, mu_gd, jnp.zeros((RW_G_PAD - RWKV_G_LORA,), rwkv_mu.dtype), mu_wa])[None, :]
    zz = jnp.zeros((RWKV_W_LORA, RWKV_WIDTH), w_up_lora.dtype)
    wwa = jnp.concatenate([jnp.concatenate([w_up_lora, zz], axis=1),
                           jnp.concatenate([zz, a_up_lora], axis=1)], axis=0).astype(BF16)
    gup = jnp.concatenate([g_up_lora, jnp.zeros((RW_G_PAD - RWKV_G_LORA, RWKV_WIDTH), g_up_lora.dtype)],
                          axis=0).astype(BF16)
    return w_rope, w_plain, w_f32, mu_p, wwa, gup


def _overlap_matrix(n_cp, n_sel):
    c_start = np.arange(n_cp) * NSA_CMP_STRIDE
    c_end = c_start + NSA_CMP_LEN - 1
    j_start = np.arange(n_sel) * NSA_SEL_BLOCK
    ov = (c_start[:, None] <= j_start[None, :] + NSA_SEL_BLOCK - 1) & (c_end[:, None] >= j_start[None, :])
    ov = np.pad(ov.astype(np.float32), ((0, 0), (0, -n_sel % LANES)))
    return jnp.asarray(ov).astype(BF16)


def _chunk_rows(z, groups):
    b, s, _ = z.shape
    z = z.reshape(b, s // NSA_CMP_STRIDE, NSA_CMP_STRIDE, groups, HEAD_DIM)
    return z.transpose(0, 3, 1, 2, 4).reshape(b, groups, s // NSA_CMP_STRIDE, NSA_CMP_STRIDE * HEAD_DIM)


def _nsa_branches(proj_rope, proj_plain, positions, cmp_pos, cmp_w1, cmp_w2, tq):
    b, s, _ = proj_rope.shape
    n_chunk = s // NSA_CMP_STRIDE
    n_sel = s // NSA_SEL_BLOCK
    kc2 = _chunk_rows(proj_plain[:, :, PL_KC:PL_KC + NSA_KV], NSA_GROUPS)
    vc2 = _chunk_rows(proj_plain[:, :, PL_VC:PL_VC + NSA_KV], NSA_GROUPS)
    cmp_end = np.minimum(np.arange(n_chunk) * NSA_CMP_STRIDE + NSA_CMP_LEN - 1, s - 1)
    pos_c = jnp.take(positions, jnp.asarray(cmp_end), axis=1)
    cosc, sinc = _rope_tables(pos_c.reshape(-1))
    cosc = cosc.reshape(b, n_chunk, LANES)
    sinc = sinc.reshape(b, n_chunk, LANES)
    w1 = cmp_w1.astype(BF16)
    w2 = cmp_w2.astype(BF16)
    pe = cmp_pos.reshape(2, 1, NSA_CMP_LEN * HEAD_DIM)
    kcmp = _nsa_compress(kc2, w1[0], w2[0], pe[0], cosc, sinc, True)
    vcmp = _nsa_compress(vc2, w1[1], w2[1], pe[1], cosc, sinc, False)
    o_cmp, sel = _nsa_cmp(proj_rope, kcmp, vcmp, _overlap_matrix(n_chunk, n_sel), n_sel, tq)
    o_slc = _nsa_flash(proj_rope, proj_plain, sel, "sel", tq, tq)
    o_win = _nsa_flash(proj_rope, proj_plain, None, "win", tq, tq)
    return o_cmp, o_slc, o_win


def _layer(x2, b, s, layer, cosf, sinf, positions, p):
    m = b * s
    w_rope, w_plain, w_f32, mu_p, wwa, gup = _pack_layer_weights(
        p["w_in"], p["w_gate"], p["rwkv_mu"], p["rwkv_w_up"], p["rwkv_a_up"], p["rwkv_g_up"])
    scale = HEAD_DIM ** -0.5 * math.log2(math.e)
    colscale = jnp.concatenate([jnp.full((ROPE_SCALED,), scale, F32),
                                jnp.ones((ROPE_COLS - ROPE_SCALED,), F32)])[None, :]
    g_pre = p["norm_pre_mix"]
    proj_rope, u = _norm_matmul_rope(x2, g_pre, w_rope, cosf, sinf, colscale, tm=512, tn=ROPE_COLS // 2)
    proj_plain = _matmul(u, w_plain, BF16, tm=2048, tn=1024)
    proj_f32 = _matmul(u, w_f32, F32, tm=1024, tn=F32_COLS // 2)
    pr3 = proj_rope.reshape(b, s, ROPE_COLS)
    pp3 = proj_plain.reshape(b, s, PLAIN_COLS)

    tq = min(512, s)
    o_cmp, o_slc, o_win = _nsa_branches(pr3, pp3, positions, p["nsa_cmp_pos"], p["nsa_cmp_w1"], p["nsa_cmp_w2"], tq)
    y_diff = _diff_flash(pr3, pp3, p["diff_lambda"], p["diff_subln"], layer, tq, tq)

    y_rwkv = _rwkv_scan(proj_f32.reshape(b, s, F32_COLS), mu_p, wwa, gup, p["rwkv_w0"], p["rwkv_a0"],
                        p["rwkv_k_k"], p["rwkv_k_a"], p["rwkv_r_k"].reshape(-1), p["rwkv_lnx_w"], p["rwkv_lnx_b"])

    merged = _merge(o_cmp.reshape(m, NSA_WIDTH), o_slc.reshape(m, NSA_WIDTH), o_win.reshape(m, NSA_WIDTH),
                    proj_f32, proj_plain, y_diff.reshape(m, DIFF_WIDTH), y_rwkv.reshape(m, RWKV_WIDTH),
                    p["w_branch"].astype(BF16), p["b_gate"])
    x2, u_mlp = _matmul_norm_res(merged, p["w_out"].astype(BF16), p["norm_post_mix"], x2,
                                 next_gain=p["norm_pre_mlp"])
    hidden = _matmul(u_mlp, p["w_up"], BF16, tm=2048, tn=1024, relu2=True)
    x2 = _matmul_norm_res(hidden, p["w_down"].astype(BF16), p["norm_post_mlp"], x2)
    return x2


def kernel(x, positions, norm_pre_mix, norm_post_mix, norm_pre_mlp, norm_post_mlp, w_in, nsa_cmp_pos, nsa_cmp_w1, nsa_cmp_w2, diff_lambda, diff_subln, rwkv_mu, rwkv_w0, rwkv_w_up, rwkv_a0, rwkv_a_up, rwkv_g_up, rwkv_k_k, rwkv_k_a, rwkv_r_k, rwkv_lnx_w, rwkv_lnx_b, w_gate, b_gate, w_branch, w_out, w_up, w_down):
    b, s, d = x.shape
    depth = w_in.shape[0]
    stacked = dict(
        norm_pre_mix=norm_pre_mix, norm_post_mix=norm_post_mix, norm_pre_mlp=norm_pre_mlp,
        norm_post_mlp=norm_post_mlp, w_in=w_in, nsa_cmp_pos=nsa_cmp_pos, nsa_cmp_w1=nsa_cmp_w1,
        nsa_cmp_w2=nsa_cmp_w2, diff_lambda=diff_lambda, diff_subln=diff_subln, rwkv_mu=rwkv_mu,
        rwkv_w0=rwkv_w0, rwkv_w_up=rwkv_w_up, rwkv_a0=rwkv_a0, rwkv_a_up=rwkv_a_up, rwkv_g_up=rwkv_g_up,
        rwkv_k_k=rwkv_k_k, rwkv_k_a=rwkv_k_a, rwkv_r_k=rwkv_r_k, rwkv_lnx_w=rwkv_lnx_w,
        rwkv_lnx_b=rwkv_lnx_b, w_gate=w_gate, b_gate=b_gate, w_branch=w_branch, w_out=w_out,
        w_up=w_up, w_down=w_down)
    cosf, sinf = _rope_tables(positions.reshape(-1))
    x2 = x.reshape(b * s, d)
    for layer in range(depth):
        x2 = _layer(x2, b, s, layer, cosf, sinf, positions, {n: a[layer] for n, a in stacked.items()})
    return x2.reshape(b, s, d)
```

```python
import functools
import math

import jax
import jax.numpy as jnp
import numpy as np
from jax import lax
from jax.experimental import pallas as pl
from jax.experimental.pallas import tpu as pltpu

F32 = jnp.float32
BF16 = jnp.bfloat16

D_MODEL = 2048
RMS_EPS = 1e-6
ROPE_THETA = 500000.0
HEAD_DIM = 128
ROT_HALF = HEAD_DIM // 8
NSA_HEADS = 8
NSA_GROUPS = 2
NSA_HPG = NSA_HEADS // NSA_GROUPS
NSA_CMP_LEN = 32
NSA_CMP_STRIDE = 16
NSA_CMP_HIDDEN = 256
NSA_SEL_BLOCK = 64
SEL_SHIFT = NSA_SEL_BLOCK.bit_length() - 1
NSA_SEL_TOPK = 16
NSA_WINDOW = 512
NSA_WIDTH = NSA_HEADS * HEAD_DIM
NSA_KV = NSA_GROUPS * HEAD_DIM
DIFF_HEADS = 4
DIFF_VDIM = 2 * HEAD_DIM
DIFF_WIDTH = DIFF_HEADS * DIFF_VDIM
DIFF_QK = 2 * DIFF_HEADS * HEAD_DIM
RWKV_HEAD = 64
RWKV_WIDTH = 1024
RWKV_W_LORA = 64
RWKV_A_LORA = 64
RWKV_G_LORA = 160
RWKV_LNX_EPS = 64e-5
N_BRANCH = 3
BRANCH_WIDTH = 1024

LANES = 128
SUBLANES = 8
VMEM_LIMIT_BYTES = 56 * 1024 * 1024

NEG = -1e30

ROPE_Q, ROPE_DQ, ROPE_DK, ROPE_KS, ROPE_KW = 0, 1024, 2048, 3072, 3328
ROPE_SCALED = ROPE_DK
ROPE_COLS = 3584
PL_KC, PL_VC, PL_VS, PL_VW, PL_DV, PL_BG = 0, 256, 512, 768, 1024, 2048
PLAIN_COLS = PL_BG + N_BRANCH * D_MODEL
RW_LORA_PAD = 128
RW_G_PAD = 256
RW_GD = 3 * RWKV_WIDTH
RW_WA = RW_GD + RW_G_PAD
RW_COLS = RW_WA + RW_LORA_PAD
F_NG = RW_COLS
F32_COLS = RW_COLS + LANES


def _cparams(sem):
    return pltpu.CompilerParams(dimension_semantics=sem, vmem_limit_bytes=VMEM_LIMIT_BYTES)


def _bdot(a, b):
    return jnp.dot(a.astype(BF16), b.astype(BF16), preferred_element_type=F32)


def _dot_nt(a, b):
    return lax.dot_general(a.astype(BF16), b.astype(BF16), (((1,), (1,)), ((), ())),
                           preferred_element_type=F32)


def _dot_tn(a, b):
    return lax.dot_general(a.astype(BF16), b.astype(BF16), (((0,), (0,)), ((), ())),
                           preferred_element_type=F32)


def _split2(x):
    hi = x.astype(BF16)
    lo = (x - hi.astype(F32)).astype(BF16)
    return hi, lo


def _split3(x):
    hi = x.astype(BF16)
    r1 = x - hi.astype(F32)
    mid = r1.astype(BF16)
    lo = (r1 - mid.astype(F32)).astype(BF16)
    return hi, mid, lo


def _dot_exact_lhs(a_bf, x):
    hi, mid, lo = _split3(x)
    d = lambda p: jnp.dot(a_bf, p, preferred_element_type=F32)
    return d(hi) + (d(mid) + d(lo))


def _dot_exact_rhs(x, b_bf):
    hi, mid, lo = _split3(x)
    d = lambda p: jnp.dot(p, b_bf, preferred_element_type=F32)
    return d(hi) + (d(mid) + d(lo))


def _dot3(a, b):
    ah, al = _split2(a)
    bh, bl = _split2(b)
    d = lambda p, q: jnp.dot(p, q, preferred_element_type=F32)
    return d(ah, bh) + (d(ah, bl) + d(al, bh))


def _rope_partner(z, lane):
    return jnp.where(lane < ROT_HALF, pltpu.roll(z, LANES - ROT_HALF, 1), pltpu.roll(z, ROT_HALF, 1))


def _rope_table_kernel(pos_ref, invf_ref, sign_ref, cos_ref, sin_ref):
    ang = pos_ref[...].astype(F32) * invf_ref[...]
    cos_ref[...] = jnp.cos(ang)
    sin_ref[...] = jnp.sin(ang) * sign_ref[...]


def _rope_tables(pos_flat):
    n = pos_flat.shape[0]
    half = ROT_HALF
    inv_freq = ROPE_THETA ** (-jnp.arange(half, dtype=F32) / half)
    zeros = jnp.zeros((LANES - 2 * half,), F32)
    invf = jnp.concatenate([inv_freq, inv_freq, zeros])[None, :]
    sign = jnp.concatenate([-jnp.ones((half,), F32), jnp.ones((half,), F32), zeros])[None, :]
    tm = min(n, 2048)
    assert n % tm == 0
    vec = pl.BlockSpec((1, LANES), lambda i: (0, 0))
    out = pl.BlockSpec((tm, LANES), lambda i: (i, 0))
    return pl.pallas_call(
        _rope_table_kernel,
        out_shape=(jax.ShapeDtypeStruct((n, LANES), F32),) * 2,
        grid=(n // tm,),
        in_specs=[pl.BlockSpec((tm, 1), lambda i: (i, 0)), vec, vec],
        out_specs=(out, out),
        compiler_params=_cparams(("parallel",)),
        name="rope_tables",
    )(pos_flat[:, None], invf, sign)


def _norm_matmul_rope_kernel(x_ref, g_ref, w_ref, cos_ref, sin_ref, cs_ref, o_ref, u_ref, *, tn):
    @pl.when(pl.program_id(1) == 0)
    def _():
        x = x_ref[...]
        ms = jnp.mean(x * x, axis=-1, keepdims=True)
        u_ref[...] = (x * lax.rsqrt(ms + RMS_EPS) * g_ref[...]).astype(BF16)

    acc = jnp.dot(u_ref[...], w_ref[...], preferred_element_type=F32) * cs_ref[...]
    cosv = cos_ref[...]
    sinv = sin_ref[...]
    lane = lax.broadcasted_iota(jnp.int32, cosv.shape, 1)
    for h in range(tn // LANES):
        z = acc[:, h * LANES:(h + 1) * LANES]
        o_ref[:, h * LANES:(h + 1) * LANES] = (z * cosv + _rope_partner(z, lane) * sinv).astype(o_ref.dtype)


def _norm_matmul_rope(x, g, w, cosf, sinf, colscale, tm, tn):
    m, d = x.shape
    n = w.shape[1]
    tm = min(tm, m)
    assert m % tm == 0 and n % tn == 0
    rows = pl.BlockSpec((tm, LANES), lambda i, j: (i, 0))
    return pl.pallas_call(
        functools.partial(_norm_matmul_rope_kernel, tn=tn),
        out_shape=(jax.ShapeDtypeStruct((m, n), BF16), jax.ShapeDtypeStruct((m, d), BF16)),
        grid=(m // tm, n // tn),
        in_specs=[
            pl.BlockSpec((tm, d), lambda i, j: (i, 0)),
            pl.BlockSpec((1, d), lambda i, j: (0, 0)),
            pl.BlockSpec((d, tn), lambda i, j: (0, j)),
            rows, rows,
            pl.BlockSpec((1, tn), lambda i, j: (0, j)),
        ],
        out_specs=(pl.BlockSpec((tm, tn), lambda i, j: (i, j)), pl.BlockSpec((tm, d), lambda i, j: (i, 0))),
        compiler_params=_cparams(("parallel", "arbitrary")),
        name="norm_matmul_rope",
    )(x, g[None, :], w, cosf, sinf, colscale)


def _matmul_kernel(a_ref, w_ref, o_ref, *, relu2):
    acc = jnp.dot(a_ref[...], w_ref[...], preferred_element_type=F32)
    if relu2:
        acc = jnp.square(jnp.maximum(acc, 0.0))
    o_ref[...] = acc.astype(o_ref.dtype)


def _matmul(a, w, out_dtype, tm, tn, relu2=False):
    m, kdim = a.shape
    n = w.shape[1]
    tm = min(tm, m)
    assert m % tm == 0 and n % tn == 0
    return pl.pallas_call(
        functools.partial(_matmul_kernel, relu2=relu2),
        out_shape=jax.ShapeDtypeStruct((m, n), out_dtype),
        grid=(m // tm, n // tn),
        in_specs=[pl.BlockSpec((tm, kdim), lambda i, j: (i, 0)), pl.BlockSpec((kdim, tn), lambda i, j: (0, j))],
        out_specs=pl.BlockSpec((tm, tn), lambda i, j: (i, j)),
        compiler_params=_cparams(("parallel", "parallel")),
        name="matmul",
    )(a, w)


def _matmul_norm_res_kernel(a_ref, w_ref, g_ref, res_ref, *rest, with_next):
    if with_next:
        gnext_ref, o_ref, unext_ref, acc_ref = rest
    else:
        o_ref, acc_ref = rest
    k = pl.program_id(1)

    @pl.when(k == 0)
    def _():
        acc_ref[...] = jnp.zeros_like(acc_ref)

    acc_ref[...] += jnp.dot(a_ref[...], w_ref[...], preferred_element_type=F32)

    @pl.when(k == pl.num_programs(1) - 1)
    def _():
        y = acc_ref[...]
        ms = jnp.mean(y * y, axis=-1, keepdims=True)
        out = res_ref[...] + y * lax.rsqrt(ms + RMS_EPS) * g_ref[...]
        o_ref[...] = out
        if with_next:
            ms2 = jnp.mean(out * out, axis=-1, keepdims=True)
            unext_ref[...] = (out * lax.rsqrt(ms2 + RMS_EPS) * gnext_ref[...]).astype(BF16)


def _matmul_norm_res(a, w, g, res, next_gain=None, tm=512, tk=2048):
    m, kdim = a.shape
    n = w.shape[1]
    tm = min(tm, m)
    assert m % tm == 0 and kdim % tk == 0
    with_next = next_gain is not None
    vec = pl.BlockSpec((1, n), lambda i, k: (0, 0))
    rowblk = pl.BlockSpec((tm, n), lambda i, k: (i, 0))
    in_specs = [pl.BlockSpec((tm, tk), lambda i, k: (i, k)), pl.BlockSpec((tk, n), lambda i, k: (k, 0)), vec, rowblk]
    args = [a, w, g[None, :], res]
    out_shape = jax.ShapeDtypeStruct((m, n), F32)
    out_specs = rowblk
    if with_next:
        in_specs.append(vec)
        args.append(next_gain[None, :])
        out_shape = (out_shape, jax.ShapeDtypeStruct((m, n), BF16))
        out_specs = (rowblk, rowblk)
    return pl.pallas_call(
        functools.partial(_matmul_norm_res_kernel, with_next=with_next),
        out_shape=out_shape,
        grid=(m // tm, kdim // tk),
        in_specs=in_specs,
        out_specs=out_specs,
        scratch_shapes=[pltpu.VMEM((tm, n), F32)],
        compiler_params=_cparams(("parallel", "arbitrary")),
        name="matmul_norm_res",
    )(*args)


def _gelu_tanh(x):
    return 0.5 * x * (1.0 + jnp.tanh(math.sqrt(2.0 / math.pi) * (x + 0.044715 * (x * x * x))))


def _nsa_compress_kernel(x_ref, w1_ref, w2_ref, pe_ref, cos_ref, sin_ref, o_ref, *, use_rope):
    x = x_ref[0, 0]
    half = x.shape[1]
    n_chunk = x.shape[0]
    a = jnp.dot(x, w1_ref[0:half, :], preferred_element_type=F32)
    b = jnp.dot(x, w1_ref[half:2 * half, :], preferred_element_type=F32)
    pe = jnp.broadcast_to(pe_ref[...], (SUBLANES, pe_ref.shape[1]))
    peb = _dot_exact_rhs_general(pe, w1_ref[...])[0:1, :]
    h = a + pltpu.roll(b, n_chunk - 1, 0) + peb
    y = _bdot(_gelu_tanh(h), w2_ref[...])
    if use_rope:
        lane = lax.broadcasted_iota(jnp.int32, y.shape, 1)
        y = y * cos_ref[0] + _rope_partner(y, lane) * sin_ref[0]
    o_ref[0, 0] = y.astype(o_ref.dtype)


def _dot_exact_rhs_general(x, w_bf):
    hi, mid, lo = _split3(x)
    d = lambda p: jnp.dot(p, w_bf, preferred_element_type=F32)
    return d(hi) + (d(mid) + d(lo))


def _nsa_compress(x2, w1, w2, pe_flat, cosc, sinc, use_rope):
    b, g, n_chunk, width = x2.shape
    dh = w2.shape[1]
    return pl.pallas_call(
        functools.partial(_nsa_compress_kernel, use_rope=use_rope),
        out_shape=jax.ShapeDtypeStruct((b, g, n_chunk, dh), BF16),
        grid=(b, g),
        in_specs=[
            pl.BlockSpec((1, 1, n_chunk, width), lambda i, j: (i, j, 0, 0)),
            pl.BlockSpec(w1.shape, lambda i, j: (0, 0)),
            pl.BlockSpec(w2.shape, lambda i, j: (0, 0)),
            pl.BlockSpec(pe_flat.shape, lambda i, j: (0, 0)),
            pl.BlockSpec((1, n_chunk, dh), lambda i, j: (i, 0, 0)),
            pl.BlockSpec((1, n_chunk, dh), lambda i, j: (i, 0, 0)),
        ],
        out_specs=pl.BlockSpec((1, 1, n_chunk, dh), lambda i, j: (i, j, 0, 0)),
        compiler_params=_cparams(("parallel", "parallel")),
        name="nsa_compress",
    )(x2, w1, w2, pe_flat, cosc, sinc)


def _nsa_cmp_kernel(q_ref, kc_ref, vc_ref, ov_ref, o_ref, sel_ref, *, tq, top, n_sel):
    i = pl.program_id(2)
    kc = kc_ref[0, 0]
    vc = vc_ref[0, 0]
    ncp = kc.shape[0]
    t = i * tq + lax.broadcasted_iota(jnp.int32, (tq, ncp), 0)
    c = lax.broadcasted_iota(jnp.int32, (tq, ncp), 1)
    valid = (c * NSA_CMP_STRIDE + (NSA_CMP_LEN - 1)) <= t
    psum = jnp.zeros((tq, ncp), F32)
    for h in range(NSA_HPG):
        q = q_ref[0, :, h * HEAD_DIM:(h + 1) * HEAD_DIM]
        s = jnp.where(valid, _dot_nt(q, kc), NEG)
        m = jnp.max(s, axis=1, keepdims=True)
        e = jnp.where(valid, jnp.exp2(s - m), 0.0)
        l = jnp.sum(e, axis=1, keepdims=True)
        p = e / jnp.where(l > 0.0, l, 1.0)
        o_ref[0, :, h * HEAD_DIM:(h + 1) * HEAD_DIM] = _bdot(p, vc).astype(o_ref.dtype)
        psum = psum + p
    imp = jnp.transpose(_dot_exact_rhs(psum, ov_ref[...]))[0:n_sel]
    jj = lax.broadcasted_iota(jnp.int32, (n_sel, tq), 0)
    blk_t = jnp.right_shift(i * tq + lax.broadcasted_iota(jnp.int32, (n_sel, tq), 1), SEL_SHIFT)
    forced = (jj == 0) | (jj == blk_t) | (jj == blk_t - 1)
    imp = jnp.where(forced, 1e9, jnp.where(jj > blk_t, -1.0, imp))
    ng = n_sel // SUBLANES
    groups = [imp[g * SUBLANES:(g + 1) * SUBLANES] for g in range(ng)]
    ranks = [jnp.zeros((SUBLANES, tq), F32) for _ in range(ng)]
    sub = lax.broadcasted_iota(jnp.int32, (SUBLANES, tq), 0)
    for ii in range(n_sel):
        gi, ri = divmod(ii, SUBLANES)
        row = jnp.broadcast_to(imp[ii:ii + 1, :], (SUBLANES, tq))
        for g in range(ng):
            if g > gi:
                beats = row >= groups[g]
            elif g < gi:
                beats = row > groups[g]
            else:
                beats = (row > groups[g]) | ((row == groups[g]) & (sub > ri))
            ranks[g] = ranks[g] + jnp.where(beats, 1.0, 0.0)
    chosen = jnp.where(jnp.concatenate(ranks, axis=0) < float(top), 1.0, 0.0)
    pad = sel_ref.shape[3] - n_sel
    if pad:
        chosen = jnp.concatenate([chosen, jnp.zeros((pad, tq), F32)], axis=0)
    sel_ref[0, 0] = jnp.transpose(chosen).astype(sel_ref.dtype)


def _nsa_cmp(proj_rope, kcmp, vcmp, overlap, n_sel, tq):
    b, s, _ = proj_rope.shape
    g = kcmp.shape[1]
    ncp = kcmp.shape[2]
    n_pad = overlap.shape[1]
    top = min(NSA_SEL_TOPK, n_sel)
    qw = NSA_HPG * HEAD_DIM
    return pl.pallas_call(
        functools.partial(_nsa_cmp_kernel, tq=tq, top=top, n_sel=n_sel),
        out_shape=(jax.ShapeDtypeStruct((b, s, NSA_WIDTH), BF16),
                   jax.ShapeDtypeStruct((b, g, s, n_pad), BF16)),
        grid=(b, g, s // tq),
        in_specs=[
            pl.BlockSpec((1, tq, qw), lambda bi, gi, i: (bi, i, ROPE_Q // qw + gi)),
            pl.BlockSpec((1, 1, ncp, HEAD_DIM), lambda bi, gi, i: (bi, gi, 0, 0)),
            pl.BlockSpec((1, 1, ncp, HEAD_DIM), lambda bi, gi, i: (bi, gi, 0, 0)),
            pl.BlockSpec(overlap.shape, lambda bi, gi, i: (0, 0)),
        ],
        out_specs=(pl.BlockSpec((1, tq, qw), lambda bi, gi, i: (bi, i, gi)),
                   pl.BlockSpec((1, 1, tq, n_pad), lambda bi, gi, i: (bi, gi, i, 0))),
        compiler_params=_cparams(("parallel", "parallel", "parallel")),
        name="nsa_cmp_select",
    )(proj_rope, kcmp, vcmp, overlap)


def _nsa_flash_kernel(qi_ref, kb_ref, first_ref, *refs, mode, tq, tk, sub):
    if mode == "sel":
        q_ref, k_ref, v_ref, sel_ref, hot_ref, o_ref, m_ref, l_ref, acc_ref = refs
    else:
        q_ref, k_ref, v_ref, o_ref, m_ref, l_ref, acc_ref = refs
    n = pl.program_id(2)
    i = qi_ref[n]
    kb = kb_ref[n]

    @pl.when(first_ref[n] == 1)
    def _():
        m_ref[...] = jnp.full_like(m_ref, NEG)
        l_ref[...] = jnp.zeros_like(l_ref)
        acc_ref[...] = jnp.zeros_like(acc_ref)

    ngrp = k_ref.shape[2] // HEAD_DIM

    def tile(rows, cols, valid):
        for gi in range(ngrp):
            k = k_ref[0, cols, gi * HEAD_DIM:(gi + 1) * HEAD_DIM]
            if mode == "sel":
                penalty = ((sel_ref[0, gi, rows, :].astype(F32) - 1.0) * (-NEG)).astype(BF16)
                k = jnp.concatenate([k, hot_ref[cols, :]], axis=1)
            v_ones = jnp.concatenate([v_ref[0, cols, gi * HEAD_DIM:(gi + 1) * HEAD_DIM],
                                      jnp.ones((sub, LANES), BF16)], axis=1)
            for hd in range(NSA_HPG):
                slot = gi * NSA_HPG + hd
                q = q_ref[0, rows, slot * HEAD_DIM:(slot + 1) * HEAD_DIM]
                if mode == "sel":
                    q = jnp.concatenate([q, penalty], axis=1)
                s = _dot_nt(q, k)
                if valid is not None:
                    s = jnp.where(valid, s, NEG)
                m_old = m_ref[slot, rows, :]
                m_new = jnp.maximum(m_old, jnp.max(s, axis=1, keepdims=True))
                alpha = jnp.exp2(m_old - m_new)
                p = jnp.exp2(s - jnp.concatenate([m_new] * (sub // LANES), axis=1))
                pv = jnp.dot(p.astype(BF16), v_ones, preferred_element_type=F32)
                l_ref[slot, rows, :] = alpha * l_ref[slot, rows, :] + pv[:, HEAD_DIM:]
                acc_ref[slot, rows, :] = alpha * acc_ref[slot, rows, :] + pv[:, 0:HEAD_DIM]
                m_ref[slot, rows, :] = m_new

    def step(diag):
        lower = lax.broadcasted_iota(jnp.int32, (sub, sub), 0) >= lax.broadcasted_iota(jnp.int32, (sub, sub), 1)
        for qs in range(tq // sub):
            for ks in range(tk // sub):
                rows = slice(qs * sub, (qs + 1) * sub)
                cols = slice(ks * sub, (ks + 1) * sub)
                if mode == "win":
                    d = ((i * tq + qs * sub + lax.broadcasted_iota(jnp.int32, (sub, sub), 0))
                         - (kb * tk + ks * sub + lax.broadcasted_iota(jnp.int32, (sub, sub), 1)))
                    tile(rows, cols, (d >= 0) & (d < NSA_WINDOW))
                elif not diag or ks < qs:
                    tile(rows, cols, None)
                elif ks == qs:
                    tile(rows, cols, lower)

    if mode == "sel":
        pl.when(kb < i)(functools.partial(step, False))
        pl.when(kb == i)(functools.partial(step, True))
    else:
        step(True)

    @pl.when(kb == i)
    def _():
        for slot in range(ngrp * NSA_HPG):
            o_ref[0, :, slot * HEAD_DIM:(slot + 1) * HEAD_DIM] = (acc_ref[slot] / l_ref[slot]).astype(o_ref.dtype)


def _nsa_flash(proj_rope, proj_plain, sel, mode, tq, tk, sub):
    b, s, _ = proj_rope.shape
    g = NSA_GROUPS
    gpb = g
    qw = gpb * NSA_HPG * HEAD_DIM
    kw = gpb * HEAD_DIM
    nq = s // tq
    assert tq == tk
    if mode == "sel":
        back = nq
        koff, voff = ROPE_KS, PL_VS
    else:
        assert NSA_WINDOW % tk == 0
        back = NSA_WINDOW // tk
        koff, voff = ROPE_KW, PL_VW
    assert g % gpb == 0 and ROPE_Q % qw == 0 and koff % kw == 0 and voff % kw == 0
    qi, kb, first = _causal_pairs(nq, back)
    in_specs = [
        pl.BlockSpec((1, tq, qw), lambda bi, gi, n, qi, kb, fs: (bi, qi[n], ROPE_Q // qw + gi)),
        pl.BlockSpec((1, tk, kw), lambda bi, gi, n, qi, kb, fs: (bi, kb[n], koff // kw + gi)),
        pl.BlockSpec((1, tk, kw), lambda bi, gi, n, qi, kb, fs: (bi, kb[n], voff // kw + gi)),
    ]
    args = [proj_rope, proj_rope, proj_plain]
    if mode == "sel":
        n_pad = sel.shape[3]
        in_specs.append(pl.BlockSpec((1, gpb, tq, n_pad), lambda bi, gi, n, qi, kb, fs: (bi, gi, qi[n], 0)))
        in_specs.append(pl.BlockSpec((tk, n_pad), lambda bi, gi, n, qi, kb, fs: (kb[n], 0)))
        block_of_key = jnp.arange(s, dtype=jnp.int32)[:, None] // NSA_SEL_BLOCK
        onehot = (block_of_key == jnp.arange(n_pad, dtype=jnp.int32)[None, :]).astype(BF16)
        args += [sel, onehot]
    return pl.pallas_call(
        functools.partial(_nsa_flash_kernel, mode=mode, tq=tq, tk=tk, sub=sub),
        out_shape=jax.ShapeDtypeStruct((b, s, NSA_WIDTH), BF16),
        grid_spec=pltpu.PrefetchScalarGridSpec(
            num_scalar_prefetch=3,
            grid=(b, g // gpb, qi.shape[0]),
            in_specs=in_specs,
            out_specs=pl.BlockSpec((1, tq, qw), lambda bi, gi, n, qi, kb, fs: (bi, qi[n], gi)),
            scratch_shapes=[
                pltpu.VMEM((gpb * NSA_HPG, tq, LANES), F32),
                pltpu.VMEM((gpb * NSA_HPG, tq, LANES), F32),
                pltpu.VMEM((gpb * NSA_HPG, tq, HEAD_DIM), F32),
            ],
        ),
        compiler_params=_cparams(("parallel", "parallel", "arbitrary")),
        name="nsa_flash_" + mode,
    )(qi, kb, first, *args)


def _causal_pairs(nq, back):
    qi, kb, first = [], [], []
    for i in range(nq):
        lo = max(0, i - back)
        for j in range(lo, i + 1):
            qi.append(i)
            kb.append(j)
            first.append(1 if j == lo else 0)
    as_i32 = lambda z: jnp.asarray(np.asarray(z, np.int32))
    return as_i32(qi), as_i32(kb), as_i32(first)


def _diff_flash_kernel(qi_ref, kb_ref, first_ref, q_ref, k_ref, v_ref, lam_ref, sub_ref, o_ref, m_ref, l_ref,
                       acc_ref, *, tq, tk, sub, lam_init):
    n = pl.program_id(2)
    i = qi_ref[n]
    kb = kb_ref[n]

    @pl.when(first_ref[n] == 1)
    def _():
        m_ref[...] = jnp.full_like(m_ref, NEG)
        l_ref[...] = jnp.zeros_like(l_ref)
        acc_ref[...] = jnp.zeros_like(acc_ref)

    hpb = o_ref.shape[2] // DIFF_VDIM

    def tile(rows, cols, valid):
        for hd in range(hpb):
            v = v_ref[0, cols, hd * DIFF_VDIM:(hd + 1) * DIFF_VDIM]
            for mp in range(2):
                slot = 2 * hd + mp
                q = q_ref[0, rows, slot * HEAD_DIM:(slot + 1) * HEAD_DIM]
                k = k_ref[0, cols, slot * HEAD_DIM:(slot + 1) * HEAD_DIM]
                s = _dot_nt(q, k)
                if valid is not None:
                    s = jnp.where(valid, s, NEG)
                m_old = m_ref[slot, rows, :]
                m_new = jnp.maximum(m_old, jnp.max(s, axis=1, keepdims=True))
                alpha = jnp.exp2(m_old - m_new)
                p = jnp.exp2(s - jnp.concatenate([m_new] * (sub // LANES), axis=1))
                l_ref[slot, rows, :] = alpha * l_ref[slot, rows, :] + jnp.sum(p, axis=1, keepdims=True)
                acc_ref[slot, rows, :] = (jnp.concatenate([alpha] * (DIFF_VDIM // LANES), axis=1)
                                          * acc_ref[slot, rows, :] + _bdot(p, v))
                m_ref[slot, rows, :] = m_new

    def step(masked):
        lower = lax.broadcasted_iota(jnp.int32, (sub, sub), 0) >= lax.broadcasted_iota(jnp.int32, (sub, sub), 1)
        for qs in range(tq // sub):
            for ks in range(tk // sub):
                rows = slice(qs * sub, (qs + 1) * sub)
                cols = slice(ks * sub, (ks + 1) * sub)
                if not masked or ks < qs:
                    tile(rows, cols, None)
                elif ks == qs:
                    tile(rows, cols, lower)

    pl.when(kb < i)(functools.partial(step, False))
    pl.when(kb == i)(functools.partial(step, True))

    @pl.when(kb == i)
    def _():
        lam = lam_ref[...]
        lam_full = (jnp.exp(jnp.sum(lam[0:1, :] * lam[1:2, :], axis=1, keepdims=True))
                    - jnp.exp(jnp.sum(lam[2:3, :] * lam[3:4, :], axis=1, keepdims=True)) + lam_init)
        wide = lambda x: jnp.concatenate([x] * (DIFF_VDIM // LANES), axis=1)
        for hd in range(hpb):
            o = (acc_ref[2 * hd] / wide(l_ref[2 * hd])
                 - lam_full * (acc_ref[2 * hd + 1] / wide(l_ref[2 * hd + 1])))
            ms = jnp.mean(o * o, axis=-1, keepdims=True)
            o = o * lax.rsqrt(ms + 1e-5) * sub_ref[...]
            o_ref[0, :, hd * DIFF_VDIM:(hd + 1) * DIFF_VDIM] = (o * (1.0 - lam_init)).astype(o_ref.dtype)


def _diff_flash(proj_rope, proj_plain, lam, subln, layer, tq, tk, sub):
    b, s, _ = proj_rope.shape
    assert tq == tk
    nq = s // tq
    lam_init = 0.8 - 0.6 * math.exp(-0.3 * layer)
    hpb = DIFF_HEADS
    w = hpb * DIFF_VDIM
    assert DIFF_HEADS % hpb == 0 and ROPE_DQ % w == 0 and ROPE_DK % w == 0 and PL_DV % w == 0
    qi, kb, first = _causal_pairs(nq, nq)
    return pl.pallas_call(
        functools.partial(_diff_flash_kernel, tq=tq, tk=tk, sub=sub, lam_init=lam_init),
        out_shape=jax.ShapeDtypeStruct((b, s, DIFF_WIDTH), BF16),
        grid_spec=pltpu.PrefetchScalarGridSpec(
            num_scalar_prefetch=3,
            grid=(b, DIFF_HEADS // hpb, qi.shape[0]),
            in_specs=[
                pl.BlockSpec((1, tq, w), lambda bi, h, n, qi, kb, fs: (bi, qi[n], ROPE_DQ // w + h)),
                pl.BlockSpec((1, tk, w), lambda bi, h, n, qi, kb, fs: (bi, kb[n], ROPE_DK // w + h)),
                pl.BlockSpec((1, tk, w), lambda bi, h, n, qi, kb, fs: (bi, kb[n], PL_DV // w + h)),
                pl.BlockSpec(lam.shape, lambda bi, h, n, qi, kb, fs: (0, 0)),
                pl.BlockSpec((1, DIFF_VDIM), lambda bi, h, n, qi, kb, fs: (0, 0)),
            ],
            out_specs=pl.BlockSpec((1, tq, w), lambda bi, h, n, qi, kb, fs: (bi, qi[n], h)),
            scratch_shapes=[
                pltpu.VMEM((2 * hpb, tq, LANES), F32),
                pltpu.VMEM((2 * hpb, tq, LANES), F32),
                pltpu.VMEM((2 * hpb, tq, DIFF_VDIM), F32),
            ],
        ),
        compiler_params=_cparams(("parallel", "parallel", "arbitrary")),
        name="diff_flash",
    )(qi, kb, first, proj_rope, proj_rope, proj_plain, lam, subln[None, :])


def _bmm(a, b):
    return jnp.einsum("umk,ukn->umn", a.astype(BF16), b.astype(BF16), preferred_element_type=F32)


def _bmm_nt(a, b):
    return jnp.einsum("umk,unk->umn", a.astype(BF16), b.astype(BF16), preferred_element_type=F32)


def _rwkv_scan_kernel(r_ref, k_ref, v_ref, gd_ref, wa_ref, mur_ref, muk_ref, muv_ref, mugd_ref, muwa_ref,
                      wdec_ref, wrate_ref, gup_ref, w0_ref, a0_ref, kk_ref, ka_ref, rk_ref, lnw_ref, lnb_ref,
                      o_ref, st_ref, last_ref, lastgd_ref, *, ts, chunk):
    @pl.when(pl.program_id(2) == 0)
    def _():
        st_ref[...] = jnp.zeros_like(st_ref)
        last_ref[...] = jnp.zeros_like(last_ref)
        lastgd_ref[...] = jnp.zeros_like(lastgd_ref)

    def shifted(x_ref, mu_ref, carry):
        x = x_ref[0]
        row = lax.broadcasted_iota(jnp.int32, x.shape, 0)
        prev = jnp.where(row == 0, carry, pltpu.roll(x, 1, 0))
        return x + (prev - x) * mu_ref[...]

    r = shifted(r_ref, mur_ref, last_ref[0, 0:1, :])
    k = shifted(k_ref, muk_ref, last_ref[1, 0:1, :])
    v = shifted(v_ref, muv_ref, last_ref[2, 0:1, :])
    wa = shifted(wa_ref, muwa_ref, last_ref[3, 0:1, :])
    gd = shifted(gd_ref, mugd_ref, lastgd_ref[0:1, :])
    for n, ref in enumerate((r_ref, k_ref, v_ref, wa_ref)):
        last_ref[n, 0:1, :] = ref[0, ts - 1:ts, :]
    lastgd_ref[0:1, :] = gd_ref[0, ts - 1:ts, :]

    wa = jnp.where(lax.broadcasted_iota(jnp.int32, wa.shape, 1) < RWKV_W_LORA, jnp.tanh(wa), wa)
    lw = -math.exp(-0.5) * jax.nn.sigmoid(w0_ref[...] + _bdot(wa, wdec_ref[...]))
    a = jax.nn.sigmoid(a0_ref[...] + _bdot(wa, wrate_ref[...]))
    g = _bdot(jax.nn.sigmoid(gd), gup_ref[...])

    c = chunk
    nc = ts // c
    head0 =lax.broadcasted_iota(jnp.int32, (ts, LANES), 1) < RWKV_HEAD

    def seg_sum(x):
        s0 = jnp.sum(jnp.where(head0, x, 0.0), axis=1, keepdims=True)
        s1 = jnp.sum(jnp.where(head0, 0.0, x), axis=1, keepdims=True)
        return jnp.where(head0, s0, s1)

    kk = k * kk_ref[...]
    kap = kk / jnp.maximum(jnp.sqrt(seg_sum(kk * kk)), 1e-12)
    kmod = k * (1.0 + (a - 1.0) * ka_ref[...])
    bvec = kap * a

    row_in_chunk = lax.broadcasted_iota(jnp.int32, (ts, LANES), 0) & (c - 1)
    cum = lw
    step = 1
    while step < c:
        cum = cum + jnp.where(row_in_chunk >= step, pltpu.roll(cum, step, 0), 0.0)
        step *= 2
    g_incl = jnp.exp(cum)
    g_inv = jnp.exp(-cum)
    split = lambda x: x.reshape(nc, c, LANES)
    g_last = [g_incl[(ch + 1) * c - 1:(ch + 1) * c, :] for ch in range(nc)]
    g_last_rows = jnp.concatenate([jnp.broadcast_to(gl, (c, LANES)) for gl in g_last], axis=0)
    kh = split(kmod * g_inv)
    bh = split(bvec * g_inv)
    kg = split(kmod * g_inv * g_last_rows)
    bg = split(bvec * g_inv * g_last_rows)
    kaph = kap * jnp.exp(cum - lw)
    rh = r * g_incl
    v3 = split(v)

    assert 2 * c == LANES
    h0 = lax.broadcasted_iota(jnp.int32, (nc, c, LANES), 2) < RWKV_HEAD
    stack = lambda y: jnp.concatenate([jnp.where(h0, y, 0.0), jnp.where(h0, 0.0, y)], axis=1)
    kap3 = split(kaph)
    r3 = split(rh)
    gram = _bmm_nt(jnp.concatenate([kap3, r3], axis=1),
                   jnp.concatenate([stack(bh), stack(kh)], axis=1))
    row = lax.broadcasted_iota(jnp.int32, (c, 4 * c), 0)
    col = lax.broadcasted_iota(jnp.int32, (c, 4 * c), 1) & (c - 1)
    top = jnp.where(row > col, gram[:, 0:c, :], 0.0)
    abk = jnp.where(row >= col, gram[:, c:2 * c, :], 0.0)
    lb = top[:, :, 0:LANES]
    lk = top[:, :, LANES:2 * LANES]
    tm = -lb
    q = _bmm(lb, stack(lb))
    n = 2
    while True:
        tm = tm + q + _bmm(tm, stack(q))
        n *= 2
        if n >= c:
            break
        q = _bmm(q, stack(q))
    kap_t = kap3 + _bmm(tm, stack(kap3))
    lkv = _bmm(lk, stack(v3))
    z0 = -(lkv + _bmm(tm, stack(lkv)))
    r_t = r3 - _bmm(abk[:, :, 0:LANES], stack(kap_t))
    y0 = _bmm(abk, jnp.concatenate([stack(z0), stack(v3)], axis=1))

    br = lax.broadcasted_iota(jnp.int32, (LANES, LANES), 0)
    bc = lax.broadcasted_iota(jnp.int32, (LANES, LANES), 1)
    blockdiag = (br < RWKV_HEAD) == (bc < RWKV_HEAD)
    state = st_ref[...]
    ys = []
    for ch in range(nc):
        trans = jnp.where(blockdiag, -_dot_tn(bg[ch], kap_t[ch]), 0.0)
        trans = trans + jnp.where(br == bc, jnp.broadcast_to(g_last[ch], (LANES, LANES)), 0.0)
        inject = jnp.where(blockdiag, _dot_tn(jnp.concatenate([bg[ch], kg[ch]], axis=0),
                                              jnp.concatenate([z0[ch], v3[ch]], axis=0)), 0.0)
        ys.append(_bdot(r_t[ch], state) + y0[ch])
        state = _dot3(trans, state) + inject
    st_ref[...] = state
    y = jnp.concatenate(ys, axis=0)

    mean = seg_sum(y) * (1.0 / RWKV_HEAD)
    yc = y - mean
    var = seg_sum(yc * yc) * (1.0 / RWKV_HEAD)
    yn = yc * lax.rsqrt(var + RWKV_LNX_EPS) * lnw_ref[...] + lnb_ref[...]
    bonus = seg_sum(r * kmod * rk_ref[...]) * v
    o_ref[0] = ((yn + bonus) * g).astype(o_ref.dtype)


def _rwkv_scan(proj3, mu_p, wwa, gup, w0, a0, k_k, k_a, r_k, lnx_w, lnx_b, ts=1024, chunk=64):
    b, s, _ = proj3.shape
    w = RWKV_WIDTH
    ts = min(ts, s)
    assert s % ts == 0 and ts % chunk == 0
    npair = w // LANES
    col = lambda off: pl.BlockSpec((1, ts, LANES), lambda bi, j, t: (bi, t, off // LANES + j))
    mucol = lambda off: pl.BlockSpec((1, LANES), lambda bi, j, t: (0, off // LANES + j))
    vec = pl.BlockSpec((1, LANES), lambda bi, j, t: (0, j))
    in_specs = [
        col(0), col(w), col(2 * w),
        pl.BlockSpec((1, ts, RW_G_PAD), lambda bi, j, t: (bi, t, RW_GD // RW_G_PAD)),
        pl.BlockSpec((1, ts, RW_LORA_PAD), lambda bi, j, t: (bi, t, RW_WA // RW_LORA_PAD)),
        mucol(0), mucol(w), mucol(2 * w),
        pl.BlockSpec((1, RW_G_PAD), lambda bi, j, t: (0, RW_GD // RW_G_PAD)),
        pl.BlockSpec((1, RW_LORA_PAD), lambda bi, j, t: (0, RW_WA // RW_LORA_PAD)),
        pl.BlockSpec((RW_LORA_PAD, LANES), lambda bi, j, t: (0, j)),
        pl.BlockSpec((RW_LORA_PAD, LANES), lambda bi, j, t: (0, npair + j)),
        pl.BlockSpec((RW_G_PAD, LANES), lambda bi, j, t: (0, j)),
    ] + [vec] * 7
    row = lambda z: z[None, :]
    return pl.pallas_call(
        functools.partial(_rwkv_scan_kernel, ts=ts, chunk=chunk),
        out_shape=jax.ShapeDtypeStruct((b, s, w), BF16),
        grid=(b, npair, s // ts),
        in_specs=in_specs,
        out_specs=pl.BlockSpec((1, ts, LANES), lambda bi, j, t: (bi, t, j)),
        scratch_shapes=[
            pltpu.VMEM((LANES, LANES), F32),
            pltpu.VMEM((4, SUBLANES, LANES), F32),
            pltpu.VMEM((SUBLANES, RW_G_PAD), F32),
        ],
        compiler_params=_cparams(("parallel", "parallel", "arbitrary")),
        name="rwkv_scan",
    )(proj3, proj3, proj3, proj3, proj3, mu_p, mu_p, mu_p, mu_p, mu_p, wwa, wwa, gup,
      row(w0), row(a0), row(k_k), row(k_a), row(r_k), row(lnx_w), row(lnx_b))


def _merge_kernel(oc_ref, os_ref, ow_ref, ng_ref, yd_ref, yr_ref, wb_ref, g0_ref, g1_ref, g2_ref, bias_ref,
                  o_ref, yn_ref):
    @pl.when(pl.program_id(1) == 0)
    def _():
        gates = jax.nn.sigmoid(ng_ref[...])
        for hd in range(NSA_HEADS):
            sl = slice(hd * HEAD_DIM, (hd + 1) * HEAD_DIM)
            o = (gates[:, 3 * hd:3 * hd + 1] * oc_ref[:, sl]
                 + gates[:, 3 * hd + 1:3 * hd + 2] * os_ref[:, sl]
                 + gates[:, 3 * hd + 2:3 * hd + 3] * ow_ref[:, sl])
            yn_ref[:, sl] = o.astype(BF16)

    branches = (yn_ref, yd_ref, yr_ref)
    graw = (g0_ref, g1_ref, g2_ref)
    acc = None
    for bi in range(N_BRANCH):
        gate = jax.nn.sigmoid(graw[bi][...] + bias_ref[bi])
        term = gate * jnp.dot(branches[bi][...], wb_ref[bi], preferred_element_type=F32)
        acc = term if acc is None else acc + term
    o_ref[...] = acc.astype(o_ref.dtype)


def _merge(o_cmp, o_slc, o_win, proj_f32, proj_plain, y_diff, y_rwkv, w_branch, b_gate, tm=1024, tn=512):
    m = o_cmp.shape[0]
    n = w_branch.shape[2]
    tm = min(tm, m)
    assert m % tm == 0 and n % tn == 0 and PL_BG % tn == 0
    bw = BRANCH_WIDTH
    rowblk = lambda: pl.BlockSpec((tm, bw), lambda i, j: (i, 0))
    gate_spec = lambda bi: pl.BlockSpec((tm, tn), lambda i, j: (i, PL_BG // tn + bi * (n // tn) + j))
    return pl.pallas_call(
        _merge_kernel,
        out_shape=jax.ShapeDtypeStruct((m, n), BF16),
        grid=(m // tm, n // tn),
        in_specs=[
            rowblk(), rowblk(), rowblk(),
            pl.BlockSpec((tm, LANES), lambda i, j: (i, F_NG // LANES)),
            rowblk(), rowblk(),
            pl.BlockSpec((N_BRANCH, bw, tn), lambda i, j: (0, 0, j)),
            gate_spec(0), gate_spec(1), gate_spec(2),
            pl.BlockSpec((N_BRANCH, 1, tn), lambda i, j: (0, 0, j)),
        ],
        out_specs=pl.BlockSpec((tm, tn), lambda i, j: (i, j)),
        scratch_shapes=[pltpu.VMEM((tm, bw), BF16)],
        compiler_params=_cparams(("parallel", "arbitrary")),
        name="gated_merge",
    )(o_cmp, o_slc, o_win, proj_f32, y_diff, y_rwkv, w_branch, proj_plain, proj_plain, proj_plain,
      b_gate[:, None, :])


def _pack_layer_weights(w_in, w_gate, rwkv_mu, w_up_lora, a_up_lora, g_up_lora):
    d = w_in.shape[0]
    nsa_sizes = (NSA_WIDTH,) + (NSA_KV,) * 6 + (3 * NSA_HEADS,)
    diff_sizes = (DIFF_QK, DIFF_QK, DIFF_WIDTH)
    rw_sizes = (RWKV_WIDTH,) * 3 + (RWKV_W_LORA, RWKV_A_LORA, RWKV_G_LORA)
    offs = np.cumsum((0,) + nsa_sizes + diff_sizes + rw_sizes)
    seg = [w_in[:, offs[i]:offs[i + 1]] for i in range(len(offs) - 1)]
    q, kc, vc, ks, vs, kw, vw, ng, dq, dk, dv, rr, rk, rv, wd, ad, gd = seg
    w_rope = jnp.concatenate([q, dq, dk, ks, kw], axis=1).astype(BF16)
    w_plain = jnp.concatenate([kc, vc, vs, vw, dv] + [w_gate[bi] for bi in range(N_BRANCH)], axis=1).astype(BF16)
    zpad = lambda n: jnp.zeros((d, n), w_in.dtype)
    w_f32 = jnp.concatenate(
        [rr, rk, rv, gd, zpad(RW_G_PAD - RWKV_G_LORA), wd, ad, ng, zpad(LANES - 3 * NSA_HEADS)],
        axis=1).astype(BF16)
    mu_rkv, mu_wa, mu_gd = jnp.split(rwkv_mu, [3 * RWKV_WIDTH, 3 * RWKV_WIDTH + RW_LORA_PAD])
    mu_p = jnp.concatenate([mu_rkv, mu_gd, jnp.zeros((RW_G_PAD - RWKV_G_LORA,), rwkv_mu.dtype), mu_wa])[None, :]
    zz = jnp.zeros((RWKV_W_LORA, RWKV_WIDTH), w_up_lora.dtype)
    wwa = jnp.concatenate([jnp.concatenate([w_up_lora, zz], axis=1),
                           jnp.concatenate([zz, a_up_lora], axis=1)], axis=0).astype(BF16)
    gup = jnp.concatenate([g_up_lora, jnp.zeros((RW_G_PAD - RWKV_G_LORA, RWKV_WIDTH), g_up_lora.dtype)],
                          axis=0).astype(BF16)
    return w_rope, w_plain, w_f32, mu_p, wwa, gup


def _overlap_matrix(n_cp, n_sel):
    c_start = np.arange(n_cp) * NSA_CMP_STRIDE
    c_end = c_start + NSA_CMP_LEN - 1
    j_start = np.arange(n_sel) * NSA_SEL_BLOCK
    ov = (c_start[:, None] <= j_start[None, :] + NSA_SEL_BLOCK - 1) & (c_end[:, None] >= j_start[None, :])
    ov = np.pad(ov.astype(np.float32), ((0, 0), (0, -n_sel % LANES)))
    return jnp.asarray(ov).astype(BF16)


def _chunk_rows(z, groups):
    b, s, _ = z.shape
    z = z.reshape(b, s // NSA_CMP_STRIDE, NSA_CMP_STRIDE, groups, HEAD_DIM)
    return z.transpose(0, 3, 1, 2, 4).reshape(b, groups, s // NSA_CMP_STRIDE, NSA_CMP_STRIDE * HEAD_DIM)


def _nsa_branches(proj_rope, proj_plain, positions, cmp_pos, cmp_w1, cmp_w2, tq):
    b, s, _ = proj_rope.shape
    n_chunk = s // NSA_CMP_STRIDE
    n_sel = s // NSA_SEL_BLOCK
    kc2 = _chunk_rows(proj_plain[:, :, PL_KC:PL_KC + NSA_KV], NSA_GROUPS)
    vc2 = _chunk_rows(proj_plain[:, :, PL_VC:PL_VC + NSA_KV], NSA_GROUPS)
    cmp_end = np.minimum(np.arange(n_chunk) * NSA_CMP_STRIDE + NSA_CMP_LEN - 1, s - 1)
    pos_c = jnp.take(positions, jnp.asarray(cmp_end), axis=1)
    cosc, sinc = _rope_tables(pos_c.reshape(-1))
    cosc = cosc.reshape(b, n_chunk, LANES)
    sinc = sinc.reshape(b, n_chunk, LANES)
    w1 = cmp_w1.astype(BF16)
    w2 = cmp_w2.astype(BF16)
    pe = cmp_pos.reshape(2, 1, NSA_CMP_LEN * HEAD_DIM)
    kcmp = _nsa_compress(kc2, w1[0], w2[0], pe[0], cosc, sinc, True)
    vcmp = _nsa_compress(vc2, w1[1], w2[1], pe[1], cosc, sinc, False)
    o_cmp, sel = _nsa_cmp(proj_rope, kcmp, vcmp, _overlap_matrix(n_chunk, n_sel), n_sel, tq)
    big = min(2 * tq, s)
    o_slc = _nsa_flash(proj_rope, proj_plain, sel, "sel", big, big, tq)
    o_win = _nsa_flash(proj_rope, proj_plain, None, "win", tq, tq, tq)
    return o_cmp, o_slc, o_win


def _layer(x2, b, s, layer, cosf, sinf, positions, p):
    m = b * s
    w_rope, w_plain, w_f32, mu_p, wwa, gup = _pack_layer_weights(
        p["w_in"], p["w_gate"], p["rwkv_mu"], p["rwkv_w_up"], p["rwkv_a_up"], p["rwkv_g_up"])
    scale = HEAD_DIM ** -0.5 * math.log2(math.e)
    colscale = jnp.concatenate([jnp.full((ROPE_SCALED,), scale, F32),
                                jnp.ones((ROPE_COLS - ROPE_SCALED,), F32)])[None, :]
    g_pre = p["norm_pre_mix"]
    proj_rope, u = _norm_matmul_rope(x2, g_pre, w_rope, cosf, sinf, colscale, tm=512, tn=ROPE_COLS // 2)
    proj_plain = _matmul(u, w_plain, BF16, tm=2048, tn=1024)
    proj_f32 = _matmul(u, w_f32, F32, tm=1024, tn=F32_COLS // 2)
    pr3 = proj_rope.reshape(b, s, ROPE_COLS)
    pp3 = proj_plain.reshape(b, s, PLAIN_COLS)

    tq = min(512, s)
    o_cmp, o_slc, o_win = _nsa_branches(pr3, pp3, positions, p["nsa_cmp_pos"], p["nsa_cmp_w1"], p["nsa_cmp_w2"], tq)
    big = min(2 * tq, s)
    y_diff = _diff_flash(pr3, pp3, p["diff_lambda"], p["diff_subln"], layer, big, big, tq)

    y_rwkv = _rwkv_scan(proj_f32.reshape(b, s, F32_COLS), mu_p, wwa, gup, p["rwkv_w0"], p["rwkv_a0"],
                        p["rwkv_k_k"], p["rwkv_k_a"], p["rwkv_r_k"].reshape(-1), p["rwkv_lnx_w"], p["rwkv_lnx_b"])

    merged = _merge(o_cmp.reshape(m, NSA_WIDTH), o_slc.reshape(m, NSA_WIDTH), o_win.reshape(m, NSA_WIDTH),
                    proj_f32, proj_plain, y_diff.reshape(m, DIFF_WIDTH), y_rwkv.reshape(m, RWKV_WIDTH),
                    p["w_branch"].astype(BF16), p["b_gate"])
    x2, u_mlp = _matmul_norm_res(merged, p["w_out"].astype(BF16), p["norm_post_mix"], x2,
                                 next_gain=p["norm_pre_mlp"])
    hidden = _matmul(u_mlp, p["w_up"].astype(BF16), BF16, tm=2048, tn=1024, relu2=True)
    x2 = _matmul_norm_res(hidden, p["w_down"].astype(BF16), p["norm_post_mlp"], x2)
    return x2


def kernel(x, positions, norm_pre_mix, norm_post_mix, norm_pre_mlp, norm_post_mlp, w_in, nsa_cmp_pos, nsa_cmp_w1, nsa_cmp_w2, diff_lambda, diff_subln, rwkv_mu, rwkv_w0, rwkv_w_up, rwkv_a0, rwkv_a_up, rwkv_g_up, rwkv_k_k, rwkv_k_a, rwkv_r_k, rwkv_lnx_w, rwkv_lnx_b, w_gate, b_gate, w_branch, w_out, w_up, w_down):
    b, s, d = x.shape
    depth = w_in.shape[0]
    stacked = dict(
        norm_pre_mix=norm_pre_mix, norm_post_mix=norm_post_mix, norm_pre_mlp=norm_pre_mlp,
        norm_post_mlp=norm_post_mlp, w_in=w_in, nsa_cmp_pos=nsa_cmp_pos, nsa_cmp_w1=nsa_cmp_w1,
        nsa_cmp_w2=nsa_cmp_w2, diff_lambda=diff_lambda, diff_subln=diff_subln, rwkv_mu=rwkv_mu,
        rwkv_w0=rwkv_w0, rwkv_w_up=rwkv_w_up, rwkv_a0=rwkv_a0, rwkv_a_up=rwkv_a_up, rwkv_g_up=rwkv_g_up,
        rwkv_k_k=rwkv_k_k, rwkv_k_a=rwkv_k_a, rwkv_r_k=rwkv_r_k, rwkv_lnx_w=rwkv_lnx_w,
        rwkv_lnx_b=rwkv_lnx_b, w_gate=w_gate, b_gate=b_gate, w_branch=w_branch, w_out=w_out,
        w_up=w_up, w_down=w_down)
    cosf, sinf = _rope_tables(positions.reshape(-1))
    x2 = x.reshape(b * s, d)
    for layer in range(depth):
        x2 = _layer(x2, b, s, layer, cosf, sinf, positions, {n: a[layer] for n, a in stacked.items()})
    return x2.reshape(b, s, d)
```
